```python
import jax, jax.numpy as jnp
from jax import lax
import numpy as np

D_MODEL = 1024
BATCH = 2
SEQ = 8192
DEPTH = 4
DEC_BATCH = 128
DEC_SEQ = 1
PAST_LEN = 8192
PAGE_SIZE = 128

N_EVEN = (DEPTH + 1) // 2
N_ODD = DEPTH // 2
Q_BLOCK = 128
NORM_EPS = 1e-6
CONV_WIDTH = 4
SB_HEADS = 8
SB_KV_HEADS = 4
SB_GROUP = SB_HEADS // SB_KV_HEADS
SB_HEAD_DIM = 64
LRU_WIDTH = 512
LRU_BLOCKS = 8
LRU_BLOCK_DIM = LRU_WIDTH // LRU_BLOCKS
LRU_C = 8.0
MLA_HEADS = 8
MLA_Q_RANK = 256
MLA_KV_RANK = 128
MLA_NOPE_DIM = 64
MLA_ROPE_DIM = 32
MLA_V_DIM = 64
MLA_QK_DIM = MLA_NOPE_DIM + MLA_ROPE_DIM
ROPE_THETA = 10000.0
SSD_HEADS = 8
SSD_HEAD_DIM = 64
SSD_INNER = SSD_HEADS * SSD_HEAD_DIM
SSD_GROUPS = 2
SSD_HPG = SSD_HEADS // SSD_GROUPS
SSD_STATE = 64
SSD_CHUNK = 128
SSD_CONV_DIM = SSD_INNER + 2 * SSD_GROUPS * SSD_STATE
FFN_HIDDEN = ((8 * D_MODEL // 3 + 255) // 256) * 256
AB_SIZES = (SB_HEADS * SB_HEAD_DIM, SB_KV_HEADS * SB_HEAD_DIM, SB_KV_HEADS * SB_HEAD_DIM, LRU_WIDTH, LRU_WIDTH)
AB_IN = SB_HEADS * SB_HEAD_DIM + 2 * SB_KV_HEADS * SB_HEAD_DIM + 2 * LRU_WIDTH
AB_OUT = SB_HEADS * SB_HEAD_DIM + LRU_WIDTH
CD_SIZES = (MLA_Q_RANK, MLA_KV_RANK, MLA_ROPE_DIM, SSD_INNER, SSD_CONV_DIM, SSD_HEADS)
CD_IN = MLA_Q_RANK + MLA_KV_RANK + MLA_ROPE_DIM + SSD_INNER + SSD_CONV_DIM + SSD_HEADS
CD_OUT = MLA_HEADS * MLA_V_DIM + SSD_INNER

kernel_name = 'hybrid_sb_rglru_mla_ssd_decode_step'


def split_cols(x, sizes):
    out, start = [], 0
    for s in sizes:
        out.append(x[..., start:start + s])
        start += s
    return out


def rms_norm(x, g):
    xf = x.astype(jnp.float32)
    y = xf * lax.rsqrt(jnp.mean(xf * xf, axis=-1, keepdims=True) + NORM_EPS)
    return (y * g.astype(jnp.float32)).astype(x.dtype)


def rope(x, pos):
    half = x.shape[-1] // 2
    inv = ROPE_THETA ** (-jnp.arange(half, dtype=jnp.float32) / half)
    ang = pos.astype(jnp.float32)[:, None] * inv[None, :]
    shape = (pos.shape[0],) + (1,) * (x.ndim - 3) + (half,)
    cos, sin = jnp.cos(ang).reshape(shape), jnp.sin(ang).reshape(shape)
    x1, x2 = x[..., :half].astype(jnp.float32), x[..., half:].astype(jnp.float32)
    return jnp.concatenate([x1 * cos - x2 * sin, x2 * cos + x1 * sin], axis=-1).astype(x.dtype)


def causal_conv(x, buf, w, b):
    t = x.shape[1]
    xp = jnp.concatenate([buf.astype(x.dtype), x], axis=1)
    y = b
    for i in range(CONV_WIDTH):
        y = y + xp[:, i:i + t] * w[i]
    return y, xp[:, xp.shape[1] - (CONV_WIDTH - 1):]


def linear_scan(a, b, h0):
    b = b.at[:, 0].add(a[:, 0] * h0)

    def combine(left, right):
        a_l, b_l = left
        a_r, b_r = right
        return a_r * a_l, a_r * b_l + b_r

    _, h = lax.associative_scan(combine, (a, b), axis=1)
    return h


def stick_breaking(q, k, v, q_pos, k_pos):
    z = jnp.einsum('...qhgd,...khd->...hgqk', q, k).astype(jnp.float32) * (SB_HEAD_DIM ** -0.5)
    causal = k_pos[None, :] < q_pos[:, None]
    log_keep = jnp.where(causal, jax.nn.log_sigmoid(-z), 0.0)
    tail = lax.cumsum(log_keep, axis=log_keep.ndim - 1, reverse=True) - log_keep
    w = jnp.where(causal, jnp.exp(jax.nn.log_sigmoid(z) + tail), 0.0)
    return jnp.einsum('...hgqk,...khd->...qhgd', w.astype(v.dtype), v)


def softmax_attn(q, k, v, q_pos, k_pos):
    s = jnp.einsum('...qhd,...khd->...hqk', q, k).astype(jnp.float32) * (q.shape[-1] ** -0.5)
    s = jnp.where(k_pos[None, :] <= q_pos[:, None], s, -jnp.inf)
    p = jax.nn.softmax(s, axis=-1)
    return jnp.einsum('...hqk,...khd->...qhd', p.astype(v.dtype), v)


def sweep_query_blocks(attn, q, k, v):
    nb, s = q.shape[0], q.shape[1]
    n_blk = s // Q_BLOCK
    qb = q.reshape((nb, n_blk, Q_BLOCK) + q.shape[2:]).swapaxes(0, 1)
    k_pos = jnp.arange(s)

    def one(args):
        i, q_i = args
        return attn(q_i, k, v, i * Q_BLOCK + jnp.arange(Q_BLOCK), k_pos)

    o = lax.map(one, (jnp.arange(n_blk), qb))
    return o.swapaxes(0, 1).reshape((nb, s) + o.shape[3:])


def gather_pages(cache, layer, pt):
    rows = cache[layer, pt]
    return rows.reshape((pt.shape[0] * PAGE_SIZE,) + cache.shape[3:])


def sb_attend_sample(q, k, v, cache, layer, page_table):
    t = q.shape[1]
    q_pos = PAST_LEN + jnp.arange(t)
    k_pos = jnp.arange(PAST_LEN + t)

    def one(args):
        q_i, k_i, v_i, pt = args
        past = gather_pages(cache, layer, pt)
        kk = jnp.concatenate([past[:, 0], k_i], axis=0)
        vv = jnp.concatenate([past[:, 1], v_i], axis=0)
        return stick_breaking(q_i, kk, vv, q_pos, k_pos)

    return lax.map(one, (q, k, v, page_table))


def mla_kv(latent, w_ukv, gk):
    ckv, kpe = latent[..., :MLA_KV_RANK], latent[..., MLA_KV_RANK:]
    kv = jnp.einsum('...tr,rhe->...the', ckv, w_ukv)
    k_nope, v = kv[..., :MLA_NOPE_DIM], kv[..., MLA_NOPE_DIM:]
    k_rope = jnp.broadcast_to(kpe[..., None, :], k_nope.shape[:-1] + (MLA_ROPE_DIM,))
    k = rms_norm(jnp.concatenate([k_nope, k_rope], axis=-1), gk)
    return k, v


def mla_attend_sample(q, latent, w_ukv, gk, cache, layer, page_table):
    t = q.shape[1]
    q_pos = PAST_LEN + jnp.arange(t)
    k_pos = jnp.arange(PAST_LEN + t)

    def one(args):
        q_i, lat_i, pt = args
        lat = jnp.concatenate([gather_pages(cache, layer, pt), lat_i], axis=0)
        k, v = mla_kv(lat, w_ukv, gk)
        return softmax_attn(q_i, k, v, q_pos, k_pos)

    return lax.map(one, (q, latent, page_table))


def rglru(xb, gate, conv_buf, h0, conv_w, conv_b, wa, ba, wx, bx, lam):
    f32 = jnp.float32
    xc, new_buf = causal_conv(xb, conv_buf, conv_w, conv_b)
    nb, t = xc.shape[0], xc.shape[1]
    xblk = xc.reshape(nb, t, LRU_BLOCKS, LRU_BLOCK_DIM)
    r = jax.nn.sigmoid(jnp.einsum('btki,kij->btkj', xblk, wa).reshape(nb, t, LRU_WIDTH) + ba).astype(f32)
    i = jax.nn.sigmoid(jnp.einsum('btki,kij->btkj', xblk, wx).reshape(nb, t, LRU_WIDTH) + bx).astype(f32)
    log_a = -LRU_C * r * jax.nn.softplus(-lam.astype(f32))
    a = jnp.exp(log_a)
    b = jnp.sqrt(-jnp.expm1(2.0 * log_a)) * (i * xc.astype(f32))
    h = linear_scan(a, b, h0.astype(f32))
    y = jax.nn.gelu(gate) * h.astype(gate.dtype)
    return y, new_buf, h[:, -1].astype(xb.dtype)


def ssd_scan(x, dt, a, bm, cm, h0):
    nb, t = x.shape[0], x.shape[1]
    q = SSD_CHUNK if t % SSD_CHUNK == 0 else t
    nc = t // q

    def chunk(u):
        return u.reshape((nb, nc, q) + u.shape[2:])

    x, dt, bm, cm = chunk(x), chunk(dt), chunk(bm), chunk(cm)
    cum = jnp.cumsum(dt * a, axis=2)
    causal = (jnp.arange(q)[:, None] >= jnp.arange(q)[None, :])[:, :, None, None]
    seg = cum[:, :, :, None] - cum[:, :, None]
    decay = jnp.exp(jnp.where(causal, seg, -jnp.inf))
    cb = jnp.einsum('bclgn,bcsgn->bclsg', cm, bm)
    mix = cb[..., None] * decay * dt[:, :, None]
    y_diag = jnp.einsum('bclsgr,bcsgrp->bclgrp', mix, x)
    to_end = jnp.exp(cum[:, :, -1:] - cum) * dt
    chunk_states = jnp.einsum('bclgn,bclgr,bclgrp->bcgrpn', bm, to_end, x)
    h_end = linear_scan(jnp.exp(cum[:, :, -1])[..., None, None], chunk_states, h0)
    h_start = jnp.concatenate([h0[:, None], h_end[:, :-1]], axis=1)
    y_off = jnp.einsum('bclgn,bcgrpn,bclgr->bclgrp', cm, h_start, jnp.exp(cum))
    y = (y_diag + y_off).reshape((nb, t) + x.shape[3:])
    return y, h_end[:, -1]


def ab_mixer(h, wts, conv_buf, h0, attend):
    w_in, w_out, gq, gk, conv_w, conv_b, wa, ba, wx, bx, lam = wts
    nb, t = h.shape[0], h.shape[1]
    q, k, v, xb, gate = split_cols(h @ w_in, AB_SIZES)
    q = rms_norm(q.reshape(nb, t, SB_HEADS, SB_HEAD_DIM), gq).reshape(nb, t, SB_KV_HEADS, SB_GROUP, SB_HEAD_DIM)
    k = rms_norm(k.reshape(nb, t, SB_KV_HEADS, SB_HEAD_DIM), gk)
    v = v.reshape(nb, t, SB_KV_HEADS, SB_HEAD_DIM)
    o_a = attend(q, k, v)
    y_b, new_buf, h_last = rglru(xb, gate, conv_buf, h0, conv_w, conv_b, wa, ba, wx, bx, lam)
    o = jnp.concatenate([o_a.reshape(nb, t, -1), y_b], axis=-1) @ w_out
    return o, jnp.stack([k, v], axis=2), new_buf, h_last


def cd_mixer(h, pos, wts, conv_buf, h0, attend):
    (w_in, w_out, q_lat_g, w_uq, kv_lat_g, w_ukv, gq, gk,
     conv_w, conv_b, dt_bias, a_log, d_skip, norm_g) = wts
    f32 = jnp.float32
    nb, t = h.shape[0], h.shape[1]
    cq, ckv, kpe, z, xbc, dt = split_cols(h @ w_in, CD_SIZES)
    q = jnp.einsum('btr,rhe->bthe', rms_norm(cq, q_lat_g), w_uq)
    q = jnp.concatenate([q[..., :MLA_NOPE_DIM], rope(q[..., MLA_NOPE_DIM:], pos)], axis=-1)
    q = rms_norm(q, gq)
    latent = jnp.concatenate([rms_norm(ckv, kv_lat_g), rope(kpe, pos)], axis=-1)
    o_c = attend(q, latent, w_ukv, gk)
    xbc, new_buf = causal_conv(xbc, conv_buf, conv_w, conv_b)
    xbc = jax.nn.silu(xbc)
    xs, bm, cm = split_cols(xbc, (SSD_INNER, SSD_GROUPS * SSD_STATE, SSD_GROUPS * SSD_STATE))
    xs = xs.reshape(nb, t, SSD_GROUPS, SSD_HPG, SSD_HEAD_DIM).astype(f32)
    bm = bm.reshape(nb, t, SSD_GROUPS, SSD_STATE).astype(f32)
    cm = cm.reshape(nb, t, SSD_GROUPS, SSD_STATE).astype(f32)
    dt = jax.nn.softplus(dt.astype(f32) + dt_bias.astype(f32)).reshape(nb, t, SSD_GROUPS, SSD_HPG)
    a = -jnp.exp(a_log.astype(f32)).reshape(SSD_GROUPS, SSD_HPG)
    h0 = h0.astype(f32).reshape(nb, SSD_GROUPS, SSD_HPG, SSD_HEAD_DIM, SSD_STATE)
    y, h_last = ssd_scan(xs, dt, a, bm, cm, h0)
    y = y + d_skip.astype(f32).reshape(SSD_GROUPS, SSD_HPG, 1) * xs
    y = y.reshape(nb, t, SSD_INNER).astype(h.dtype) * jax.nn.silu(z)
    y = rms_norm(y.reshape(nb, t, SSD_GROUPS, SSD_INNER // SSD_GROUPS),
                 norm_g.reshape(SSD_GROUPS, SSD_INNER // SSD_GROUPS)).reshape(nb, t, SSD_INNER)
    o = jnp.concatenate([o_c.reshape(nb, t, -1), y], axis=-1) @ w_out
    return o, latent, new_buf, h_last.reshape(nb, SSD_HEADS, SSD_HEAD_DIM, SSD_STATE).astype(h.dtype)


def swiglu(h, w_in, w_out):
    u, g = split_cols(h @ w_in, (FFN_HIDDEN, FFN_HIDDEN))
    return (jax.nn.silu(u) * g) @ w_out


def setup_inputs(seed: int = 0) -> dict:
    key = jax.random.key(seed)
    ks = iter(jax.random.split(key, 48))
    f32 = jnp.float32

    def nrm(shape, scale=1.0):
        return scale * jax.random.normal(next(ks), shape, f32)

    def gain(shape):
        return 1.0 + nrm(shape, 0.02)

    n_pages = PAST_LEN // PAGE_SIZE
    n_used = DEC_BATCH * n_pages
    n_pool = n_used + (n_used + 3) // 4
    page_table = jax.random.permutation(next(ks), n_pool)[:n_used].reshape(DEC_BATCH, n_pages).astype(jnp.int32)

    x_prompt = nrm((BATCH, SEQ, D_MODEL))
    x_sample = nrm((DEC_BATCH, DEC_SEQ, D_MODEL))
    cache_sb_kv = nrm((N_EVEN, n_pool, PAGE_SIZE, 2, SB_KV_HEADS, SB_HEAD_DIM))
    cache_mla_kv = nrm((N_ODD, n_pool, PAGE_SIZE, MLA_KV_RANK + MLA_ROPE_DIM))
    state_lru_h = nrm((N_EVEN, DEC_BATCH, LRU_WIDTH), 0.5)
    state_lru_conv = nrm((N_EVEN, DEC_BATCH, CONV_WIDTH - 1, LRU_WIDTH))
    state_ssm_h = nrm((N_ODD, DEC_BATCH, SSD_HEADS, SSD_HEAD_DIM, SSD_STATE), 0.3)
    state_ssm_conv = nrm((N_ODD, DEC_BATCH, CONV_WIDTH - 1, SSD_CONV_DIM))

    u_lam = jax.random.uniform(next(ks), (N_EVEN, LRU_WIDTH), f32, 0.9, 0.999)
    s_lam = u_lam ** (1.0 / LRU_C)
    lru_lambda = jnp.log(s_lam) - jnp.log1p(-s_lam)
    dt0 = jnp.exp(jax.random.uniform(next(ks), (N_ODD, SSD_HEADS), f32, np.log(0.001).astype(np.float32), np.log(0.1).astype(np.float32)))
    ssd_dt_bias = dt0 + jnp.log(-jnp.expm1(-dt0))
    ssd_a_log = jnp.log(jax.random.uniform(next(ks), (N_ODD, SSD_HEADS), f32, 1.0, 16.0))

    return {
        'x_prompt': x_prompt,
        'x_sample': x_sample,
        'cache_sb_kv': cache_sb_kv,
        'cache_mla_kv': cache_mla_kv,
        'page_table': page_table,
        'state_lru_h': state_lru_h,
        'state_lru_conv': state_lru_conv,
        'state_ssm_h': state_ssm_h,
        'state_ssm_conv': state_ssm_conv,
        'norm_mix': gain((DEPTH, D_MODEL)),
        'norm_ffn': gain((DEPTH, D_MODEL)),
        'ab_w_in': nrm((N_EVEN, D_MODEL, AB_IN), D_MODEL ** -0.5),
        'ab_w_out': nrm((N_EVEN, AB_OUT, D_MODEL), AB_OUT ** -0.5),
        'sb_q_gain': gain((N_EVEN, SB_HEAD_DIM)),
        'sb_k_gain': gain((N_EVEN, SB_HEAD_DIM)),
        'lru_conv_w': nrm((N_EVEN, CONV_WIDTH, LRU_WIDTH), CONV_WIDTH ** -0.5),
        'lru_conv_b': nrm((N_EVEN, LRU_WIDTH), 0.02),
        'lru_wa': nrm((N_EVEN, LRU_BLOCKS, LRU_BLOCK_DIM, LRU_BLOCK_DIM), LRU_BLOCK_DIM ** -0.5),
        'lru_ba': nrm((N_EVEN, LRU_WIDTH), 0.02),
        'lru_wx': nrm((N_EVEN, LRU_BLOCKS, LRU_BLOCK_DIM, LRU_BLOCK_DIM), LRU_BLOCK_DIM ** -0.5),
        'lru_bx': nrm((N_EVEN, LRU_WIDTH), 0.02),
        'lru_lambda': lru_lambda,
        'cd_w_in': nrm((N_ODD, D_MODEL, CD_IN), D_MODEL ** -0.5),
        'cd_w_out': nrm((N_ODD, CD_OUT, D_MODEL), CD_OUT ** -0.5),
        'mla_q_lat_gain': gain((N_ODD, MLA_Q_RANK)),
        'mla_w_uq': nrm((N_ODD, MLA_Q_RANK, MLA_HEADS, MLA_QK_DIM), MLA_Q_RANK ** -0.5),
        'mla_kv_lat_gain': gain((N_ODD, MLA_KV_RANK)),
        'mla_w_ukv': nrm((N_ODD, MLA_KV_RANK, MLA_HEADS, MLA_NOPE_DIM + MLA_V_DIM), MLA_KV_RANK ** -0.5),
        'mla_q_gain': gain((N_ODD, MLA_QK_DIM)),
        'mla_k_gain': gain((N_ODD, MLA_QK_DIM)),
        'ssd_conv_w': nrm((N_ODD, CONV_WIDTH, SSD_CONV_DIM), CONV_WIDTH ** -0.5),
        'ssd_conv_b': nrm((N_ODD, SSD_CONV_DIM), 0.02),
        'ssd_dt_bias': ssd_dt_bias,
        'ssd_a_log': ssd_a_log,
        'ssd_d': 1.0 + nrm((N_ODD, SSD_HEADS), 0.1),
        'ssd_norm_gain': gain((N_ODD, SSD_INNER)),
        'ffn_w_in': nrm((DEPTH, D_MODEL, 2 * FFN_HIDDEN), D_MODEL ** -0.5),
        'ffn_w_out': nrm((DEPTH, FFN_HIDDEN, D_MODEL), FFN_HIDDEN ** -0.5),
    }


def reference(x_prompt, x_sample, cache_sb_kv, cache_mla_kv, page_table, state_lru_h, state_lru_conv,
              state_ssm_h, state_ssm_conv, norm_mix, norm_ffn, ab_w_in, ab_w_out, sb_q_gain, sb_k_gain,
              lru_conv_w, lru_conv_b, lru_wa, lru_ba, lru_wx, lru_bx, lru_lambda, cd_w_in, cd_w_out,
              mla_q_lat_gain, mla_w_uq, mla_kv_lat_gain, mla_w_ukv, mla_q_gain, mla_k_gain, ssd_conv_w,
              ssd_conv_b, ssd_dt_bias, ssd_a_log, ssd_d, ssd_norm_gain, ffn_w_in, ffn_w_out):
    xp, xs = x_prompt, x_sample
    bp = xp.shape[0]
    pos_p = jnp.arange(xp.shape[1])
    pos_s = PAST_LEN + jnp.arange(xs.shape[1])
    sb_kv_p, sb_kv_s, lru_h_p, lru_h_s, lru_c_p, lru_c_s = [], [], [], [], [], []
    mla_p, mla_s, ssm_h_p, ssm_h_s, ssm_c_p, ssm_c_s = [], [], [], [], [], []
    for li in range(DEPTH):
        hp = rms_norm(xp, norm_mix[li])
        hs = rms_norm(xs, norm_mix[li])
        if li % 2 == 0:
            e = li // 2
            wts = (ab_w_in[e], ab_w_out[e], sb_q_gain[e], sb_k_gain[e], lru_conv_w[e], lru_conv_b[e],
                   lru_wa[e], lru_ba[e], lru_wx[e], lru_bx[e], lru_lambda[e])
            zero_buf = jnp.zeros((bp, CONV_WIDTH - 1, LRU_WIDTH), xp.dtype)
            zero_h = jnp.zeros((bp, LRU_WIDTH), xp.dtype)
            op, kv_p, cb_p, h_p = ab_mixer(
                hp, wts, zero_buf, zero_h,
                lambda q, k, v: sweep_query_blocks(stick_breaking, q, k, v))
            os_, kv_s, cb_s, h_s = ab_mixer(
                hs, wts, state_lru_conv[e], state_lru_h[e],
                lambda q, k, v: sb_attend_sample(q, k, v, cache_sb_kv, e, page_table))
            sb_kv_p.append(kv_p); sb_kv_s.append(kv_s)
            lru_h_p.append(h_p); lru_h_s.append(h_s)
            lru_c_p.append(cb_p); lru_c_s.append(cb_s)
        else:
            o = li // 2
            wts = (cd_w_in[o], cd_w_out[o], mla_q_lat_gain[o], mla_w_uq[o], mla_kv_lat_gain[o], mla_w_ukv[o],
                   mla_q_gain[o], mla_k_gain[o], ssd_conv_w[o], ssd_conv_b[o], ssd_dt_bias[o], ssd_a_log[o],
                   ssd_d[o], ssd_norm_gain[o])
            zero_buf = jnp.zeros((bp, CONV_WIDTH - 1, SSD_CONV_DIM), xp.dtype)
            zero_h = jnp.zeros((bp, SSD_HEADS, SSD_HEAD_DIM, SSD_STATE), xp.dtype)
            op, lat_p, cb_p, h_p = cd_mixer(
                hp, pos_p, wts, zero_buf, zero_h,
                lambda q, lat, w, g: sweep_query_blocks(softmax_attn, q, *mla_kv(lat, w, g)))
            os_, lat_s, cb_s, h_s = cd_mixer(
                hs, pos_s, wts, state_ssm_conv[o], state_ssm_h[o],
                lambda q, lat, w, g: mla_attend_sample(q, lat, w, g, cache_mla_kv, o, page_table))
            mla_p.append(lat_p); mla_s.append(lat_s)
            ssm_h_p.append(h_p); ssm_h_s.append(h_s)
            ssm_c_p.append(cb_p); ssm_c_s.append(cb_s)
        xp = xp + op
        xs = xs + os_
        xp = xp + swiglu(rms_norm(xp, norm_ffn[li]), ffn_w_in[li], ffn_w_out[li])
        xs = xs + swiglu(rms_norm(xs, norm_ffn[li]), ffn_w_in[li], ffn_w_out[li])
    sb_kv_prompt = jnp.stack(sb_kv_p)
    sb_kv_sample = jnp.stack(sb_kv_s)
    lru_h_prompt = jnp.stack(lru_h_p)
    lru_h_sample = jnp.stack(lru_h_s)
    lru_conv_prompt = jnp.stack(lru_c_p)
    lru_conv_sample = jnp.stack(lru_c_s)
    mla_kv_prompt = jnp.stack(mla_p)
    mla_kv_sample = jnp.stack(mla_s)
    ssm_h_prompt = jnp.stack(ssm_h_p)
    ssm_h_sample = jnp.stack(ssm_h_s)
    ssm_conv_prompt = jnp.stack(ssm_c_p)
    ssm_conv_sample = jnp.stack(ssm_c_s)
    return (xp, xs, sb_kv_prompt, sb_kv_sample, lru_h_prompt, lru_h_sample, lru_conv_prompt, lru_conv_sample,
            mla_kv_prompt, mla_kv_sample, ssm_h_prompt, ssm_h_sample, ssm_conv_prompt, ssm_conv_sample)
```

```python
import functools
import math

import jax
import jax.numpy as jnp
from jax import lax
from jax.experimental import pallas as pl
from jax.experimental.pallas import tpu as pltpu

F32 = jnp.float32
BF16 = jnp.bfloat16
NORM_EPS = 1e-6
LRU_C = 8.0
ROPE_THETA = 10000.0
CONV_WIDTH = 4
LANES = 128
VMEM_LIMIT_BYTES = 56 * 1024 * 1024
_NT = (((1,), (1,)), ((), ()))
_TN = (((0,), (0,)), ((), ()))


def _cparams(*sem):
    return pltpu.CompilerParams(dimension_semantics=sem, vmem_limit_bytes=VMEM_LIMIT_BYTES)


def _dot(a, b):
    return jnp.dot(a, b, preferred_element_type=F32)


def _split2(x):
    hi = x.astype(BF16)
    lo = (x - hi.astype(F32)).astype(BF16)
    return hi, lo


def _split3(x):
    hi = x.astype(BF16)
    r = x - hi.astype(F32)
    mid = r.astype(BF16)
    lo = (r - mid.astype(F32)).astype(BF16)
    return hi, mid, lo


def _rms(x, g):
    return x * lax.rsqrt(jnp.mean(x * x, axis=-1, keepdims=True) + NORM_EPS) * g


def _sigmoid(x):
    return 1.0 / (1.0 + jnp.exp(-x))


def _silu(x):
    return x * _sigmoid(x)


def _softplus(x):
    return jnp.maximum(x, 0.0) + jnp.log(1.0 + jnp.exp(-jnp.abs(x)))


def _gelu_tanh(x):
    c = math.sqrt(2.0 / math.pi)
    return x * (0.5 * (1.0 + jnp.tanh(c * (x + 0.044715 * (x * x * x)))))


def _block_ones(n, blk, dtype=BF16):
    r = lax.broadcasted_iota(jnp.int32, (n, n), 0) // blk
    c = lax.broadcasted_iota(jnp.int32, (n, n), 1) // blk
    return (r == c).astype(dtype)


def _full(shape):
    nd = len(shape)
    return pl.BlockSpec(shape, lambda *_: (0,) * nd)


def _ab_inproj_kernel(x_ref, g_ref, w_ref, gq_ref, gk_ref, pq_ref, pk_ref,
                      qh_ref, kh_ref, vh_ref, kv_ref, xb_ref, gate_ref, *, nq, nk, hd):
    h = _rms(x_ref[...], g_ref[...]).astype(BF16)
    p = _dot(h, w_ref[...])
    dq, dk = nq * hd, nk * hd
    q = p[:, :dq]
    k = p[:, dq:dq + dk]
    v = p[:, dq + dk:dq + 2 * dk]

    def head_norm(t, ones_ref, gain):
        hi, lo = _split2(t * t)
        ms = (_dot(hi, ones_ref[...]) + _dot(lo, ones_ref[...])) * (1.0 / hd)
        return t * lax.rsqrt(ms + NORM_EPS) * gain

    qn = head_norm(q, pq_ref, gq_ref[...])
    kn = head_norm(k, pk_ref, gk_ref[...])
    for i in range(nq):
        qh_ref[i] = qn[:, i * hd:(i + 1) * hd].astype(BF16)
    for i in range(nk):
        kh_ref[i] = kn[:, i * hd:(i + 1) * hd].astype(BF16)
        vh_ref[i] = v[:, i * hd:(i + 1) * hd].astype(BF16)
    kv_ref[:, :dk] = kn
    kv_ref[:, dk:] = v
    w_lru = xb_ref.shape[-1]
    xb_ref[...] = p[:, dq + 2 * dk:dq + 2 * dk + w_lru]
    gate_ref[...] = p[:, dq + 2 * dk + w_lru:]


def _ab_inproj(x, g, w_bf, gq, gk, *, nq, nk, hd, w_lru, tm):
    m, d = x.shape
    n = w_bf.shape[1]
    dq, dk = nq * hd, nk * hd
    scale = hd ** -0.5
    gq_t = (jnp.tile(gq, nq) * scale).reshape(1, dq)
    gk_t = jnp.tile(gk, nk).reshape(1, dk)
    row = lambda i: (i, 0)
    head = lambda i: (0, i, 0)
    return pl.pallas_call(
        functools.partial(_ab_inproj_kernel, nq=nq, nk=nk, hd=hd),
        grid=(m // tm,),
        in_specs=[pl.BlockSpec((tm, d), row), _full((1, d)), _full((d, n)), _full((1, dq)), _full((1, dk)),
                  _full((dq, dq)), _full((dk, dk))],
        out_specs=[pl.BlockSpec((nq, tm, hd), head), pl.BlockSpec((nk, tm, hd), head),
                   pl.BlockSpec((nk, tm, hd), head), pl.BlockSpec((tm, 2 * dk), row),
                   pl.BlockSpec((tm, w_lru), row), pl.BlockSpec((tm, w_lru), row)],
        out_shape=[jax.ShapeDtypeStruct((nq, m, hd), BF16), jax.ShapeDtypeStruct((nk, m, hd), BF16),
                   jax.ShapeDtypeStruct((nk, m, hd), BF16), jax.ShapeDtypeStruct((m, 2 * dk), F32),
                   jax.ShapeDtypeStruct((m, w_lru), F32), jax.ShapeDtypeStruct((m, w_lru), F32)],
        compiler_params=_cparams("parallel"),
    )(x, g.reshape(1, d), w_bf, gq_t, gk_t, _block_ones(dq, hd), _block_ones(dk, hd))


def _sb_block(q2, k, v, uu, r_carry, acc, mask):
    z = lax.dot_general(q2, k, _NT, preferred_element_type=F32)
    lk = -(jnp.maximum(z, 0.0) + jnp.log(1.0 + jnp.exp(-jnp.abs(z))))
    if mask is not None:
        lk = jnp.where(mask, lk, 0.0)
    hi, lo = _split2(lk)
    tail = _dot(jnp.concatenate([hi, lo], axis=1), uu) + r_carry
    w = jnp.exp(z + tail)
    if mask is not None:
        w = jnp.where(mask, w, 0.0)
    acc = acc + _dot(w.astype(BF16), v)
    return tail[:, 0:1], acc


def _sb_prompt_kernel(q_ref, k_ref, v_ref, uu_ref, o_ref, *, tq, grp, hd):
    i = pl.program_id(2)
    rows = grp * tq
    q2 = q_ref[...].reshape(rows, hd)
    uu = uu_ref[...]
    t_idx = lax.broadcasted_iota(jnp.int32, (grp, tq, tq), 1).reshape(rows, tq)
    j_idx = lax.broadcasted_iota(jnp.int32, (rows, tq), 1)
    mask = j_idx < t_idx

    def load(kb):
        off = pl.multiple_of(kb * tq, tq)
        return k_ref[pl.ds(off, tq), :], v_ref[pl.ds(off, tq), :]

    k, v = load(i)
    carry = _sb_block(q2, k, v, uu, jnp.zeros((rows, 1), F32), jnp.zeros((rows, hd), F32), mask)

    def body(s, c):
        k, v = load(i - 1 - s)
        return _sb_block(q2, k, v, uu, c[0], c[1], None)

    _, acc = lax.fori_loop(0, i, body, carry)
    o_ref[...] = acc.reshape(grp, tq, hd).astype(o_ref.dtype)


def _rev_cumsum_ones(n):
    s = lax.broadcasted_iota(jnp.int32, (n, n), 0)
    j = lax.broadcasted_iota(jnp.int32, (n, n), 1)
    u = (s >= j).astype(BF16)
    return jnp.concatenate([u, u], axis=0)


def _sb_prompt(qh, kh, vh, *, nb, tq):
    nq, m, hd = qh.shape
    nk = kh.shape[0]
    grp = nq // nk
    t = m // nb
    n_blk = t // tq
    return pl.pallas_call(
        functools.partial(_sb_prompt_kernel, tq=tq, grp=grp, hd=hd),
        grid=(nb, nk, n_blk),
        in_specs=[pl.BlockSpec((grp, tq, hd), lambda b, h, i: (h, b * n_blk + i, 0)),
                  pl.BlockSpec((None, t, hd), lambda b, h, i: (h, b, 0)),
                  pl.BlockSpec((None, t, hd), lambda b, h, i: (h, b, 0)),
                  _full((2 * tq, tq))],
        out_specs=pl.BlockSpec((grp, tq, hd), lambda b, h, i: (h, b * n_blk + i, 0)),
        out_shape=jax.ShapeDtypeStruct((nq, m, hd), BF16),
        compiler_params=_cparams("parallel", "parallel", "arbitrary"),
    )(qh, kh, vh, _rev_cumsum_ones(tq))


def _lru_gates(xc, wbd_ref, bias_ref, lam_ref):
    w = xc.shape[-1]
    ra = _dot(xc.astype(BF16), wbd_ref[...]) + bias_ref[...]
    r = _sigmoid(ra[:, :w])
    ig = _sigmoid(ra[:, w:])
    log_a = (-LRU_C) * r * _softplus(-lam_ref[...])
    a = jnp.exp(log_a)
    b = jnp.sqrt(-jnp.tanh(log_a) * (a * a + 1.0)) * (ig * xc)
    return a, b


def _rglru_prompt_kernel(xb_ref, gate_ref, cw_ref, cb_ref, wbd_ref, bias_ref, lam_ref,
                         y_ref, hlast_ref, cbuf_ref, xpad, a_scr, b_scr, h_scr, *, tt):
    t = pl.program_id(1)
    pad = 8

    @pl.when(t == 0)
    def _():
        h_scr[...] = jnp.zeros_like(h_scr)
        xpad[0:pad, :] = jnp.zeros((pad, xpad.shape[1]), F32)

    xpad[pad:pad + tt, :] = xb_ref[...]
    xc = cb_ref[...]
    for i in range(CONV_WIDTH):
        o = pad - (CONV_WIDTH - 1) + i
        xc = xc + xpad[o:o + tt, :] * cw_ref[i:i + 1, :]
    cbuf_ref[...] = xpad[pad + tt - (CONV_WIDTH - 1):pad + tt, :]
    xpad[0:pad, :] = xpad[tt:tt + pad, :]
    a, b = _lru_gates(xc, wbd_ref, bias_ref, lam_ref)
    a_scr[...] = a
    b_scr[...] = b

    def body(r, h):
        h = a_scr[pl.ds(r, 1), :] * h + b_scr[pl.ds(r, 1), :]
        b_scr[pl.ds(r, 1), :] = h
        return h

    h = lax.fori_loop(0, tt, body, h_scr[...], unroll=8)
    h_scr[...] = h
    hlast_ref[...] = h
    y_ref[...] = (_gelu_tanh(gate_ref[...]) * b_scr[...]).astype(y_ref.dtype)


def _lru_weights(wa, ba, wx, bx):
    nblk, bd, _ = wa.shape
    w = nblk * bd
    eye = jnp.eye(nblk, dtype=F32)

    def bdiag(m):
        return jnp.einsum('kij,kl->kilj', m, eye).reshape(w, w)

    wbd = jnp.concatenate([bdiag(wa), bdiag(wx)], axis=1).astype(BF16)
    bias = jnp.concatenate([ba, bx]).reshape(1, 2 * w)
    return wbd, bias


def _rglru_prompt(xb, gate, conv_w, conv_b, wbd, bias, lam, *, nb, tt):
    m, w = xb.shape
    t = m // nb
    nt = t // tt
    row = lambda b, i: (b * nt + i, 0)
    return pl.pallas_call(
        functools.partial(_rglru_prompt_kernel, tt=tt),
        grid=(nb, nt),
        in_specs=[pl.BlockSpec((tt, w), row), pl.BlockSpec((tt, w), row), _full((CONV_WIDTH, w)),
                  _full((1, w)), _full((w, 2 * w)), _full((1, 2 * w)), _full((1, w))],
        out_specs=[pl.BlockSpec((tt, w), row), pl.BlockSpec((None, 1, w), lambda b, i: (b, 0, 0)),
                   pl.BlockSpec((None, CONV_WIDTH - 1, w), lambda b, i: (b, 0, 0))],
        out_shape=[jax.ShapeDtypeStruct((m, w), BF16), jax.ShapeDtypeStruct((nb, 1, w), F32),
                   jax.ShapeDtypeStruct((nb, CONV_WIDTH - 1, w), F32)],
        scratch_shapes=[pltpu.VMEM((tt + 8, w), F32), pltpu.VMEM((tt, w), F32), pltpu.VMEM((tt, w), F32),
                        pltpu.VMEM((1, w), F32)],
        compiler_params=_cparams("parallel", "arbitrary"),
    )(xb, gate, conv_w, conv_b.reshape(1, w), wbd, bias, lam.reshape(1, w))


def _post_kernel(x_ref, oh_ref, y2_ref, wo_ref, g_ref, wu_ref, wg_ref, wd_ref, out_ref,
                 x1_scr, hb_scr, acc_scr):
    j = pl.program_id(1)
    nh, _, hd = oh_ref.shape

    @pl.when(j == 0)
    def _():
        mix = _dot(y2_ref[...], wo_ref[nh * hd:, :])
        for h in range(nh):
            mix = mix + _dot(oh_ref[h], wo_ref[h * hd:(h + 1) * hd, :])
        x1 = x_ref[...] + mix
        x1_scr[...] = x1
        hb_scr[...] = _rms(x1, g_ref[...]).astype(BF16)
        acc_scr[...] = jnp.zeros_like(acc_scr)

    hb = hb_scr[...]
    u = _dot(hb, wu_ref[...])
    gg = _dot(hb, wg_ref[...])
    acc_scr[...] += _dot((_silu(u) * gg).astype(BF16), wd_ref[...])

    @pl.when(j == pl.num_programs(1) - 1)
    def _():
        out_ref[...] = x1_scr[...] + acc_scr[...]


def _post(x, oh, y2, w_out_bf, g, w_in_bf, w_dn_bf, *, tm, th):
    m, d = x.shape
    nh, _, hd = oh.shape
    hid = w_dn_bf.shape[0]
    nj = hid // th
    return pl.pallas_call(
        _post_kernel,
        grid=(m // tm, nj),
        in_specs=[pl.BlockSpec((tm, d), lambda i, j: (i, 0)),
                  pl.BlockSpec((nh, tm, hd), lambda i, j: (0, i, 0)),
                  pl.BlockSpec((tm, y2.shape[1]), lambda i, j: (i, 0)),
                  pl.BlockSpec(w_out_bf.shape, lambda i, j: (0, 0)),
                  pl.BlockSpec((1, d), lambda i, j: (0, 0)),
                  pl.BlockSpec((d, th), lambda i, j: (0, j)),
                  pl.BlockSpec((d, th), lambda i, j: (0, nj + j)),
                  pl.BlockSpec((th, d), lambda i, j: (j, 0))],
        out_specs=pl.BlockSpec((tm, d), lambda i, j: (i, 0)),
        out_shape=jax.ShapeDtypeStruct((m, d), F32),
        scratch_shapes=[pltpu.VMEM((tm, d), F32), pltpu.VMEM((tm, d), BF16), pltpu.VMEM((tm, d), F32)],
        compiler_params=_cparams("parallel", "arbitrary"),
    )(x, oh, y2, w_out_bf, g.reshape(1, d), w_in_bf, w_in_bf, w_dn_bf)


def _rope_tables(pos, nope, rope):
    half = rope // 2
    inv = ROPE_THETA ** (-jnp.arange(half, dtype=F32) / half)
    ang = pos.astype(F32)[:, None] * inv[None, :]
    n = pos.shape[0]
    lead = jnp.zeros((n, nope), F32)
    trail = jnp.zeros((n, LANES - nope - rope), F32)
    cos = jnp.concatenate([lead, jnp.cos(ang), jnp.cos(ang), trail], axis=1)
    sin = jnp.concatenate([lead, jnp.sin(ang), jnp.sin(ang), trail], axis=1)
    return cos, sin


def _rot_partner(w):
    half = w.shape[-1] // 2
    return jnp.concatenate([-w[..., half:], w[..., :half]], axis=-1)


def _cd_weights(w_in, w_uq, w_ukv, gq, gk, dt_bias, a_log, d_skip, *, kv_rank, q_rank, rope, inner, conv_dim):
    d = w_in.shape[0]
    nh, qk = w_uq.shape[1], w_uq.shape[2]
    nope = qk - rope
    vd = w_ukv.shape[2] - nope
    n_ssd = dt_bias.shape[0]
    assert qk <= LANES and kv_rank == LANES and n_ssd <= LANES
    o = 0
    cq = w_in[:, o:o + q_rank]; o += q_rank
    ckv = w_in[:, o:o + kv_rank]; o += kv_rank
    kpe = w_in[:, o:o + rope]; o += rope
    z = w_in[:, o:o + inner]; o += inner
    xbc = w_in[:, o:o + conv_dim]; o += conv_dim
    dt = w_in[:, o:o + n_ssd]

    def on_rope_lanes(w):
        return jnp.concatenate([jnp.zeros((d, nope), F32), w, jnp.zeros((d, LANES - qk), F32)], axis=1)

    w_pad = jnp.concatenate([cq, ckv, on_rope_lanes(kpe), on_rope_lanes(_rot_partner(kpe)), z, xbc,
                             dt, jnp.zeros((d, LANES - n_ssd), F32)], axis=1).astype(BF16)
    zq = jnp.zeros((q_rank, nh, LANES - qk), F32)
    wqm = jnp.concatenate([w_uq, zq], axis=2).reshape(q_rank, nh * LANES).astype(BF16)
    wqp = jnp.concatenate([jnp.zeros((q_rank, nh, nope), F32), _rot_partner(w_uq[:, :, nope:]), zq],
                          axis=2).reshape(q_rank, nh * LANES).astype(BF16)
    wk = w_ukv[:, :, :nope]
    wk128 = jnp.concatenate([wk, jnp.zeros((kv_rank, nh, LANES - nope), F32)], axis=2)
    pad1 = lambda v, fill=0.0: jnp.concatenate([v, jnp.full((LANES - v.shape[0],), fill, F32)]).reshape(1, LANES)
    return dict(
        w_pad=w_pad, wqm=wqm, wqp=wqp,
        wk128=wk128.reshape(kv_rank, nh * LANES).astype(BF16),
        wk=wk.reshape(kv_rank, nh * nope).astype(BF16),
        wv=w_ukv[:, :, nope:].reshape(kv_rank, nh * vd).astype(BF16),
        gq128=pad1(gq * qk ** -0.5), gk128=pad1(gk), dtb128=pad1(dt_bias), alog128=pad1(a_log),
        dsk=jnp.repeat(d_skip, inner // n_ssd).reshape(1, inner),
        nh=nh, qk=qk, nope=nope, rope=rope, vd=vd, q_rank=q_rank, kv_rank=kv_rank, inner=inner,
        conv_dim=conv_dim, n_ssd=n_ssd)


def _cd_inproj_kernel(x_ref, g_ref, w_ref, gql_ref, gkl_ref, wqm_ref, wqp_ref, wk_ref, wv_ref, gq_ref, gk_ref,
                      cos_ref, sin_ref, qh_ref, kh_ref, vh_ref, lat_ref, z_ref, xbc_ref, dt_ref,
                      *, nh, qk, nope, rope, vd, q_rank, kv_rank, inner, conv_dim):
    h = _rms(x_ref[...], g_ref[...]).astype(BF16)
    p = _dot(h, w_ref[...])
    o = 0
    cq = p[:, o:o + q_rank]; o += q_rank
    ckv = p[:, o:o + kv_rank]; o += kv_rank
    kpe = p[:, o:o + LANES]; o += LANES
    kpe_rot = p[:, o:o + LANES]; o += LANES
    z_ref[...] = p[:, o:o + inner]; o += inner
    xbc_ref[...] = p[:, o:o + conv_dim]; o += conv_dim
    dt_ref[...] = p[:, o:o + LANES]

    cos_t = cos_ref[...]
    sin_t = sin_ref[...]
    lane = lax.broadcasted_iota(jnp.int32, (1, LANES), 1)
    cos_q = cos_t + (lane < nope).astype(F32)
    cqn = _rms(cq, gql_ref[...]).astype(BF16)
    qm = _dot(cqn, wqm_ref[...])
    qp = _dot(cqn, wqp_ref[...])
    for i in range(nh):
        qi = qm[:, i * LANES:(i + 1) * LANES] * cos_q + qp[:, i * LANES:(i + 1) * LANES] * sin_t
        ms = jnp.sum(qi * qi, axis=-1, keepdims=True) * (1.0 / qk)
        qh_ref[i] = (qi * lax.rsqrt(ms + NORM_EPS) * gq_ref[...]).astype(BF16)

    ckvn = _rms(ckv, gkl_ref[...])
    kper = kpe * cos_t + kpe_rot * sin_t
    lat_ref[:, :kv_rank] = ckvn
    lat_ref[:, kv_rank:] = kper[:, nope:nope + rope]
    cb = ckvn.astype(BF16)
    kn = _dot(cb, wk_ref[...])
    vv = _dot(cb, wv_ref[...])
    for i in range(nh):
        ki = kn[:, i * LANES:(i + 1) * LANES] + kper
        ms = jnp.sum(ki * ki, axis=-1, keepdims=True) * (1.0 / qk)
        kh_ref[i] = (ki * lax.rsqrt(ms + NORM_EPS) * gk_ref[...]).astype(BF16)
        vh_ref[i] = vv[:, i * vd:(i + 1) * vd].astype(BF16)


def _cd_inproj(x, g, cw, gql, gkl, cos, sin, *, tm):
    m, d = x.shape
    nh, vd, kv_rank, rope = cw["nh"], cw["vd"], cw["kv_rank"], cw["rope"]
    inner, conv_dim, q_rank = cw["inner"], cw["conv_dim"], cw["q_rank"]
    n_pos = cos.shape[0] // tm
    row = lambda i: (i, 0)
    head = lambda i: (0, i, 0)
    dims = {k: cw[k] for k in ("nh", "qk", "nope", "rope", "vd", "q_rank", "kv_rank", "inner", "conv_dim")}
    return pl.pallas_call(
        functools.partial(_cd_inproj_kernel, **dims),
        grid=(m // tm,),
        in_specs=[pl.BlockSpec((tm, d), row), _full((1, d)), _full(cw["w_pad"].shape), _full((1, q_rank)),
                  _full((1, kv_rank)), _full(cw["wqm"].shape), _full(cw["wqp"].shape), _full(cw["wk128"].shape),
                  _full(cw["wv"].shape), _full((1, LANES)), _full((1, LANES)),
                  pl.BlockSpec((tm, LANES), lambda i: (i % n_pos, 0)),
                  pl.BlockSpec((tm, LANES), lambda i: (i % n_pos, 0))],
        out_specs=[pl.BlockSpec((nh, tm, LANES), head), pl.BlockSpec((nh, tm, LANES), head),
                   pl.BlockSpec((nh, tm, vd), head), pl.BlockSpec((tm, kv_rank + rope), row),
                   pl.BlockSpec((tm, inner), row), pl.BlockSpec((tm, conv_dim), row),
                   pl.BlockSpec((tm, LANES), row)],
        out_shape=[jax.ShapeDtypeStruct((nh, m, LANES), BF16), jax.ShapeDtypeStruct((nh, m, LANES), BF16),
                   jax.ShapeDtypeStruct((nh, m, vd), BF16), jax.ShapeDtypeStruct((m, kv_rank + rope), F32),
                   jax.ShapeDtypeStruct((m, inner), F32), jax.ShapeDtypeStruct((m, conv_dim), F32),
                   jax.ShapeDtypeStruct((m, LANES), F32)],
        compiler_params=_cparams("parallel"),
    )(x, g.reshape(1, d), cw["w_pad"], gql.reshape(1, q_rank), gkl.reshape(1, kv_rank), cw["wqm"], cw["wqp"],
      cw["wk128"], cw["wv"], cw["gq128"], cw["gk128"], cos, sin)


def _mla_prompt_kernel(q_ref, k_ref, v_ref, o_ref, *, tq):
    i = pl.program_id(2)
    q = q_ref[...]
    vd = v_ref.shape[-1]
    t_idx = lax.broadcasted_iota(jnp.int32, (tq, tq), 0)
    j_idx = lax.broadcasted_iota(jnp.int32, (tq, tq), 1)
    mask = j_idx <= t_idx

    def block(kb, carry, msk):
        m, l, acc = carry
        off = pl.multiple_of(kb * tq, tq)
        s = lax.dot_general(q, k_ref[pl.ds(off, tq), :], _NT, preferred_element_type=F32)
        if msk is not None:
            s = jnp.where(msk, s, -jnp.inf)
        m_new = jnp.maximum(m, jnp.max(s, axis=-1, keepdims=True))
        alpha = jnp.exp(m - m_new)
        p = jnp.exp(s - m_new)
        l = alpha * l + jnp.sum(p, axis=-1, keepdims=True)
        acc = alpha * acc + _dot(p.astype(BF16), v_ref[pl.ds(off, tq), :])
        return m_new, l, acc

    init = (jnp.full((tq, 1), -jnp.inf, F32), jnp.zeros((tq, 1), F32), jnp.zeros((tq, vd), F32))
    carry = block(i, init, mask)
    _, l, acc = lax.fori_loop(0, i, lambda s, c: block(s, c, None), carry)
    o_ref[...] = (acc / l).astype(o_ref.dtype)


def _mla_prompt(qh, kh, vh, *, nb, tq):
    nh, m, dk = qh.shape
    vd = vh.shape[-1]
    t = m // nb
    n_blk = t // tq
    return pl.pallas_call(
        functools.partial(_mla_prompt_kernel, tq=tq),
        grid=(nb, nh, n_blk),
        in_specs=[pl.BlockSpec((None, tq, dk), lambda b, h, i: (h, b * n_blk + i, 0)),
                  pl.BlockSpec((None, t, dk), lambda b, h, i: (h, b, 0)),
                  pl.BlockSpec((None, t, vd), lambda b, h, i: (h, b, 0))],
        out_specs=pl.BlockSpec((None, tq, vd), lambda b, h, i: (h, b * n_blk + i, 0)),
        out_shape=jax.ShapeDtypeStruct((nh, m, vd), BF16),
        compiler_params=_cparams("parallel", "parallel", "arbitrary"),
    )(qh, kh, vh)


def _causal_conv_tile(x_ref, xpad, cw_ref, cb_ref, cbuf_ref, rows):
    pad = 8
    xpad[pad:pad + rows, :] = x_ref[...]
    xc = cb_ref[...]
    for i in range(CONV_WIDTH):
        o = pad - (CONV_WIDTH - 1) + i
        xc = xc + xpad[o:o + rows, :] * cw_ref[i:i + 1, :]
    cbuf_ref[...] = xpad[pad + rows - (CONV_WIDTH - 1):pad + rows, :]
    xpad[0:pad, :] = xpad[rows:rows + pad, :]
    return xc


def _gated_group_norm(y, z, ng_ref, y_ref, n_groups):
    y = y * _silu(z)
    gs = y.shape[-1] // n_groups
    for g in range(n_groups):
        y_ref[:, g * gs:(g + 1) * gs] = _rms(y[:, g * gs:(g + 1) * gs], ng_ref[:, g * gs:(g + 1) * gs]).astype(y_ref.dtype)


def _ssd_prompt_kernel(xbc_ref, z_ref, dt_ref, cw_ref, cb_ref, dtb_ref, alog_ref, dsk_ref, ng_ref, ltri_ref,
                       y_ref, hlast_ref, cbuf_ref, xpad, state, yscr, *, q, nh, hd, ns, n_groups):
    c = pl.program_id(1)

    @pl.when(c == 0)
    def _():
        state[...] = jnp.zeros_like(state)
        xpad[0:8, :] = jnp.zeros((8, xpad.shape[1]), F32)

    xc = _silu(_causal_conv_tile(xbc_ref, xpad, cw_ref, cb_ref, cbuf_ref, q))
    inner = nh * hd
    hpg = nh // n_groups
    xs = xc[:, :inner]
    bm = xc[:, inner:inner + n_groups * ns]
    cm = xc[:, inner + n_groups * ns:]
    lane = lax.broadcasted_iota(jnp.int32, (1, LANES), 1)
    dtv = _softplus(dt_ref[...] + dtb_ref[...])
    a = jnp.where(lane < nh, -jnp.exp(alog_ref[...]), 0.0)
    ltri = ltri_ref[...]
    cum = sum(_dot(ltri, part) for part in _split3(dtv * a))
    cum_t = cum.T
    dt_t = dtv.T
    cum_last = cum[q - 1:q, :]
    to_end = jnp.exp(cum_last - cum) * dtv
    ecum = jnp.exp(cum)
    elast = jnp.exp(cum_last)
    causal = lax.broadcasted_iota(jnp.int32, (q, q), 0) >= lax.broadcasted_iota(jnp.int32, (q, q), 1)
    for g in range(n_groups):
        cmg = cm[:, g * ns:(g + 1) * ns].astype(BF16)
        bmg = bm[:, g * ns:(g + 1) * ns].astype(BF16)
        cb = lax.dot_general(cmg, bmg, _NT, preferred_element_type=F32)
        for r in range(g * hpg, (g + 1) * hpg):
            seg = cum[:, r:r + 1] - cum_t[r:r + 1, :]
            decay = jnp.exp(jnp.where(causal, seg, -jnp.inf))
            mix = cb * decay * dt_t[r:r + 1, :]
            xh = xs[:, r * hd:(r + 1) * hd]
            hprev = state[r]
            y = _dot(mix.astype(BF16), xh.astype(BF16))
            y = y + lax.dot_general(cmg, hprev.astype(BF16), _NT, preferred_element_type=F32) * ecum[:, r:r + 1]
            yscr[:, r * hd:(r + 1) * hd] = y + dsk_ref[:, r * hd:(r + 1) * hd] * xh
            xw = (xh * to_end[:, r:r + 1]).astype(BF16)
            state[r] = elast[:, r:r + 1] * hprev + lax.dot_general(xw, bmg, _TN, preferred_element_type=F32)
    hlast_ref[...] = state[...]
    _gated_group_norm(yscr[...], z_ref[...], ng_ref, y_ref, n_groups)


def _lower_tri_ones(n):
    r = lax.broadcasted_iota(jnp.int32, (n, n), 0)
    c = lax.broadcasted_iota(jnp.int32, (n, n), 1)
    return (r >= c).astype(BF16)


def _ssd_prompt(xbc, z, dt, conv_w, conv_b, cw, norm_g, *, nb, q, state_dim):
    m, conv_dim = xbc.shape
    inner, nh = cw["inner"], cw["n_ssd"]
    hd = inner // nh
    n_groups = (conv_dim - inner) // (2 * state_dim)
    t = m // nb
    nc = t // q
    row = lambda b, c: (b * nc + c, 0)
    return pl.pallas_call(
        functools.partial(_ssd_prompt_kernel, q=q, nh=nh, hd=hd, ns=state_dim, n_groups=n_groups),
        grid=(nb, nc),
        in_specs=[pl.BlockSpec((q, conv_dim), row), pl.BlockSpec((q, inner), row), pl.BlockSpec((q, LANES), row),
                  _full((CONV_WIDTH, conv_dim)), _full((1, conv_dim)), _full((1, LANES)), _full((1, LANES)),
                  _full((1, inner)), _full((1, inner)), _full((q, q))],
        out_specs=[pl.BlockSpec((q, inner), row),
                   pl.BlockSpec((None, nh, hd, state_dim), lambda b, c: (b, 0, 0, 0)),
                   pl.BlockSpec((None, CONV_WIDTH - 1, conv_dim), lambda b, c: (b, 0, 0))],
        out_shape=[jax.ShapeDtypeStruct((m, inner), BF16), jax.ShapeDtypeStruct((nb, nh, hd, state_dim), F32),
                   jax.ShapeDtypeStruct((nb, CONV_WIDTH - 1, conv_dim), F32)],
        scratch_shapes=[pltpu.VMEM((q + 8, conv_dim), F32), pltpu.VMEM((nh, hd, state_dim), F32),
                        pltpu.VMEM((q, inner), F32)],
        compiler_params=_cparams("parallel", "arbitrary"),
    )(xbc, z, dt, conv_w, conv_b.reshape(1, conv_dim), cw["dtb128"], cw["alog128"], cw["dsk"],
      norm_g.reshape(1, inner), _lower_tri_ones(q))


def _sb_sample_kernel(pt_ref, qbd_ref, uu_ref, *rest, npg, dk):
    pages = rest[:npg]
    o_ref, z_scr, r_scr, acc_scr = rest[npg:]
    c = pl.program_id(1)
    nrow = qbd_ref.shape[0]

    @pl.when(c == 0)
    def _():
        r_scr[...] = jnp.zeros_like(r_scr)
        acc_scr[...] = jnp.zeros_like(acc_scr)

    qbd = qbd_ref[...]
    for j in range(npg):
        kp = pages[j][:, :dk].astype(BF16)
        z_scr[j * nrow:(j + 1) * nrow, :] = lax.dot_general(qbd, kp, _NT, preferred_element_type=F32)
    z = z_scr[...]
    lk = -(jnp.maximum(z, 0.0) + jnp.log(1.0 + jnp.exp(-jnp.abs(z))))
    hi, lo = _split2(lk)
    tl = _dot(jnp.concatenate([hi, lo], axis=1), uu_ref[...])
    tot = jnp.broadcast_to(tl[:, 0:1], tl.shape)
    r = r_scr[...]
    carries = [None] * npg
    for j in reversed(range(npg)):
        carries[j] = r
        r = r + tot[j * nrow:(j + 1) * nrow, :]
    r_scr[...] = r
    w = jnp.exp(z + tl + jnp.concatenate(carries, axis=0))
    acc = acc_scr[...]
    for j in range(npg):
        vp = pages[j][:, dk:].astype(BF16)
        acc = acc + _dot(w[j * nrow:(j + 1) * nrow, :].astype(BF16), vp)
    acc_scr[...] = acc
    o_ref[...] = acc


def _sb_sample(qh, cache, layer, page_table, *, n_kv, npg=32):
    n_q, s, hd = qh.shape
    grp = n_q // n_kv
    dk = n_kv * hd
    page = cache.shape[2]
    n_pages = page_table.shape[1]
    npg = min(npg, n_pages)
    nch = n_pages // npg
    eye = jnp.eye(n_kv, dtype=qh.dtype)
    qbd = jnp.einsum('hgsd,hk->sghkd', qh.reshape(n_kv, grp, s, hd), eye).reshape(s, n_q, dk)

    def page_spec(j):
        return pl.BlockSpec((None, None, page, 2 * dk),
                            lambda i, c, pt: (layer, pt[i, (nch - 1 - c) * npg + j], 0, 0))

    grid_spec = pltpu.PrefetchScalarGridSpec(
        num_scalar_prefetch=1, grid=(s, nch),
        in_specs=[pl.BlockSpec((None, n_q, dk), lambda i, c, pt: (i, 0, 0)),
                  pl.BlockSpec((2 * page, page), lambda i, c, pt: (0, 0))] + [page_spec(j) for j in range(npg)],
        out_specs=pl.BlockSpec((None, n_q, dk), lambda i, c, pt: (i, 0, 0)),
        scratch_shapes=[pltpu.VMEM((npg * n_q, page), F32), pltpu.VMEM((n_q, page), F32), pltpu.VMEM((n_q, dk), F32)])
    og = pl.pallas_call(
        functools.partial(_sb_sample_kernel, npg=npg, dk=dk),
        grid_spec=grid_spec,
        out_shape=jax.ShapeDtypeStruct((s, n_q, dk), F32),
        compiler_params=_cparams("parallel", "arbitrary"),
    )(page_table, qbd, _rev_cumsum_ones(page), *([cache] * npg))
    idx = jnp.arange(n_kv)
    o = og.reshape(s, grp, n_kv, n_kv, hd)[:, :, idx, idx, :]
    return jnp.transpose(o, (2, 1, 0, 3)).reshape(n_q, s, hd).astype(BF16)


def _conv_step(x, buf_ref, nbuf_ref, cw_ref, cb_ref):
    xc = cb_ref[...]
    for i in range(CONV_WIDTH - 1):
        xc = xc + buf_ref[i] * cw_ref[i:i + 1, :]
        if i > 0:
            nbuf_ref[i - 1] = buf_ref[i]
    nbuf_ref[CONV_WIDTH - 2] = x
    return xc + x * cw_ref[CONV_WIDTH - 1:CONV_WIDTH, :]


def _rglru_step_kernel(xb_ref, gate_ref, buf_ref, h0_ref, cw_ref, cb_ref, wbd_ref, bias_ref, lam_ref,
                       y_ref, h_ref, nbuf_ref):
    xc = _conv_step(xb_ref[...], buf_ref, nbuf_ref, cw_ref, cb_ref)
    a, b = _lru_gates(xc, wbd_ref, bias_ref, lam_ref)
    h = a * h0_ref[...] + b
    h_ref[...] = h
    y_ref[...] = (_gelu_tanh(gate_ref[...]) * h).astype(y_ref.dtype)


def _rglru_step(xb, gate, buf, h0, conv_w, conv_b, wbd, bias, lam):
    s, w = xb.shape
    return pl.pallas_call(
        _rglru_step_kernel,
        out_shape=[jax.ShapeDtypeStruct((s, w), BF16), jax.ShapeDtypeStruct((s, w), F32),
                   jax.ShapeDtypeStruct((CONV_WIDTH - 1, s, w), F32)],
        compiler_params=pltpu.CompilerParams(vmem_limit_bytes=VMEM_LIMIT_BYTES),
    )(xb, gate, buf, h0, conv_w, conv_b.reshape(1, w), wbd, bias, lam.reshape(1, w))


def _mla_sample_kernel(pt_ref, qn_ref, qr_ref, new_ref, wk_ref, wv_ref, ones_n_ref, ones_r_ref, *rest,
                       n_pages, page, kv_rank, rope, qk, tk):
    pages = rest[:n_pages]
    o_ref, ckv_scr, kpe_scr, s_scr = rest[n_pages:]
    nrow = qn_ref.shape[0]
    zpad = jnp.zeros((LANES - nrow, qn_ref.shape[1]), BF16)
    qn = jnp.concatenate([qn_ref[...], zpad], axis=0)
    qr = jnp.concatenate([qr_ref[...], jnp.zeros((LANES - nrow, LANES), BF16)], axis=0)
    rope_pad = jnp.zeros((page, LANES - rope), F32)
    for j in range(n_pages):
        ckv_scr[j * page:(j + 1) * page, :] = pages[j][:, :kv_rank].astype(BF16)
        kpe_scr[j * page:(j + 1) * page, :] = jnp.concatenate([pages[j][:, kv_rank:], rope_pad], axis=1)

    def scores(cb, kp):
        kn = _dot(cb, wk_ref[...])
        h1, l1 = _split2(kn * kn)
        h2, l2 = _split2(kp * kp)
        ssq = (_dot(h1, ones_n_ref[...]) + _dot(l1, ones_n_ref[...])
               + _dot(h2, ones_r_ref[...]) + _dot(l2, ones_r_ref[...]))
        s = (lax.dot_general(kn.astype(BF16), qn, _NT, preferred_element_type=F32)
             + lax.dot_general(kp.astype(BF16), qr, _NT, preferred_element_type=F32))
        return s * lax.rsqrt(ssq * (1.0 / qk) + NORM_EPS)

    def chunk(i, m):
        off = pl.multiple_of(i * tk, tk)
        s = scores(ckv_scr[pl.ds(off, tk), :], kpe_scr[pl.ds(off, tk), :])
        s_scr[pl.ds(off, tk), :] = s
        return jnp.maximum(m, jnp.max(s, axis=0, keepdims=True))

    n_keys = n_pages * page
    new = new_ref[...]
    cb_new = new[:, :kv_rank].astype(BF16)
    kp_new = jnp.concatenate([new[:, kv_rank:], jnp.zeros((new.shape[0], LANES - rope), F32)], axis=1)
    first = lax.broadcasted_iota(jnp.int32, (new.shape[0], LANES), 0) == 0
    s_new = jnp.where(first, scores(cb_new, kp_new), -jnp.inf)
    m = lax.fori_loop(0, n_keys // tk, chunk, jnp.max(s_new, axis=0, keepdims=True))
    p_new = jnp.exp(s_new - m)

    def sum_chunk(i, l):
        off = pl.multiple_of(i * tk, tk)
        p = jnp.exp(s_scr[pl.ds(off, tk), :] - m)
        s_scr[pl.ds(off, tk), :] = p
        return l + jnp.sum(p, axis=0, keepdims=True)

    l = lax.fori_loop(0, n_keys // tk, sum_chunk, jnp.sum(p_new, axis=0, keepdims=True))
    inv = 1.0 / l

    def pv_chunk(i, acc):
        off = pl.multiple_of(i * tk, tk)
        p = (s_scr[pl.ds(off, tk), :] * inv).astype(BF16)
        return acc + lax.dot_general(p, ckv_scr[pl.ds(off, tk), :], _TN, preferred_element_type=F32)

    o_lat = lax.dot_general((p_new * inv).astype(BF16), cb_new, _TN, preferred_element_type=F32)
    o_lat = lax.fori_loop(0, n_keys // tk, pv_chunk, o_lat)
    hi, lo = _split2(o_lat[:nrow, :])
    o_ref[...] = _dot(hi, wv_ref[...]) + _dot(lo, wv_ref[...])


def _mla_sample(qh, lat_new, cache, layer, page_table, cw, gk):
    nh, s, _ = qh.shape
    nope, rope, vd, kv_rank, qk = cw["nope"], cw["rope"], cw["vd"], cw["kv_rank"], cw["qk"]
    page = cache.shape[2]
    n_pages = page_table.shape[1]
    qg = qh.astype(F32) * cw["gk128"].reshape(1, 1, LANES)
    eye = jnp.eye(nh, dtype=F32)
    qn = jnp.einsum('hsd,hk->shkd', qg[:, :, :nope], eye).reshape(s, nh, nh * nope).astype(BF16)
    qr = jnp.concatenate([qg[:, :, nope:qk], jnp.zeros((nh, s, LANES - rope), F32)], axis=2)
    qr = jnp.swapaxes(qr, 0, 1).astype(BF16)
    new = jnp.broadcast_to(lat_new[:, None, :], (s, 8, kv_rank + rope))
    lane = jnp.arange(LANES)
    ones_n = ((jnp.arange(nh * nope)[:, None] // nope) == lane[None, :]).astype(BF16)
    ones_r = ((jnp.arange(LANES)[:, None] < rope) & (lane[None, :] < nh)).astype(BF16)
    tk = min(512, n_pages * page)

    def page_spec(j):
        return pl.BlockSpec((None, None, page, kv_rank + rope), lambda i, pt: (layer, pt[i, j], 0, 0))

    grid_spec = pltpu.PrefetchScalarGridSpec(
        num_scalar_prefetch=1, grid=(s,),
        in_specs=[pl.BlockSpec((None, nh, nh * nope), lambda i, pt: (i, 0, 0)),
                  pl.BlockSpec((None, nh, LANES), lambda i, pt: (i, 0, 0)),
                  pl.BlockSpec((None, 8, kv_rank + rope), lambda i, pt: (i, 0, 0)),
                  pl.BlockSpec(cw["wk"].shape, lambda i, pt: (0, 0)),
                  pl.BlockSpec(cw["wv"].shape, lambda i, pt: (0, 0)),
                  pl.BlockSpec(ones_n.shape, lambda i, pt: (0, 0)),
                  pl.BlockSpec(ones_r.shape, lambda i, pt: (0, 0))] + [page_spec(j) for j in range(n_pages)],
        out_specs=pl.BlockSpec((None, nh, nh * vd), lambda i, pt: (i, 0, 0)),
        scratch_shapes=[pltpu.VMEM((n_pages * page, kv_rank), BF16), pltpu.VMEM((n_pages * page, LANES), F32),
                        pltpu.VMEM((n_pages * page, LANES), F32)])
    om = pl.pallas_call(
        functools.partial(_mla_sample_kernel, n_pages=n_pages, page=page, kv_rank=kv_rank, rope=rope, qk=qk, tk=tk),
        grid_spec=grid_spec,
        out_shape=jax.ShapeDtypeStruct((s, nh, nh * vd), F32),
        compiler_params=_cparams("parallel"),
    )(page_table, qn, qr, new, cw["wk"], cw["wv"], ones_n, ones_r, *([cache] * n_pages))
    idx = jnp.arange(nh)
    o = om.reshape(s, nh, nh, vd)[:, idx, idx, :]
    return jnp.swapaxes(o, 0, 1).astype(BF16)


def _ssd_step_kernel(xbc_ref, z_ref, dt_ref, buf_ref, h0_ref, cw_ref, cb_ref, dtb_ref, alog_ref, dsk_ref, ng_ref,
                     ex_ref, en_ref, ext_ref, y_ref, hnew_ref, nbuf_ref, yscr, *, nh, hd, ns, n_groups):
    xc = _silu(_conv_step(xbc_ref[...], buf_ref, nbuf_ref, cw_ref, cb_ref))
    inner = nh * hd
    hpg = nh // n_groups
    sz = hd * ns
    xs = xc[:, :inner]
    bm = xc[:, inner:inner + n_groups * ns]
    cm = xc[:, inner + n_groups * ns:]
    lane = lax.broadcasted_iota(jnp.int32, (1, LANES), 1)
    dtv = _softplus(dt_ref[...] + dtb_ref[...])
    a = jnp.where(lane < nh, -jnp.exp(alog_ref[...]), 0.0)
    da = jnp.exp(dtv * a)
    for g in range(n_groups):
        cmg = cm[:, g * ns:(g + 1) * ns]
        bmg = bm[:, g * ns:(g + 1) * ns]
        cbg = jnp.sum(cmg * bmg, axis=-1, keepdims=True)
        be = _dot(bmg.astype(BF16), en_ref[...])
        ce = _dot(cmg.astype(BF16), en_ref[...])
        for r in range(g * hpg, (g + 1) * hpg):
            xh = xs[:, r * hd:(r + 1) * hd]
            xe = _dot(xh.astype(BF16), ex_ref[...])
            dtr = dtv[:, r:r + 1]
            dar = da[:, r:r + 1]
            h0 = h0_ref[:, r * sz:(r + 1) * sz]
            hnew_ref[:, r * sz:(r + 1) * sz] = dar * h0 + (xe * dtr) * be
            hi, lo = _split2(ce * h0)
            y_off = (_dot(hi, ext_ref[...]) + _dot(lo, ext_ref[...])) * dar
            yscr[:, r * hd:(r + 1) * hd] = cbg * dtr * xh + y_off + dsk_ref[:, r * hd:(r + 1) * hd] * xh
    _gated_group_norm(yscr[...], z_ref[...], ng_ref, y_ref, n_groups)


def _ssd_step(xbc, z, dt, buf, h0, conv_w, conv_b, cw, norm_g, *, ts=16):
    s, conv_dim = xbc.shape
    _, nh, hd, ns = h0.shape
    inner = cw["inner"]
    n_groups = (conv_dim - inner) // (2 * ns)
    sz = hd * ns
    col = jnp.arange(sz)
    ex = ((col[None, :] // ns) == jnp.arange(hd)[:, None]).astype(BF16)
    en = ((col[None, :] % ns) == jnp.arange(ns)[:, None]).astype(BF16)
    ts = min(ts, s)
    row = lambda i: (i, 0)
    y, hnew, nbuf = pl.pallas_call(
        functools.partial(_ssd_step_kernel, nh=nh, hd=hd, ns=ns, n_groups=n_groups),
        grid=(s // ts,),
        in_specs=[pl.BlockSpec((ts, conv_dim), row), pl.BlockSpec((ts, inner), row), pl.BlockSpec((ts, LANES), row),
                  pl.BlockSpec((CONV_WIDTH - 1, ts, conv_dim), lambda i: (0, i, 0)),
                  pl.BlockSpec((ts, nh * sz), row), _full((CONV_WIDTH, conv_dim)), _full((1, conv_dim)),
                  _full((1, LANES)), _full((1, LANES)), _full((1, inner)), _full((1, inner)),
                  _full((hd, sz)), _full((ns, sz)), _full((sz, hd))],
        out_specs=[pl.BlockSpec((ts, inner), row), pl.BlockSpec((ts, nh * sz), row),
                   pl.BlockSpec((CONV_WIDTH - 1, ts, conv_dim), lambda i: (0, i, 0))],
        out_shape=[jax.ShapeDtypeStruct((s, inner), BF16), jax.ShapeDtypeStruct((s, nh * sz), F32),
                   jax.ShapeDtypeStruct((CONV_WIDTH - 1, s, conv_dim), F32)],
        scratch_shapes=[pltpu.VMEM((ts, inner), F32)],
        compiler_params=_cparams("parallel"),
    )(xbc, z, dt, buf, h0.reshape(s, nh * sz), conv_w, conv_b.reshape(1, conv_dim), cw["dtb128"], cw["alog128"],
      cw["dsk"], norm_g.reshape(1, inner), ex, en, ex.T)
    return y, hnew.reshape(s, nh, hd, ns), nbuf


def _prompt_tiles(t, hidden):
    pick = lambda want: max(c for c in (8, 16, 32, 64, 128, 256, 512, 1024) if c <= want and t % c == 0)
    th = hidden // 2 if hidden % (2 * LANES) == 0 else hidden
    return dict(tm_proj=pick(512), tm_post=pick(512), tq_sb=pick(256), tq_mla=pick(512), tt_lru=pick(512),
                ssd_chunk=pick(128), th=th)


def kernel(x_prompt, x_sample, cache_sb_kv, cache_mla_kv, page_table, state_lru_h, state_lru_conv, state_ssm_h, state_ssm_conv, norm_mix, norm_ffn, ab_w_in, ab_w_out, sb_q_gain, sb_k_gain, lru_conv_w, lru_conv_b, lru_wa, lru_ba, lru_wx, lru_bx, lru_lambda, cd_w_in, cd_w_out, mla_q_lat_gain, mla_w_uq, mla_kv_lat_gain, mla_w_ukv, mla_q_gain, mla_k_gain, ssd_conv_w, ssd_conv_b, ssd_dt_bias, ssd_a_log, ssd_d, ssd_norm_gain, ffn_w_in, ffn_w_out):
    nb, t, d = x_prompt.shape
    ns = x_sample.shape[0]
    depth = norm_mix.shape[0]
    page = cache_sb_kv.shape[2]
    past = page_table.shape[1] * page
    n_kv, hd = cache_sb_kv.shape[4], cache_sb_kv.shape[5]
    n_q = (ab_w_in.shape[2] - 2 * n_kv * hd - 2 * lru_lambda.shape[1]) // hd
    w_lru = lru_lambda.shape[1]
    sizes = _prompt_tiles(t, ffn_w_out.shape[1])

    xp = x_prompt.reshape(nb * t, d)
    xs = x_sample.reshape(ns, d)
    cache_sb = cache_sb_kv.reshape(cache_sb_kv.shape[:3] + (2 * n_kv * hd,))
    cos_p, sin_p = _rope_tables(jnp.arange(t), mla_q_gain.shape[1] - (cache_mla_kv.shape[3] - mla_kv_lat_gain.shape[1]),
                                cache_mla_kv.shape[3] - mla_kv_lat_gain.shape[1])
    cos_s, sin_s = _rope_tables(jnp.full((ns,), past), mla_q_gain.shape[1] - (cache_mla_kv.shape[3] - mla_kv_lat_gain.shape[1]),
                                cache_mla_kv.shape[3] - mla_kv_lat_gain.shape[1])
    outs = {k: [] for k in ("sb_p", "sb_s", "lh_p", "lh_s", "lc_p", "lc_s", "ml_p", "ml_s", "sh_p", "sh_s", "sc_p", "sc_s")}
    for li in range(depth):
        w_ffn_in = ffn_w_in[li].astype(BF16)
        w_ffn_out = ffn_w_out[li].astype(BF16)
        if li % 2 == 0:
            e = li // 2
            w_in = ab_w_in[e].astype(BF16)
            w_out = ab_w_out[e].astype(BF16)
            wbd, bias = _lru_weights(lru_wa[e], lru_ba[e], lru_wx[e], lru_bx[e])
            qh, kh, vh, kv, xb, gate = _ab_inproj(xp, norm_mix[li], w_in, sb_q_gain[e], sb_k_gain[e],
                                                  nq=n_q, nk=n_kv, hd=hd, w_lru=w_lru, tm=sizes["tm_proj"])
            oh = _sb_prompt(qh, kh, vh, nb=nb, tq=sizes["tq_sb"])
            y2, hl, cbuf = _rglru_prompt(xb, gate, lru_conv_w[e], lru_conv_b[e], wbd, bias, lru_lambda[e],
                                         nb=nb, tt=sizes["tt_lru"])
            xp = _post(xp, oh, y2, w_out, norm_ffn[li], w_ffn_in, w_ffn_out, tm=sizes["tm_post"], th=sizes["th"])
            outs["sb_p"].append(kv.reshape(nb, t, 2, n_kv, hd))
            outs["lh_p"].append(hl.reshape(nb, w_lru))
            outs["lc_p"].append(cbuf)
            qh, kh, vh, kv, xb, gate = _ab_inproj(xs, norm_mix[li], w_in, sb_q_gain[e], sb_k_gain[e],
                                                  nq=n_q, nk=n_kv, hd=hd, w_lru=w_lru, tm=ns)
            oh = _sb_sample(qh, cache_sb, e, page_table, n_kv=n_kv)
            y2, hl, cbuf = _rglru_step(xb, gate, jnp.swapaxes(state_lru_conv[e], 0, 1), state_lru_h[e],
                                       lru_conv_w[e], lru_conv_b[e], wbd, bias, lru_lambda[e])
            xs = _post(xs, oh, y2, w_out, norm_ffn[li], w_ffn_in, w_ffn_out, tm=ns, th=sizes["th"])
            outs["sb_s"].append(kv.reshape(ns, 1, 2, n_kv, hd))
            outs["lh_s"].append(hl)
            outs["lc_s"].append(jnp.swapaxes(cbuf, 0, 1))
        else:
            o = li // 2
            cw = _cd_weights(cd_w_in[o], mla_w_uq[o], mla_w_ukv[o], mla_q_gain[o], mla_k_gain[o], ssd_dt_bias[o],
                             ssd_a_log[o], ssd_d[o], kv_rank=mla_kv_lat_gain.shape[1], q_rank=mla_q_lat_gain.shape[1],
                             rope=cache_mla_kv.shape[3] - mla_kv_lat_gain.shape[1], inner=ssd_norm_gain.shape[1],
                             conv_dim=ssd_conv_w.shape[2])
            w_out = cd_w_out[o].astype(BF16)
            qh, kh, vh, lat, z, xbc, dt = _cd_inproj(xp, norm_mix[li], cw, mla_q_lat_gain[o], mla_kv_lat_gain[o],
                                                     cos_p, sin_p, tm=sizes["tm_proj"])
            oh = _mla_prompt(qh, kh, vh, nb=nb, tq=sizes["tq_mla"])
            y2, hl, cbuf = _ssd_prompt(xbc, z, dt, ssd_conv_w[o], ssd_conv_b[o], cw, ssd_norm_gain[o],
                                       nb=nb, q=sizes["ssd_chunk"], state_dim=state_ssm_h.shape[4])
            xp = _post(xp, oh, y2, w_out, norm_ffn[li], w_ffn_in, w_ffn_out, tm=sizes["tm_post"], th=sizes["th"])
            outs["ml_p"].append(lat.reshape(nb, t, lat.shape[1]))
            outs["sh_p"].append(hl)
            outs["sc_p"].append(cbuf)
            qh, kh, vh, lat, z, xbc, dt = _cd_inproj(xs, norm_mix[li], cw, mla_q_lat_gain[o], mla_kv_lat_gain[o],
                                                     cos_s, sin_s, tm=ns)
            oh = _mla_sample(qh, lat, cache_mla_kv, o, page_table, cw, mla_k_gain[o])
            y2, hl, cbuf = _ssd_step(xbc, z, dt, jnp.swapaxes(state_ssm_conv[o], 0, 1), state_ssm_h[o],
                                     ssd_conv_w[o], ssd_conv_b[o], cw, ssd_norm_gain[o])
            xs = _post(xs, oh, y2, w_out, norm_ffn[li], w_ffn_in, w_ffn_out, tm=ns, th=sizes["th"])
            outs["ml_s"].append(lat.reshape(ns, 1, lat.shape[1]))
            outs["sh_s"].append(hl)
            outs["sc_s"].append(jnp.swapaxes(cbuf, 0, 1))
    st = {k: jnp.stack(v) for k, v in outs.items()}
    return (xp.reshape(nb, t, d), xs.reshape(ns, 1, d), st["sb_p"], st["sb_s"], st["lh_p"], st["lh_s"], st["lc_p"],
            st["lc_s"], st["ml_p"], st["ml_s"], st["sh_p"], st["sh_s"], st["sc_p"], st["sc_s"])
```

```python
import functools
import math

import jax
import jax.numpy as jnp
from jax import lax
from jax.experimental import pallas as pl
from jax.experimental.pallas import tpu as pltpu

F32 = jnp.float32
BF16 = jnp.bfloat16
NORM_EPS = 1e-6
LRU_C = 8.0
ROPE_THETA = 10000.0
CONV_WIDTH = 4
LANES = 128
VMEM_LIMIT_BYTES = 56 * 1024 * 1024
_NT = (((1,), (1,)), ((), ()))
_TN = (((0,), (0,)), ((), ()))


def _cparams(*sem):
    return pltpu.CompilerParams(dimension_semantics=sem, vmem_limit_bytes=VMEM_LIMIT_BYTES)


def _dot(a, b):
    return jnp.dot(a, b, preferred_element_type=F32)


def _split2(x):
    hi = x.astype(BF16)
    lo = (x - hi.astype(F32)).astype(BF16)
    return hi, lo


def _split3(x):
    hi = x.astype(BF16)
    r = x - hi.astype(F32)
    mid = r.astype(BF16)
    lo = (r - mid.astype(F32)).astype(BF16)
    return hi, mid, lo


def _rms(x, g):
    return x * lax.rsqrt(jnp.mean(x * x, axis=-1, keepdims=True) + NORM_EPS) * g


def _sigmoid(x):
    return 1.0 / (1.0 + jnp.exp(-x))


def _silu(x):
    return x * _sigmoid(x)


def _softplus(x):
    return jnp.maximum(x, 0.0) + jnp.log(1.0 + jnp.exp(-jnp.abs(x)))


def _gelu_tanh(x):
    c = math.sqrt(2.0 / math.pi)
    return x * (0.5 * (1.0 + jnp.tanh(c * (x + 0.044715 * (x * x * x)))))


def _block_ones(n, blk, dtype=BF16):
    r = lax.broadcasted_iota(jnp.int32, (n, n), 0) // blk
    c = lax.broadcasted_iota(jnp.int32, (n, n), 1) // blk
    return (r == c).astype(dtype)


def _full(shape):
    nd = len(shape)
    return pl.BlockSpec(shape, lambda *_: (0,) * nd)


def _ab_inproj_kernel(x_ref, g_ref, w_ref, gq_ref, gk_ref, pq_ref, pk_ref,
                      qh_ref, kh_ref, vh_ref, kv_ref, xb_ref, gate_ref, *, nq, nk, hd):
    h = _rms(x_ref[...], g_ref[...]).astype(BF16)
    p = _dot(h, w_ref[...])
    dq, dk = nq * hd, nk * hd
    q = p[:, :dq]
    k = p[:, dq:dq + dk]
    v = p[:, dq + dk:dq + 2 * dk]

    def head_norm(t, ones_ref, gain):
        hi, lo = _split2(t * t)
        ms = (_dot(hi, ones_ref[...]) + _dot(lo, ones_ref[...])) * (1.0 / hd)
        return t * lax.rsqrt(ms + NORM_EPS) * gain

    qn = head_norm(q, pq_ref, gq_ref[...])
    kn = head_norm(k, pk_ref, gk_ref[...])
    for i in range(nq):
        qh_ref[i] = qn[:, i * hd:(i + 1) * hd].astype(BF16)
    for i in range(nk):
        kh_ref[i] = kn[:, i * hd:(i + 1) * hd].astype(BF16)
        vh_ref[i] = v[:, i * hd:(i + 1) * hd].astype(BF16)
    kv_ref[:, :dk] = kn
    kv_ref[:, dk:] = v
    w_lru = xb_ref.shape[-1]
    xb_ref[...] = p[:, dq + 2 * dk:dq + 2 * dk + w_lru]
    gate_ref[...] = p[:, dq + 2 * dk + w_lru:]


def _ab_inproj(x, g, w_bf, gq, gk, *, nq, nk, hd, w_lru, tm):
    m, d = x.shape
    n = w_bf.shape[1]
    dq, dk = nq * hd, nk * hd
    scale = -math.log2(math.e) * hd ** -0.5
    gq_t = (jnp.tile(gq, nq) * scale).reshape(1, dq)
    gk_t = jnp.tile(gk, nk).reshape(1, dk)
    row = lambda i: (i, 0)
    head = lambda i: (0, i, 0)
    return pl.pallas_call(
        functools.partial(_ab_inproj_kernel, nq=nq, nk=nk, hd=hd),
        grid=(m // tm,),
        in_specs=[pl.BlockSpec((tm, d), row), _full((1, d)), _full((d, n)), _full((1, dq)), _full((1, dk)),
                  _full((dq, dq)), _full((dk, dk))],
        out_specs=[pl.BlockSpec((nq, tm, hd), head), pl.BlockSpec((nk, tm, hd), head),
                   pl.BlockSpec((nk, tm, hd), head), pl.BlockSpec((tm, 2 * dk), row),
                   pl.BlockSpec((tm, w_lru), row), pl.BlockSpec((tm, w_lru), row)],
        out_shape=[jax.ShapeDtypeStruct((nq, m, hd), BF16), jax.ShapeDtypeStruct((nk, m, hd), BF16),
                   jax.ShapeDtypeStruct((nk, m, hd), BF16), jax.ShapeDtypeStruct((m, 2 * dk), F32),
                   jax.ShapeDtypeStruct((m, w_lru), F32), jax.ShapeDtypeStruct((m, w_lru), F32)],
        compiler_params=_cparams("parallel"),
    )(x, g.reshape(1, d), w_bf, gq_t, gk_t, _block_ones(dq, hd), _block_ones(dk, hd))


def _neg_abs(x):
    bits = lax.bitcast_convert_type(x, jnp.uint32) | jnp.uint32(0x80000000)
    return lax.bitcast_convert_type(bits, F32)


def _sb_log2_keep(nz):
    return jnp.minimum(nz, 0.0) - jnp.log2(1.0 + jnp.exp2(_neg_abs(nz)))


def _sb_prompt_kernel(q_ref, k_ref, v_ref, uu_ref, o_ref, nz_buf, x_buf, w_buf, r_scr, acc_scr, *, tq, grp, hd):
    i = pl.program_id(2)
    rows = grp * tq
    q2 = q_ref[...].reshape(rows, hd)
    t_idx = lax.broadcasted_iota(jnp.int32, (grp, tq, tq), 1).reshape(rows, tq)
    mask = lax.broadcasted_iota(jnp.int32, (rows, tq), 1) < t_idx

    def block(t):
        return pl.ds(pl.multiple_of((i - t) * tq, tq), tq)

    halves = [slice(c * (rows // 2), (c + 1) * (rows // 2)) for c in range(2)]

    def scores(t, msk=None):
        k = k_ref[block(t), :]
        nzs = [lax.dot_general(q2[h], k, _NT, preferred_element_type=F32) for h in halves]
        for h, nz in zip(halves, nzs):
            lk = _sb_log2_keep(nz)
            if msk is not None:
                lk = jnp.where(msk[h], lk, 0.0)
            hi, lo = _split2(lk)
            nz_buf[h, :] = nz
            x_buf[h, :tq] = hi
            x_buf[h, tq:] = lo

    def weights(msk=None):
        uu = uu_ref[...]
        tails = [_dot(x_buf[h, :], uu) for h in halves]
        for h, tail in zip(halves, tails):
            tail = tail + r_scr[h, :]
            w = jnp.exp2(tail - nz_buf[h, :])
            if msk is not None:
                w = jnp.where(msk[h], w, 0.0)
            w_buf[h, :] = w.astype(BF16)
            r_scr[h, :] = tail[:, 0:1]

    def values(t):
        v = v_ref[block(t), :]
        outs = [_dot(w_buf[h, :], v) for h in halves]
        for h, o in zip(halves, outs):
            acc_scr[h, :] += o

    r_scr[...] = jnp.zeros_like(r_scr)
    acc_scr[...] = jnp.zeros_like(acc_scr)
    scores(0, mask)
    weights(mask)

    @pl.when(i >= 1)
    def _():
        scores(1)

    def steady(t, carry):
        values(t - 2)
        weights()
        scores(t)
        return carry

    lax.fori_loop(2, i + 1, steady, 0)

    @pl.when(i >= 1)
    def _():
        values(i - 1)
        weights()

    values(i)
    o_ref[...] = acc_scr[...].reshape(grp, tq, hd).astype(o_ref.dtype)


def _rev_cumsum_ones(n):
    s = lax.broadcasted_iota(jnp.int32, (n, n), 0)
    j = lax.broadcasted_iota(jnp.int32, (n, n), 1)
    u = (s >= j).astype(BF16)
    return jnp.concatenate([u, u], axis=0)


def _sb_prompt(qh, kh, vh, *, nb, tq):
    nq, m, hd = qh.shape
    nk = kh.shape[0]
    grp = nq // nk
    t = m // nb
    n_blk = t // tq
    return pl.pallas_call(
        functools.partial(_sb_prompt_kernel, tq=tq, grp=grp, hd=hd),
        grid=(nb, nk, n_blk),
        in_specs=[pl.BlockSpec((grp, tq, hd), lambda b, h, i: (h, b * n_blk + i, 0)),
                  pl.BlockSpec((None, t, hd), lambda b, h, i: (h, b, 0)),
                  pl.BlockSpec((None, t, hd), lambda b, h, i: (h, b, 0)),
                  _full((2 * tq, tq))],
        out_specs=pl.BlockSpec((grp, tq, hd), lambda b, h, i: (h, b * n_blk + i, 0)),
        out_shape=jax.ShapeDtypeStruct((nq, m, hd), BF16),
        scratch_shapes=[pltpu.VMEM((grp * tq, tq), F32), pltpu.VMEM((grp * tq, 2 * tq), BF16),
                        pltpu.VMEM((grp * tq, tq), BF16), pltpu.VMEM((grp * tq, 1), F32),
                        pltpu.VMEM((grp * tq, hd), F32)],
        compiler_params=_cparams("parallel", "parallel", "arbitrary"),
        name="sb_prompt",
    )(qh, kh, vh, _rev_cumsum_ones(tq))


def _lru_gates(xc, wbd_ref, bias_ref, lam_ref):
    w = xc.shape[-1]
    ra = _dot(xc.astype(BF16), wbd_ref[...]) + bias_ref[...]
    r = _sigmoid(ra[:, :w])
    ig = _sigmoid(ra[:, w:])
    log_a = (-LRU_C) * r * _softplus(-lam_ref[...])
    a = jnp.exp(log_a)
    b = jnp.sqrt(-jnp.tanh(log_a) * (a * a + 1.0)) * (ig * xc)
    return a, b


def _rglru_prompt_kernel(xb_ref, gate_ref, cw_ref, cb_ref, wbd_ref, bias_ref, lam_ref,
                         y_ref, hlast_ref, cbuf_ref, xpad, a_scr, b_scr, h_scr, *, tt):
    t = pl.program_id(1)
    pad = 8

    @pl.when(t == 0)
    def _():
        h_scr[...] = jnp.zeros_like(h_scr)
        xpad[0:pad, :] = jnp.zeros((pad, xpad.shape[1]), F32)

    xpad[pad:pad + tt, :] = xb_ref[...]
    xc = cb_ref[...]
    for i in range(CONV_WIDTH):
        o = pad - (CONV_WIDTH - 1) + i
        xc = xc + xpad[o:o + tt, :] * cw_ref[i:i + 1, :]
    cbuf_ref[...] = xpad[pad + tt - (CONV_WIDTH - 1):pad + tt, :]
    xpad[0:pad, :] = xpad[tt:tt + pad, :]
    a, b = _lru_gates(xc, wbd_ref, bias_ref, lam_ref)
    a_scr[...] = a
    b_scr[...] = b

    def body(r, h):
        h = a_scr[pl.ds(r, 1), :] * h + b_scr[pl.ds(r, 1), :]
        b_scr[pl.ds(r, 1), :] = h
        return h

    h = lax.fori_loop(0, tt, body, h_scr[...], unroll=8)
    h_scr[...] = h
    hlast_ref[...] = h
    y_ref[...] = (_gelu_tanh(gate_ref[...]) * b_scr[...]).astype(y_ref.dtype)


def _lru_weights(wa, ba, wx, bx):
    nblk, bd, _ = wa.shape
    w = nblk * bd
    eye = jnp.eye(nblk, dtype=F32)

    def bdiag(m):
        return jnp.einsum('kij,kl->kilj', m, eye).reshape(w, w)

    wbd = jnp.concatenate([bdiag(wa), bdiag(wx)], axis=1).astype(BF16)
    bias = jnp.concatenate([ba, bx]).reshape(1, 2 * w)
    return wbd, bias


def _rglru_prompt(xb, gate, conv_w, conv_b, wbd, bias, lam, *, nb, tt):
    m, w = xb.shape
    t = m // nb
    nt = t // tt
    row = lambda b, i: (b * nt + i, 0)
    return pl.pallas_call(
        functools.partial(_rglru_prompt_kernel, tt=tt),
        grid=(nb, nt),
        in_specs=[pl.BlockSpec((tt, w), row), pl.BlockSpec((tt, w), row), _full((CONV_WIDTH, w)),
                  _full((1, w)), _full((w, 2 * w)), _full((1, 2 * w)), _full((1, w))],
        out_specs=[pl.BlockSpec((tt, w), row), pl.BlockSpec((None, 1, w), lambda b, i: (b, 0, 0)),
                   pl.BlockSpec((None, CONV_WIDTH - 1, w), lambda b, i: (b, 0, 0))],
        out_shape=[jax.ShapeDtypeStruct((m, w), BF16), jax.ShapeDtypeStruct((nb, 1, w), F32),
                   jax.ShapeDtypeStruct((nb, CONV_WIDTH - 1, w), F32)],
        scratch_shapes=[pltpu.VMEM((tt + 8, w), F32), pltpu.VMEM((tt, w), F32), pltpu.VMEM((tt, w), F32),
                        pltpu.VMEM((1, w), F32)],
        compiler_params=_cparams("parallel", "arbitrary"),
    )(xb, gate, conv_w, conv_b.reshape(1, w), wbd, bias, lam.reshape(1, w))


def _post_kernel(x_ref, oh_ref, y2_ref, wo_ref, g_ref, wu_ref, wg_ref, wd_ref, out_ref,
                 x1_scr, hb_scr, acc_scr):
    j = pl.program_id(1)
    nh, _, hd = oh_ref.shape

    @pl.when(j == 0)
    def _():
        mix = _dot(y2_ref[...], wo_ref[nh * hd:, :])
        for h in range(nh):
            mix = mix + _dot(oh_ref[h], wo_ref[h * hd:(h + 1) * hd, :])
        x1 = x_ref[...] + mix
        x1_scr[...] = x1
        hb_scr[...] = _rms(x1, g_ref[...]).astype(BF16)
        acc_scr[...] = jnp.zeros_like(acc_scr)

    hb = hb_scr[...]
    u = _dot(hb, wu_ref[...])
    gg = _dot(hb, wg_ref[...])
    acc_scr[...] += _dot((_silu(u) * gg).astype(BF16), wd_ref[...])

    @pl.when(j == pl.num_programs(1) - 1)
    def _():
        out_ref[...] = x1_scr[...] + acc_scr[...]


def _post(x, oh, y2, w_out_bf, g, w_in_bf, w_dn_bf, *, tm, th):
    m, d = x.shape
    nh, _, hd = oh.shape
    hid = w_dn_bf.shape[0]
    nj = hid // th
    return pl.pallas_call(
        _post_kernel,
        grid=(m // tm, nj),
        in_specs=[pl.BlockSpec((tm, d), lambda i, j: (i, 0)),
                  pl.BlockSpec((nh, tm, hd), lambda i, j: (0, i, 0)),
                  pl.BlockSpec((tm, y2.shape[1]), lambda i, j: (i, 0)),
                  pl.BlockSpec(w_out_bf.shape, lambda i, j: (0, 0)),
                  pl.BlockSpec((1, d), lambda i, j: (0, 0)),
                  pl.BlockSpec((d, th), lambda i, j: (0, j)),
                  pl.BlockSpec((d, th), lambda i, j: (0, nj + j)),
                  pl.BlockSpec((th, d), lambda i, j: (j, 0))],
        out_specs=pl.BlockSpec((tm, d), lambda i, j: (i, 0)),
        out_shape=jax.ShapeDtypeStruct((m, d), F32),
        scratch_shapes=[pltpu.VMEM((tm, d), F32), pltpu.VMEM((tm, d), BF16), pltpu.VMEM((tm, d), F32)],
        compiler_params=_cparams("parallel", "arbitrary"),
    )(x, oh, y2, w_out_bf, g.reshape(1, d), w_in_bf, w_in_bf, w_dn_bf)


def _rope_tables(pos, nope, rope):
    half = rope // 2
    inv = ROPE_THETA ** (-jnp.arange(half, dtype=F32) / half)
    ang = pos.astype(F32)[:, None] * inv[None, :]
    n = pos.shape[0]
    lead = jnp.zeros((n, nope), F32)
    trail = jnp.zeros((n, LANES - nope - rope), F32)
    cos = jnp.concatenate([lead, jnp.cos(ang), jnp.cos(ang), trail], axis=1)
    sin = jnp.concatenate([lead, jnp.sin(ang), jnp.sin(ang), trail], axis=1)
    return cos, sin


def _rot_partner(w):
    half = w.shape[-1] // 2
    return jnp.concatenate([-w[..., half:], w[..., :half]], axis=-1)


def _cd_weights(w_in, w_uq, w_ukv, gq, gk, dt_bias, a_log, d_skip, *, kv_rank, q_rank, rope, inner, conv_dim):
    d = w_in.shape[0]
    nh, qk = w_uq.shape[1], w_uq.shape[2]
    nope = qk - rope
    vd = w_ukv.shape[2] - nope
    n_ssd = dt_bias.shape[0]
    assert qk <= LANES and kv_rank == LANES and n_ssd <= LANES
    o = 0
    cq = w_in[:, o:o + q_rank]; o += q_rank
    ckv = w_in[:, o:o + kv_rank]; o += kv_rank
    kpe = w_in[:, o:o + rope]; o += rope
    z = w_in[:, o:o + inner]; o += inner
    xbc = w_in[:, o:o + conv_dim]; o += conv_dim
    dt = w_in[:, o:o + n_ssd]

    def on_rope_lanes(w):
        return jnp.concatenate([jnp.zeros((d, nope), F32), w, jnp.zeros((d, LANES - qk), F32)], axis=1)

    w_pad = jnp.concatenate([cq, ckv, on_rope_lanes(kpe), on_rope_lanes(_rot_partner(kpe)), z, xbc,
                             dt, jnp.zeros((d, LANES - n_ssd), F32)], axis=1).astype(BF16)
    zq = jnp.zeros((q_rank, nh, LANES - qk), F32)
    wqm = jnp.concatenate([w_uq, zq], axis=2).reshape(q_rank, nh * LANES).astype(BF16)
    wqp = jnp.concatenate([jnp.zeros((q_rank, nh, nope), F32), _rot_partner(w_uq[:, :, nope:]), zq],
                          axis=2).reshape(q_rank, nh * LANES).astype(BF16)
    wk = w_ukv[:, :, :nope]
    wk128 = jnp.concatenate([wk, jnp.zeros((kv_rank, nh, LANES - nope), F32)], axis=2)
    pad1 = lambda v, fill=0.0: jnp.concatenate([v, jnp.full((LANES - v.shape[0],), fill, F32)]).reshape(1, LANES)
    return dict(
        w_pad=w_pad, wqm=wqm, wqp=wqp,
        wk128=wk128.reshape(kv_rank, nh * LANES).astype(BF16),
        wkt=wk.reshape(kv_rank, nh * nope).T.astype(BF16),
        wv=w_ukv[:, :, nope:].reshape(kv_rank, nh * vd).astype(BF16),
        gq128=pad1(gq * (math.log2(math.e) * qk ** -0.5)),
        gk128=pad1(gk), dtb128=pad1(dt_bias), alog128=pad1(a_log),
        dsk=jnp.repeat(d_skip, inner // n_ssd).reshape(1, inner),
        nh=nh, qk=qk, nope=nope, rope=rope, vd=vd, q_rank=q_rank, kv_rank=kv_rank, inner=inner,
        conv_dim=conv_dim, n_ssd=n_ssd)


def _cd_inproj_kernel(x_ref, g_ref, w_ref, gql_ref, gkl_ref, wqm_ref, wqp_ref, wk_ref, wv_ref, gq_ref, gk_ref,
                      cos_ref, sin_ref, qh_ref, kh_ref, vh_ref, lat_ref, z_ref, xbc_ref, dt_ref,
                      *, nh, qk, nope, rope, vd, q_rank, kv_rank, inner, conv_dim):
    h = _rms(x_ref[...], g_ref[...]).astype(BF16)
    p = _dot(h, w_ref[...])
    o = 0
    cq = p[:, o:o + q_rank]; o += q_rank
    ckv = p[:, o:o + kv_rank]; o += kv_rank
    kpe = p[:, o:o + LANES]; o += LANES
    kpe_rot = p[:, o:o + LANES]; o += LANES
    z_ref[...] = p[:, o:o + inner]; o += inner
    xbc_ref[...] = p[:, o:o + conv_dim]; o += conv_dim
    dt_ref[...] = p[:, o:o + LANES]

    cos_t = cos_ref[...]
    sin_t = sin_ref[...]
    lane = lax.broadcasted_iota(jnp.int32, (1, LANES), 1)
    cos_q = cos_t + (lane < nope).astype(F32)
    cqn = _rms(cq, gql_ref[...]).astype(BF16)
    qm = _dot(cqn, wqm_ref[...])
    qp = _dot(cqn, wqp_ref[...])
    for i in range(nh):
        qi = qm[:, i * LANES:(i + 1) * LANES] * cos_q + qp[:, i * LANES:(i + 1) * LANES] * sin_t
        ms = jnp.sum(qi * qi, axis=-1, keepdims=True) * (1.0 / qk)
        qh_ref[i] = (qi * lax.rsqrt(ms + NORM_EPS) * gq_ref[...]).astype(BF16)

    ckvn = _rms(ckv, gkl_ref[...])
    kper = kpe * cos_t + kpe_rot * sin_t
    lat_ref[:, :kv_rank] = ckvn
    lat_ref[:, kv_rank:] = kper[:, nope:nope + rope]
    cb = ckvn.astype(BF16)
    kn = _dot(cb, wk_ref[...])
    vv = _dot(cb, wv_ref[...])
    for i in range(nh):
        ki = kn[:, i * LANES:(i + 1) * LANES] + kper
        ms = jnp.sum(ki * ki, axis=-1, keepdims=True) * (1.0 / qk)
        kh_ref[i] = (ki * lax.rsqrt(ms + NORM_EPS) * gk_ref[...]).astype(BF16)
        vh_ref[i] = vv[:, i * vd:(i + 1) * vd].astype(BF16)


def _cd_inproj(x, g, cw, gql, gkl, cos, sin, *, tm):
    m, d = x.shape
    nh, vd, kv_rank, rope = cw["nh"], cw["vd"], cw["kv_rank"], cw["rope"]
    inner, conv_dim, q_rank = cw["inner"], cw["conv_dim"], cw["q_rank"]
    n_pos = cos.shape[0] // tm
    row = lambda i: (i, 0)
    head = lambda i: (0, i, 0)
    dims = {k: cw[k] for k in ("nh", "qk", "nope", "rope", "vd", "q_rank", "kv_rank", "inner", "conv_dim")}
    return pl.pallas_call(
        functools.partial(_cd_inproj_kernel, **dims),
        grid=(m // tm,),
        in_specs=[pl.BlockSpec((tm, d), row), _full((1, d)), _full(cw["w_pad"].shape), _full((1, q_rank)),
                  _full((1, kv_rank)), _full(cw["wqm"].shape), _full(cw["wqp"].shape), _full(cw["wk128"].shape),
                  _full(cw["wv"].shape), _full((1, LANES)), _full((1, LANES)),
                  pl.BlockSpec((tm, LANES), lambda i: (i % n_pos, 0)),
                  pl.BlockSpec((tm, LANES), lambda i: (i % n_pos, 0))],
        out_specs=[pl.BlockSpec((nh, tm, LANES), head), pl.BlockSpec((nh, tm, LANES), head),
                   pl.BlockSpec((nh, tm, vd), head), pl.BlockSpec((tm, kv_rank + rope), row),
                   pl.BlockSpec((tm, inner), row), pl.BlockSpec((tm, conv_dim), row),
                   pl.BlockSpec((tm, LANES), row)],
        out_shape=[jax.ShapeDtypeStruct((nh, m, LANES), BF16), jax.ShapeDtypeStruct((nh, m, LANES), BF16),
                   jax.ShapeDtypeStruct((nh, m, vd), BF16), jax.ShapeDtypeStruct((m, kv_rank + rope), F32),
                   jax.ShapeDtypeStruct((m, inner), F32), jax.ShapeDtypeStruct((m, conv_dim), F32),
                   jax.ShapeDtypeStruct((m, LANES), F32)],
        compiler_params=_cparams("parallel"),
    )(x, g.reshape(1, d), cw["w_pad"], gql.reshape(1, q_rank), gkl.reshape(1, kv_rank), cw["wqm"], cw["wqp"],
      cw["wk128"], cw["wv"], cw["gq128"], cw["gk128"], cos, sin)


def _mla_prompt_kernel(q_ref, k_ref, v_ref, o_ref, s_buf, p_buf, a_scr, m_scr, l_scr, acc_scr, *, tq):
    i = pl.program_id(2)
    q = q_ref[...]
    mask = lax.broadcasted_iota(jnp.int32, (tq, tq), 1) <= lax.broadcasted_iota(jnp.int32, (tq, tq), 0)

    def block(t):
        return pl.ds(pl.multiple_of(jnp.where(t == 0, i, t - 1) * tq, tq), tq)

    halves = [slice(c * (tq // 2), (c + 1) * (tq // 2)) for c in range(2)]

    def scores(t, msk=None):
        k = k_ref[block(t), :]
        ss = [lax.dot_general(q[h], k, _NT, preferred_element_type=F32) for h in halves]
        for h, s in zip(halves, ss):
            s_buf[h, :] = s if msk is None else jnp.where(msk[h], s, -jnp.inf)

    def softmax():
        s = s_buf[...]
        m_new = jnp.maximum(m_scr[...], jnp.max(s, axis=-1, keepdims=True))
        alpha = jnp.exp2(m_scr[...] - m_new)
        p = jnp.exp2(s - m_new)
        l_scr[...] = alpha * l_scr[...] + jnp.sum(p, axis=-1, keepdims=True)
        m_scr[...] = m_new
        a_scr[...] = alpha
        p_buf[...] = p.astype(BF16)

    def values(t):
        v = v_ref[block(t), :]
        outs = [_dot(p_buf[h, :], v) for h in halves]
        for h, o in zip(halves, outs):
            acc_scr[h, :] = a_scr[h, :] * acc_scr[h, :] + o

    m_scr[...] = jnp.full(m_scr.shape, -jnp.inf, F32)
    l_scr[...] = jnp.zeros_like(l_scr)
    acc_scr[...] = jnp.zeros_like(acc_scr)
    scores(0, mask)
    softmax()

    @pl.when(i >= 1)
    def _():
        scores(1)

    def steady(t, carry):
        values(t - 2)
        softmax()
        scores(t)
        return carry

    lax.fori_loop(2, i + 1, steady, 0)

    @pl.when(i >= 1)
    def _():
        values(i - 1)
        softmax()

    values(i)
    o_ref[...] = (acc_scr[...] / l_scr[...]).astype(o_ref.dtype)


def _mla_prompt(qh, kh, vh, *, nb, tq):
    nh, m, dk = qh.shape
    vd = vh.shape[-1]
    t = m // nb
    n_blk = t // tq
    return pl.pallas_call(
        functools.partial(_mla_prompt_kernel, tq=tq),
        grid=(nb, nh, n_blk),
        in_specs=[pl.BlockSpec((None, tq, dk), lambda b, h, i: (h, b * n_blk + i, 0)),
                  pl.BlockSpec((None, t, dk), lambda b, h, i: (h, b, 0)),
                  pl.BlockSpec((None, t, vd), lambda b, h, i: (h, b, 0))],
        out_specs=pl.BlockSpec((None, tq, vd), lambda b, h, i: (h, b * n_blk + i, 0)),
        out_shape=jax.ShapeDtypeStruct((nh, m, vd), BF16),
        scratch_shapes=[pltpu.VMEM((tq, tq), F32), pltpu.VMEM((tq, tq), BF16), pltpu.VMEM((tq, 1), F32),
                        pltpu.VMEM((tq, 1), F32), pltpu.VMEM((tq, 1), F32), pltpu.VMEM((tq, vd), F32)],
        compiler_params=_cparams("parallel", "parallel", "arbitrary"),
        name="mla_prompt",
    )(qh, kh, vh)


def _causal_conv_tile(x_ref, xpad, cw_ref, cb_ref, cbuf_ref, rows):
    pad = 8
    xpad[pad:pad + rows, :] = x_ref[...]
    xc = cb_ref[...]
    for i in range(CONV_WIDTH):
        o = pad - (CONV_WIDTH - 1) + i
        xc = xc + xpad[o:o + rows, :] * cw_ref[i:i + 1, :]
    cbuf_ref[...] = xpad[pad + rows - (CONV_WIDTH - 1):pad + rows, :]
    xpad[0:pad, :] = xpad[rows:rows + pad, :]
    return xc


def _gated_group_norm(y, z, ng_ref, y_ref, n_groups):
    y = y * _silu(z)
    gs = y.shape[-1] // n_groups
    for g in range(n_groups):
        y_ref[:, g * gs:(g + 1) * gs] = _rms(y[:, g * gs:(g + 1) * gs], ng_ref[:, g * gs:(g + 1) * gs]).astype(y_ref.dtype)


def _ssd_prompt_kernel(xbc_ref, z_ref, dt_ref, cw_ref, cb_ref, dtb_ref, alog_ref, dsk_ref, ng_ref, ltri_ref,
                       y_ref, hlast_ref, cbuf_ref, xpad, state, yscr, *, q, nh, hd, ns, n_groups):
    c = pl.program_id(1)

    @pl.when(c == 0)
    def _():
        state[...] = jnp.zeros_like(state)
        xpad[0:8, :] = jnp.zeros((8, xpad.shape[1]), F32)

    xc = _silu(_causal_conv_tile(xbc_ref, xpad, cw_ref, cb_ref, cbuf_ref, q))
    inner = nh * hd
    hpg = nh // n_groups
    xs = xc[:, :inner]
    bm = xc[:, inner:inner + n_groups * ns]
    cm = xc[:, inner + n_groups * ns:]
    lane = lax.broadcasted_iota(jnp.int32, (1, LANES), 1)
    dtv = _softplus(dt_ref[...] + dtb_ref[...])
    a = jnp.where(lane < nh, -jnp.exp(alog_ref[...]), 0.0)
    ltri = ltri_ref[...]
    cum = sum(_dot(ltri, part) for part in _split3(dtv * a))
    cum_t = cum.T
    dt_t = dtv.T
    cum_last = cum[q - 1:q, :]
    to_end = jnp.exp(cum_last - cum) * dtv
    ecum = jnp.exp(cum)
    elast = jnp.exp(cum_last)
    causal = lax.broadcasted_iota(jnp.int32, (q, q), 0) >= lax.broadcasted_iota(jnp.int32, (q, q), 1)
    for g in range(n_groups):
        cmg = cm[:, g * ns:(g + 1) * ns].astype(BF16)
        bmg = bm[:, g * ns:(g + 1) * ns].astype(BF16)
        cb = lax.dot_general(cmg, bmg, _NT, preferred_element_type=F32)
        for r in range(g * hpg, (g + 1) * hpg):
            seg = cum[:, r:r + 1] - cum_t[r:r + 1, :]
            decay = jnp.exp(jnp.where(causal, seg, -jnp.inf))
            mix = cb * decay * dt_t[r:r + 1, :]
            xh = xs[:, r * hd:(r + 1) * hd]
            hprev = state[r]
            y = _dot(mix.astype(BF16), xh.astype(BF16))
            y = y + lax.dot_general(cmg, hprev.astype(BF16), _NT, preferred_element_type=F32) * ecum[:, r:r + 1]
            yscr[:, r * hd:(r + 1) * hd] = y + dsk_ref[:, r * hd:(r + 1) * hd] * xh
            xw = (xh * to_end[:, r:r + 1]).astype(BF16)
            state[r] = elast[:, r:r + 1] * hprev + lax.dot_general(xw, bmg, _TN, preferred_element_type=F32)
    hlast_ref[...] = state[...]
    _gated_group_norm(yscr[...], z_ref[...], ng_ref, y_ref, n_groups)


def _lower_tri_ones(n):
    r = lax.broadcasted_iota(jnp.int32, (n, n), 0)
    c = lax.broadcasted_iota(jnp.int32, (n, n), 1)
    return (r >= c).astype(BF16)


def _ssd_prompt(xbc, z, dt, conv_w, conv_b, cw, norm_g, *, nb, q, state_dim):
    m, conv_dim = xbc.shape
    inner, nh = cw["inner"], cw["n_ssd"]
    hd = inner // nh
    n_groups = (conv_dim - inner) // (2 * state_dim)
    t = m // nb
    nc = t // q
    row = lambda b, c: (b * nc + c, 0)
    return pl.pallas_call(
        functools.partial(_ssd_prompt_kernel, q=q, nh=nh, hd=hd, ns=state_dim, n_groups=n_groups),
        grid=(nb, nc),
        in_specs=[pl.BlockSpec((q, conv_dim), row), pl.BlockSpec((q, inner), row), pl.BlockSpec((q, LANES), row),
                  _full((CONV_WIDTH, conv_dim)), _full((1, conv_dim)), _full((1, LANES)), _full((1, LANES)),
                  _full((1, inner)), _full((1, inner)), _full((q, q))],
        out_specs=[pl.BlockSpec((q, inner), row),
                   pl.BlockSpec((None, nh, hd, state_dim), lambda b, c: (b, 0, 0, 0)),
                   pl.BlockSpec((None, CONV_WIDTH - 1, conv_dim), lambda b, c: (b, 0, 0))],
        out_shape=[jax.ShapeDtypeStruct((m, inner), BF16), jax.ShapeDtypeStruct((nb, nh, hd, state_dim), F32),
                   jax.ShapeDtypeStruct((nb, CONV_WIDTH - 1, conv_dim), F32)],
        scratch_shapes=[pltpu.VMEM((q + 8, conv_dim), F32), pltpu.VMEM((nh, hd, state_dim), F32),
                        pltpu.VMEM((q, inner), F32)],
        compiler_params=_cparams("parallel", "arbitrary"),
    )(xbc, z, dt, conv_w, conv_b.reshape(1, conv_dim), cw["dtb128"], cw["alog128"], cw["dsk"],
      norm_g.reshape(1, inner), _lower_tri_ones(q))


def _sb_sample_kernel(pt_ref, qbd_ref, uu_ref, *rest, npg, dk, grp):
    pages = rest[:npg]
    o_ref, z_scr, r_scr, acc_scr = rest[npg:]
    c = pl.program_id(1)
    nrow = qbd_ref.shape[0]
    n_kv = nrow // grp
    hd = dk // n_kv
    keys = z_scr.shape[1]

    @pl.when(c == 0)
    def _():
        r_scr[...] = jnp.zeros_like(r_scr)
        acc_scr[...] = jnp.zeros_like(acc_scr)

    qbd = qbd_ref[...]
    for j in range(npg):
        z_scr[j * nrow:(j + 1) * nrow, :] = _dot(qbd, pages[j][:dk, :].astype(BF16))
    nz = z_scr[...]
    hi, lo = _split2(_sb_log2_keep(nz))
    tl = _dot(jnp.concatenate([hi, lo], axis=1), uu_ref[...])
    tot = jnp.broadcast_to(tl[:, 0:1], tl.shape)
    r = r_scr[...]
    carries = [None] * npg
    for j in reversed(range(npg)):
        carries[j] = r
        r = r + tot[j * nrow:(j + 1) * nrow, :]
    r_scr[...] = r
    z_scr[...] = jnp.exp2(tl + jnp.concatenate(carries, axis=0) - nz)
    for j in range(npg):
        vt = pages[j][dk:, :]
        for g in range(grp):
            base = j * nrow + g * n_kv
            wexp = jnp.concatenate(
                [jnp.broadcast_to(z_scr[base + h:base + h + 1, :], (hd, keys)) for h in range(n_kv)], axis=0)
            acc_scr[g] += vt * wexp

    @pl.when(c == pl.num_programs(1) - 1)
    def _():
        for g in range(grp):
            o_ref[g:g + 1, :] = jnp.sum(acc_scr[g].T, axis=0, keepdims=True)


def _sb_sample(qh, cache_t, layer, page_table, *, n_kv, npg=32):
    n_q, s, hd = qh.shape
    grp = n_q // n_kv
    dk = n_kv * hd
    page = cache_t.shape[3]
    n_pages = page_table.shape[1]
    npg = min(npg, n_pages)
    nch = n_pages // npg
    eye = jnp.eye(n_kv, dtype=qh.dtype)
    qbd = jnp.einsum('hgsd,hk->sghkd', qh.reshape(n_kv, grp, s, hd), eye).reshape(s, n_q, dk)

    def page_spec(j):
        return pl.BlockSpec((None, None, 2 * dk, page),
                            lambda i, c, pt: (layer, pt[i, (nch - 1 - c) * npg + j], 0, 0))

    grid_spec = pltpu.PrefetchScalarGridSpec(
        num_scalar_prefetch=1, grid=(s, nch),
        in_specs=[pl.BlockSpec((None, n_q, dk), lambda i, c, pt: (i, 0, 0)),
                  pl.BlockSpec((2 * page, page), lambda i, c, pt: (0, 0))] + [page_spec(j) for j in range(npg)],
        out_specs=pl.BlockSpec((None, grp, dk), lambda i, c, pt: (i, 0, 0)),
        scratch_shapes=[pltpu.VMEM((npg * n_q, page), F32), pltpu.VMEM((n_q, page), F32),
                        pltpu.VMEM((grp, dk, page), F32)])
    og = pl.pallas_call(
        functools.partial(_sb_sample_kernel, npg=npg, dk=dk, grp=grp),
        grid_spec=grid_spec,
        out_shape=jax.ShapeDtypeStruct((s, grp, dk), F32),
        compiler_params=_cparams("parallel", "arbitrary"),
        name="sb_sample",
    )(page_table, qbd, _rev_cumsum_ones(page), *([cache_t] * npg))
    o = og.reshape(s, grp, n_kv, hd)
    return jnp.transpose(o, (2, 1, 0, 3)).reshape(n_q, s, hd).astype(BF16)


def _conv_step(x, buf_ref, nbuf_ref, cw_ref, cb_ref):
    xc = cb_ref[...]
    for i in range(CONV_WIDTH - 1):
        xc = xc + buf_ref[i] * cw_ref[i:i + 1, :]
        if i > 0:
            nbuf_ref[i - 1] = buf_ref[i]
    nbuf_ref[CONV_WIDTH - 2] = x
    return xc + x * cw_ref[CONV_WIDTH - 1:CONV_WIDTH, :]


def _rglru_step_kernel(xb_ref, gate_ref, buf_ref, h0_ref, cw_ref, cb_ref, wbd_ref, bias_ref, lam_ref,
                       y_ref, h_ref, nbuf_ref):
    xc = _conv_step(xb_ref[...], buf_ref, nbuf_ref, cw_ref, cb_ref)
    a, b = _lru_gates(xc, wbd_ref, bias_ref, lam_ref)
    h = a * h0_ref[...] + b
    h_ref[...] = h
    y_ref[...] = (_gelu_tanh(gate_ref[...]) * h).astype(y_ref.dtype)


def _rglru_step(xb, gate, buf, h0, conv_w, conv_b, wbd, bias, lam):
    s, w = xb.shape
    return pl.pallas_call(
        _rglru_step_kernel,
        out_shape=[jax.ShapeDtypeStruct((s, w), BF16), jax.ShapeDtypeStruct((s, w), F32),
                   jax.ShapeDtypeStruct((CONV_WIDTH - 1, s, w), F32)],
        compiler_params=pltpu.CompilerParams(vmem_limit_bytes=VMEM_LIMIT_BYTES),
    )(xb, gate, buf, h0, conv_w, conv_b.reshape(1, w), wbd, bias, lam.reshape(1, w))


def _mla_sample_kernel(pt_ref, qn_ref, qr_ref, new_ref, wkt_ref, wv_ref, *rest,
                       n_pages, page, kv_rank, rope, qk, nh, ppi):
    pages = rest[:n_pages]
    o_ref, ckv_scr, kpe_scr, s_scr, qexp_scr = rest[n_pages:]
    n_iter = n_pages // ppi
    width = ppi * page
    nope = wkt_ref.shape[0] // nh
    for j in range(n_pages):
        lanes = slice((j % ppi) * page, (j % ppi + 1) * page)
        ckv_scr[j // ppi, :, lanes] = pages[j][:kv_rank, :].astype(BF16)
        kpe_scr[j // ppi, :, lanes] = pages[j][kv_rank:, :]
    qexp = jnp.broadcast_to(qn_ref[...], (LANES, nh * nope)).T
    qexp_scr[...] = jnp.concatenate([qexp] * (width // LANES), axis=1)
    qr = qr_ref[...]
    wkt = wkt_ref[...]

    def scores(ct, kp, w):
        kn = _dot(wkt, ct).reshape(nh, nope, w)
        ssq = jnp.sum(kn * kn, axis=1) + jnp.sum(kp * kp, axis=0, keepdims=True)
        s = jnp.sum(kn * qexp_scr[:, :w].reshape(nh, nope, w), axis=1) + _dot(qr, kp.astype(BF16))
        return s * lax.rsqrt(ssq * (1.0 / qk) + NORM_EPS)

    new_t = jnp.broadcast_to(jnp.concatenate([new_ref[...], jnp.zeros((1, 2 * LANES - kv_rank - rope), F32)], axis=1),
                             (LANES, 2 * LANES)).T
    ct_new = new_t[:kv_rank, :].astype(BF16)
    first = lax.broadcasted_iota(jnp.int32, (nh, LANES), 1) == 0
    s_new = jnp.where(first, scores(ct_new, new_t[kv_rank:kv_rank + rope, :], LANES), -jnp.inf)

    def score_pass(i, m):
        s = scores(ckv_scr[i], kpe_scr[i], width)
        s_scr[i] = s
        return jnp.maximum(m, jnp.max(s, axis=1, keepdims=True))

    unroll = 2 if n_iter % 2 == 0 else 1
    m = lax.fori_loop(0, n_iter, score_pass, jnp.max(s_new, axis=1, keepdims=True), unroll=unroll)
    p_new = jnp.exp2(s_new - m)

    def value_pass(i, c):
        l, acc_t = c
        p = jnp.exp2(s_scr[i] - m)
        acc_t = acc_t + lax.dot_general(ckv_scr[i], p.astype(BF16), _NT, preferred_element_type=F32)
        return l + jnp.sum(p, axis=1, keepdims=True), acc_t

    l0 = jnp.sum(p_new, axis=1, keepdims=True)
    acc0 = lax.dot_general(ct_new, p_new.astype(BF16), _NT, preferred_element_type=F32)
    l, acc_t = lax.fori_loop(0, n_iter, value_pass, (l0, acc0), unroll=unroll)
    acc = jnp.concatenate([acc_t, jnp.zeros((kv_rank, LANES - nh), F32)], axis=1).T[:nh, :]
    hi, lo = _split2(acc / l)
    o_ref[...] = _dot(hi, wv_ref[...]) + _dot(lo, wv_ref[...])


def _mla_sample(qh, lat_new, cache, layer, page_table, cw, gk):
    nh, s, _ = qh.shape
    nope, rope, vd, kv_rank, qk = cw["nope"], cw["rope"], cw["vd"], cw["kv_rank"], cw["qk"]
    page = cache.shape[3]
    n_pages = page_table.shape[1]
    ppi = max(c for c in (1, 2, 4) if n_pages % c == 0)
    qg = jnp.swapaxes(qh.astype(F32) * cw["gk128"].reshape(1, 1, LANES), 0, 1)
    qn = qg[:, :, :nope].reshape(s, 1, nh * nope)
    qr = qg[:, :, nope:qk].astype(BF16)
    new = lat_new.reshape(s, 1, kv_rank + rope)

    def page_spec(j):
        return pl.BlockSpec((None, None, kv_rank + rope, page), lambda i, pt: (layer, pt[i, j], 0, 0))

    grid_spec = pltpu.PrefetchScalarGridSpec(
        num_scalar_prefetch=1, grid=(s,),
        in_specs=[pl.BlockSpec((None, 1, nh * nope), lambda i, pt: (i, 0, 0)),
                  pl.BlockSpec((None, nh, rope), lambda i, pt: (i, 0, 0)),
                  pl.BlockSpec((None, 1, kv_rank + rope), lambda i, pt: (i, 0, 0)),
                  pl.BlockSpec(cw["wkt"].shape, lambda i, pt: (0, 0)),
                  pl.BlockSpec(cw["wv"].shape, lambda i, pt: (0, 0))] + [page_spec(j) for j in range(n_pages)],
        out_specs=pl.BlockSpec((None, nh, nh * vd), lambda i, pt: (i, 0, 0)),
        scratch_shapes=[pltpu.VMEM((n_pages // ppi, kv_rank, ppi * page), BF16),
                        pltpu.VMEM((n_pages // ppi, rope, ppi * page), F32),
                        pltpu.VMEM((n_pages // ppi, nh, ppi * page), F32),
                        pltpu.VMEM((nh * nope, ppi * page), F32)])
    om = pl.pallas_call(
        functools.partial(_mla_sample_kernel, n_pages=n_pages, page=page, kv_rank=kv_rank, rope=rope, qk=qk, nh=nh,
                          ppi=ppi),
        grid_spec=grid_spec,
        out_shape=jax.ShapeDtypeStruct((s, nh, nh * vd), F32),
        compiler_params=_cparams("parallel"),
        name="mla_sample",
    )(page_table, qn, qr, new, cw["wkt"], cw["wv"], *([cache] * n_pages))
    idx = jnp.arange(nh)
    o = om.reshape(s, nh, nh, vd)[:, idx, idx, :]
    return jnp.swapaxes(o, 0, 1).astype(BF16)


def _ssd_step_kernel(xbc_ref, z_ref, dt_ref, buf_ref, h0_ref, cw_ref, cb_ref, dtb_ref, alog_ref, dsk_ref, ng_ref,
                     y_ref, hnew_ref, nbuf_ref, xc_scr, xct_scr, dtt_scr, dat_scr, yoff_scr,
                     *, nh, hd, ns, n_groups):
    r = pl.program_id(0)
    inner = nh * hd
    hpg = nh // n_groups
    lane = lax.broadcasted_iota(jnp.int32, (1, LANES), 1)

    @pl.when(r == 0)
    def _():
        xc = _silu(_conv_step(xbc_ref[...], buf_ref, nbuf_ref, cw_ref, cb_ref))
        xc_scr[...] = xc
        xct_scr[...] = xc.T
        dtv = _softplus(dt_ref[...] + dtb_ref[...])
        a = jnp.where(lane < nh, -jnp.exp(alog_ref[...]), 0.0)
        dtt_scr[...] = dtv.T
        dat_scr[...] = jnp.exp(dtv * a).T

    g = r // hpg
    xt = xct_scr[pl.ds(pl.multiple_of(r * hd, hd), hd), :]
    bt = xct_scr[pl.ds(pl.multiple_of(inner + g * ns, ns), ns), :]
    ct = xct_scr[pl.ds(pl.multiple_of(inner + (n_groups + g) * ns, ns), ns), :]
    dar = dat_scr[pl.ds(r, 1), :]
    coef = xt * dtt_scr[pl.ds(r, 1), :]
    for p in range(hd):
        h0p = h0_ref[p * ns:(p + 1) * ns, :]
        hnew_ref[p * ns:(p + 1) * ns, :] = dar * h0p + coef[p:p + 1, :] * bt
        yoff_scr[pl.ds(r * hd + p, 1), :] = jnp.sum(ct * h0p, axis=0, keepdims=True)

    @pl.when(r == nh - 1)
    def _():
        xc = xc_scr[...]
        xs = xc[:, :inner]
        dtv = dtt_scr[...].T
        da = dat_scr[...].T
        per_head = lambda v: jnp.concatenate(
            [jnp.broadcast_to(v[:, i:i + 1], (v.shape[0], hd)) for i in range(nh)], axis=1)
        cb = []
        for gi in range(n_groups):
            bmg = xc[:, inner + gi * ns:inner + (gi + 1) * ns]
            cmg = xc[:, inner + (n_groups + gi) * ns:inner + (n_groups + gi + 1) * ns]
            cb.append(jnp.broadcast_to(jnp.sum(cmg * bmg, axis=-1, keepdims=True), (xc.shape[0], hpg * hd)))
        y = jnp.concatenate(cb, axis=1) * per_head(dtv) * xs + yoff_scr[...].T * per_head(da) + dsk_ref[...] * xs
        _gated_group_norm(y, z_ref[...], ng_ref, y_ref, n_groups)


def _ssd_step(xbc, z, dt, buf, h0, conv_w, conv_b, cw, norm_g):
    s, conv_dim = xbc.shape
    _, nh, hd, ns = h0.shape
    inner = cw["inner"]
    n_groups = (conv_dim - inner) // (2 * ns)
    sz = hd * ns
    h0t = jnp.transpose(h0, (1, 2, 3, 0)).reshape(nh * sz, s)
    y, hnew, nbuf = pl.pallas_call(
        functools.partial(_ssd_step_kernel, nh=nh, hd=hd, ns=ns, n_groups=n_groups),
        grid=(nh,),
        in_specs=[_full((s, conv_dim)), _full((s, inner)), _full((s, LANES)), _full((CONV_WIDTH - 1, s, conv_dim)),
                  pl.BlockSpec((sz, s), lambda r: (r, 0)), _full((CONV_WIDTH, conv_dim)), _full((1, conv_dim)),
                  _full((1, LANES)), _full((1, LANES)), _full((1, inner)), _full((1, inner))],
        out_specs=[_full((s, inner)), pl.BlockSpec((sz, s), lambda r: (r, 0)),
                   _full((CONV_WIDTH - 1, s, conv_dim))],
        out_shape=[jax.ShapeDtypeStruct((s, inner), BF16), jax.ShapeDtypeStruct((nh * sz, s), F32),
                   jax.ShapeDtypeStruct((CONV_WIDTH - 1, s, conv_dim), F32)],
        scratch_shapes=[pltpu.VMEM((s, conv_dim), F32), pltpu.VMEM((conv_dim, s), F32), pltpu.VMEM((LANES, s), F32),
                        pltpu.VMEM((LANES, s), F32), pltpu.VMEM((inner, s), F32)],
        compiler_params=_cparams("arbitrary"),
        name="ssd_step",
    )(xbc, z, dt, buf, h0t, conv_w, conv_b.reshape(1, conv_dim), cw["dtb128"], cw["alog128"],
      cw["dsk"], norm_g.reshape(1, inner))
    return y, jnp.transpose(hnew.reshape(nh, hd, ns, s), (3, 0, 1, 2)), nbuf


def _prompt_tiles(t, hidden):
    pick = lambda want: max(c for c in (8, 16, 32, 64, 128, 256, 512, 1024) if c <= want and t % c == 0)
    th = hidden // 2 if hidden % (2 * LANES) == 0 else hidden
    return dict(tm_proj=pick(512), tm_post=pick(512), tq_sb=pick(256), tq_mla=pick(512), tt_lru=pick(512),
                ssd_chunk=pick(128), th=th)


def kernel(x_prompt, x_sample, cache_sb_kv, cache_mla_kv, page_table, state_lru_h, state_lru_conv, state_ssm_h, state_ssm_conv, norm_mix, norm_ffn, ab_w_in, ab_w_out, sb_q_gain, sb_k_gain, lru_conv_w, lru_conv_b, lru_wa, lru_ba, lru_wx, lru_bx, lru_lambda, cd_w_in, cd_w_out, mla_q_lat_gain, mla_w_uq, mla_kv_lat_gain, mla_w_ukv, mla_q_gain, mla_k_gain, ssd_conv_w, ssd_conv_b, ssd_dt_bias, ssd_a_log, ssd_d, ssd_norm_gain, ffn_w_in, ffn_w_out):
    nb, t, d = x_prompt.shape
    ns = x_sample.shape[0]
    depth = norm_mix.shape[0]
    page = cache_sb_kv.shape[2]
    past = page_table.shape[1] * page
    n_kv, hd = cache_sb_kv.shape[4], cache_sb_kv.shape[5]
    n_q = (ab_w_in.shape[2] - 2 * n_kv * hd - 2 * lru_lambda.shape[1]) // hd
    w_lru = lru_lambda.shape[1]
    sizes = _prompt_tiles(t, ffn_w_out.shape[1])

    xp = x_prompt.reshape(nb * t, d)
    xs = x_sample.reshape(ns, d)
    cache_sb = jnp.transpose(cache_sb_kv, (0, 1, 3, 4, 5, 2)).reshape(cache_sb_kv.shape[:2] + (2 * n_kv * hd, page))
    cache_mla = jnp.transpose(cache_mla_kv, (0, 1, 3, 2))
    cos_p, sin_p = _rope_tables(jnp.arange(t), mla_q_gain.shape[1] - (cache_mla_kv.shape[3] - mla_kv_lat_gain.shape[1]),
                                cache_mla_kv.shape[3] - mla_kv_lat_gain.shape[1])
    cos_s, sin_s = _rope_tables(jnp.full((ns,), past), mla_q_gain.shape[1] - (cache_mla_kv.shape[3] - mla_kv_lat_gain.shape[1]),
                                cache_mla_kv.shape[3] - mla_kv_lat_gain.shape[1])
    outs = {k: [] for k in ("sb_p", "sb_s", "lh_p", "lh_s", "lc_p", "lc_s", "ml_p", "ml_s", "sh_p", "sh_s", "sc_p", "sc_s")}
    for li in range(depth):
        w_ffn_in = ffn_w_in[li].astype(BF16)
        w_ffn_out = ffn_w_out[li].astype(BF16)
        if li % 2 == 0:
            e = li // 2
            w_in = ab_w_in[e].astype(BF16)
            w_out = ab_w_out[e].astype(BF16)
            wbd, bias = _lru_weights(lru_wa[e], lru_ba[e], lru_wx[e], lru_bx[e])
            qh, kh, vh, kv, xb, gate = _ab_inproj(xp, norm_mix[li], w_in, sb_q_gain[e], sb_k_gain[e],
                                                  nq=n_q, nk=n_kv, hd=hd, w_lru=w_lru, tm=sizes["tm_proj"])
            oh = _sb_prompt(qh, kh, vh, nb=nb, tq=sizes["tq_sb"])
            y2, hl, cbuf = _rglru_prompt(xb, gate, lru_conv_w[e], lru_conv_b[e], wbd, bias, lru_lambda[e],
                                         nb=nb, tt=sizes["tt_lru"])
            xp = _post(xp, oh, y2, w_out, norm_ffn[li], w_ffn_in, w_ffn_out, tm=sizes["tm_post"], th=sizes["th"])
            outs["sb_p"].append(kv.reshape(nb, t, 2, n_kv, hd))
            outs["lh_p"].append(hl.reshape(nb, w_lru))
            outs["lc_p"].append(cbuf)
            qh, kh, vh, kv, xb, gate = _ab_inproj(xs, norm_mix[li], w_in, sb_q_gain[e], sb_k_gain[e],
                                                  nq=n_q, nk=n_kv, hd=hd, w_lru=w_lru, tm=ns)
            oh = _sb_sample(qh, cache_sb, e, page_table, n_kv=n_kv)
            y2, hl, cbuf = _rglru_step(xb, gate, jnp.swapaxes(state_lru_conv[e], 0, 1), state_lru_h[e],
                                       lru_conv_w[e], lru_conv_b[e], wbd, bias, lru_lambda[e])
            xs = _post(xs, oh, y2, w_out, norm_ffn[li], w_ffn_in, w_ffn_out, tm=ns, th=sizes["th"])
            outs["sb_s"].append(kv.reshape(ns, 1, 2, n_kv, hd))
            outs["lh_s"].append(hl)
            outs["lc_s"].append(jnp.swapaxes(cbuf, 0, 1))
        else:
            o = li // 2
            cw = _cd_weights(cd_w_in[o], mla_w_uq[o], mla_w_ukv[o], mla_q_gain[o], mla_k_gain[o], ssd_dt_bias[o],
                             ssd_a_log[o], ssd_d[o], kv_rank=mla_kv_lat_gain.shape[1], q_rank=mla_q_lat_gain.shape[1],
                             rope=cache_mla_kv.shape[3] - mla_kv_lat_gain.shape[1], inner=ssd_norm_gain.shape[1],
                             conv_dim=ssd_conv_w.shape[2])
            w_out = cd_w_out[o].astype(BF16)
            qh, kh, vh, lat, z, xbc, dt = _cd_inproj(xp, norm_mix[li], cw, mla_q_lat_gain[o], mla_kv_lat_gain[o],
                                                     cos_p, sin_p, tm=sizes["tm_proj"])
            oh = _mla_prompt(qh, kh, vh, nb=nb, tq=sizes["tq_mla"])
            y2, hl, cbuf = _ssd_prompt(xbc, z, dt, ssd_conv_w[o], ssd_conv_b[o], cw, ssd_norm_gain[o],
                                       nb=nb, q=sizes["ssd_chunk"], state_dim=state_ssm_h.shape[4])
            xp = _post(xp, oh, y2, w_out, norm_ffn[li], w_ffn_in, w_ffn_out, tm=sizes["tm_post"], th=sizes["th"])
            outs["ml_p"].append(lat.reshape(nb, t, lat.shape[1]))
            outs["sh_p"].append(hl)
            outs["sc_p"].append(cbuf)
            qh, kh, vh, lat, z, xbc, dt = _cd_inproj(xs, norm_mix[li], cw, mla_q_lat_gain[o], mla_kv_lat_gain[o],
                                                     cos_s, sin_s, tm=ns)
            oh = _mla_sample(qh, lat, cache_mla, o, page_table, cw, mla_k_gain[o])
            y2, hl, cbuf = _ssd_step(xbc, z, dt, jnp.swapaxes(state_ssm_conv[o], 0, 1), state_ssm_h[o],
                                     ssd_conv_w[o], ssd_conv_b[o], cw, ssd_norm_gain[o])
            xs = _post(xs, oh, y2, w_out, norm_ffn[li], w_ffn_in, w_ffn_out, tm=ns, th=sizes["th"])
            outs["ml_s"].append(lat.reshape(ns, 1, lat.shape[1]))
            outs["sh_s"].append(hl)
            outs["sc_s"].append(jnp.swapaxes(cbuf, 0, 1))
    st = {k: jnp.stack(v) for k, v in outs.items()}
    return (xp.reshape(nb, t, d), xs.reshape(ns, 1, d), st["sb_p"], st["sb_s"], st["lh_p"], st["lh_s"], st["lc_p"],
            st["lc_s"], st["ml_p"], st["ml_s"], st["sh_p"], st["sh_s"], st["sc_p"], st["sc_s"])
```

```python
import functools
import math

import jax
import jax.numpy as jnp
from jax import lax
from jax.experimental import pallas as pl
from jax.experimental.pallas import tpu as pltpu

F32 = jnp.float32
BF16 = jnp.bfloat16
NORM_EPS = 1e-6
LRU_C = 8.0
ROPE_THETA = 10000.0
CONV_WIDTH = 4
LANES = 128
VMEM_LIMIT_BYTES = 56 * 1024 * 1024
_NT = (((1,), (1,)), ((), ()))
_TN = (((0,), (0,)), ((), ()))


def _cparams(*sem):
    return pltpu.CompilerParams(dimension_semantics=sem, vmem_limit_bytes=VMEM_LIMIT_BYTES)


def _dot(a, b):
    return jnp.dot(a, b, preferred_element_type=F32)


def _split2(x):
    hi = lax.bitcast_convert_type(lax.bitcast_convert_type(x, jnp.uint32) & jnp.uint32(0xFFFF0000), F32)
    return hi.astype(BF16), (x - hi).astype(BF16)


def _split3(x):
    hi = x.astype(BF16)
    r = x - hi.astype(F32)
    mid = r.astype(BF16)
    lo = (r - mid.astype(F32)).astype(BF16)
    return hi, mid, lo


def _rms(x, g):
    return x * lax.rsqrt(jnp.mean(x * x, axis=-1, keepdims=True) + NORM_EPS) * g


def _sigmoid(x):
    return 1.0 / (1.0 + jnp.exp(-x))


def _silu(x):
    return x * _sigmoid(x)


def _softplus(x):
    return jnp.maximum(x, 0.0) + jnp.log(1.0 + jnp.exp(-jnp.abs(x)))


def _gelu_tanh(x):
    c = math.sqrt(2.0 / math.pi)
    return x * (0.5 * (1.0 + jnp.tanh(c * (x + 0.044715 * (x * x * x)))))


def _block_ones(n, blk, dtype=BF16):
    r = lax.broadcasted_iota(jnp.int32, (n, n), 0) // blk
    c = lax.broadcasted_iota(jnp.int32, (n, n), 1) // blk
    return (r == c).astype(dtype)


def _full(shape):
    nd = len(shape)
    return pl.BlockSpec(shape, lambda *_: (0,) * nd)


def _ab_inproj_kernel(x_ref, g_ref, w_ref, gq_ref, gk_ref, pq_ref, pk_ref,
                      qh_ref, kh_ref, vh_ref, kv_ref, xb_ref, gate_ref, *, nq, nk, hd):
    h = _rms(x_ref[...], g_ref[...]).astype(BF16)
    p = _dot(h, w_ref[...])
    dq, dk = nq * hd, nk * hd
    q = p[:, :dq]
    k = p[:, dq:dq + dk]
    v = p[:, dq + dk:dq + 2 * dk]

    def head_norm(t, ones_ref, gain):
        hi, lo = _split2(t * t)
        ms = (_dot(hi, ones_ref[...]) + _dot(lo, ones_ref[...])) * (1.0 / hd)
        return t * lax.rsqrt(ms + NORM_EPS) * gain

    qn = head_norm(q, pq_ref, gq_ref[...])
    kn = head_norm(k, pk_ref, gk_ref[...])
    for i in range(nq):
        qh_ref[i] = qn[:, i * hd:(i + 1) * hd].astype(BF16)
    for i in range(nk):
        kh_ref[i] = kn[:, i * hd:(i + 1) * hd].astype(BF16)
        vh_ref[i] = v[:, i * hd:(i + 1) * hd].astype(BF16)
    kv_ref[:, :dk] = kn
    kv_ref[:, dk:] = v
    w_lru = xb_ref.shape[-1]
    xb_ref[...] = p[:, dq + 2 * dk:dq + 2 * dk + w_lru]
    gate_ref[...] = p[:, dq + 2 * dk + w_lru:]


def _ab_inproj(x, g, w_bf, gq, gk, *, nq, nk, hd, w_lru, tm):
    m, d = x.shape
    n = w_bf.shape[1]
    dq, dk = nq * hd, nk * hd
    scale = -math.log2(math.e) * hd ** -0.5
    gq_t = (jnp.tile(gq, nq) * scale).reshape(1, dq)
    gk_t = jnp.tile(gk, nk).reshape(1, dk)
    row = lambda i: (i, 0)
    head = lambda i: (0, i, 0)
    return pl.pallas_call(
        functools.partial(_ab_inproj_kernel, nq=nq, nk=nk, hd=hd),
        grid=(m // tm,),
        in_specs=[pl.BlockSpec((tm, d), row), _full((1, d)), _full((d, n)), _full((1, dq)), _full((1, dk)),
                  _full((dq, dq)), _full((dk, dk))],
        out_specs=[pl.BlockSpec((nq, tm, hd), head), pl.BlockSpec((nk, tm, hd), head),
                   pl.BlockSpec((nk, tm, hd), head), pl.BlockSpec((tm, 2 * dk), row),
                   pl.BlockSpec((tm, w_lru), row), pl.BlockSpec((tm, w_lru), row)],
        out_shape=[jax.ShapeDtypeStruct((nq, m, hd), BF16), jax.ShapeDtypeStruct((nk, m, hd), BF16),
                   jax.ShapeDtypeStruct((nk, m, hd), BF16), jax.ShapeDtypeStruct((m, 2 * dk), F32),
                   jax.ShapeDtypeStruct((m, w_lru), F32), jax.ShapeDtypeStruct((m, w_lru), F32)],
        compiler_params=_cparams("parallel"),
    )(x, g.reshape(1, d), w_bf, gq_t, gk_t, _block_ones(dq, hd), _block_ones(dk, hd))


def _neg_abs(x):
    bits = lax.bitcast_convert_type(x, jnp.uint32) | jnp.uint32(0x80000000)
    return lax.bitcast_convert_type(bits, F32)


def _sb_log2_keep(nz):
    return jnp.minimum(nz, 0.0) - jnp.log2(1.0 + jnp.exp2(_neg_abs(nz)))


_SB_DIAG, _SB_FULL, _SB_IDLE = 0, 1, 2
_SB_DEPTH = 4


def _sb_prompt_kernel(qi_ref, kb_ref, kind_ref, q_ref, k_ref, v_ref, uu_ref, keep_ref, o_ref,
                      nz_ring, x_ring, tail_ring, w_ring, r_scr, acc_scr, *, tq, grp, hd, n_steps):
    rows = grp * tq

    def blk(idx):
        return pl.ds(pl.multiple_of(idx * tq, tq), tq)

    for ring in (nz_ring, x_ring, tail_ring, w_ring, r_scr, acc_scr):
        ring[...] = jnp.zeros_like(ring)

    def step(t, ph):
        kind, qi = kind_ref[t], qi_ref[t]
        v = v_ref[blk(kb_ref[t]), :]
        carry_on = jnp.where(kind == _SB_DIAG, 0.0, 1.0)
        outs = [_dot(w_ring[ph % 2, g * tq:(g + 1) * tq, :], v) for g in range(grp)]
        for g in range(grp):
            acc = acc_scr[g * tq:(g + 1) * tq, :] * carry_on + outs[g]
            acc_scr[g * tq:(g + 1) * tq, :] = acc
            o_ref[g, blk(qi), :] = acc.astype(o_ref.dtype)
        kind = kind_ref[t + 1]
        tail = tail_ring[(ph + 1) % 2] + r_scr[...] * jnp.where(kind == _SB_DIAG, 0.0, 1.0)
        w = jnp.where(keep_ref[kind] != 0.0, jnp.exp2(tail - nz_ring[(ph + 1) % 4]), 0.0)
        w_ring[(ph + 1) % 2] = w.astype(BF16)
        r_scr[...] = tail[:, 0:1]
        uu = uu_ref[...]
        tails = [_dot(x_ring[ph % 2, g * tq:(g + 1) * tq, :], uu) for g in range(grp)]
        for g in range(grp):
            tail_ring[ph % 2, g * tq:(g + 1) * tq, :] = tails[g]
        lk = _sb_log2_keep(nz_ring[(ph + 3) % 4]) * keep_ref[kind_ref[t + 3]]
        hi, lo = _split2(lk)
        x_ring[(ph + 1) % 2, :, :tq] = hi
        x_ring[(ph + 1) % 2, :, tq:] = lo
        k = k_ref[blk(kb_ref[t + 4]), :]
        qrows = blk(qi_ref[t + 4])
        nzs = [lax.dot_general(q_ref[g, qrows, :], k, _NT, preferred_element_type=F32) for g in range(grp)]
        for g in range(grp):
            nz_ring[ph % 4, g * tq:(g + 1) * tq, :] = nzs[g]

    def four_steps(j, carry):
        for ph in range(4):
            step(4 * j + ph, ph)
        return carry

    lax.fori_loop(0, n_steps // 4, four_steps, 0)


def _rev_cumsum_ones(n):
    s = lax.broadcasted_iota(jnp.int32, (n, n), 0)
    j = lax.broadcasted_iota(jnp.int32, (n, n), 1)
    u = (s >= j).astype(BF16)
    return jnp.concatenate([u, u], axis=0)


def _sb_prompt(qh, kh, vh, *, nb, tq):
    nq, m, hd = qh.shape
    nk = kh.shape[0]
    grp = nq // nk
    t = m // nb
    n_blk = t // tq
    rows = grp * tq
    items = [(i, i - s, _SB_DIAG if s == 0 else _SB_FULL) for i in range(n_blk) for s in range(i + 1)]
    n_steps = -(-(len(items) + _SB_DEPTH) // 4) * 4
    items = [(0, 0, _SB_IDLE)] * _SB_DEPTH + items
    items = items + [(n_blk - 1, 0, _SB_IDLE)] * (n_steps + _SB_DEPTH - len(items))
    qi_tab, kb_tab, kind_tab = (jnp.asarray([it[c] for it in items], jnp.int32) for c in range(3))
    t_idx = lax.broadcasted_iota(jnp.int32, (grp, tq, tq), 1).reshape(rows, tq)
    diag = (lax.broadcasted_iota(jnp.int32, (rows, tq), 1) < t_idx).astype(F32)
    keep = jnp.stack([diag, jnp.ones((rows, tq), F32), jnp.zeros((rows, tq), F32)])
    per_seq = lambda b, h, *_: (h, b, 0)
    const = lambda nd: (lambda b, h, *_: (0,) * nd)
    grid_spec = pltpu.PrefetchScalarGridSpec(
        num_scalar_prefetch=3, grid=(nb, nk),
        in_specs=[pl.BlockSpec((grp, t, hd), per_seq), pl.BlockSpec((None, t, hd), per_seq),
                  pl.BlockSpec((None, t, hd), per_seq), pl.BlockSpec((2 * tq, tq), const(2)),
                  pl.BlockSpec((3, rows, tq), const(3))],
        out_specs=pl.BlockSpec((grp, t, hd), per_seq),
        scratch_shapes=[pltpu.VMEM((4, rows, tq), F32), pltpu.VMEM((2, rows, 2 * tq), BF16),
                        pltpu.VMEM((2, rows, tq), F32), pltpu.VMEM((2, rows, tq), BF16),
                        pltpu.VMEM((rows, 1), F32), pltpu.VMEM((rows, hd), F32)])
    return pl.pallas_call(
        functools.partial(_sb_prompt_kernel, tq=tq, grp=grp, hd=hd, n_steps=n_steps),
        grid_spec=grid_spec,
        out_shape=jax.ShapeDtypeStruct((nq, m, hd), BF16),
        compiler_params=_cparams("parallel", "parallel"),
        name="sb_prompt",
    )(qi_tab, kb_tab, kind_tab, qh, kh, vh, _rev_cumsum_ones(tq), keep)


def _lru_gates(xc, wbd_ref, bias_ref, lam_ref):
    w = xc.shape[-1]
    ra = _dot(xc.astype(BF16), wbd_ref[...]) + bias_ref[...]
    r = _sigmoid(ra[:, :w])
    ig = _sigmoid(ra[:, w:])
    log_a = (-LRU_C) * r * _softplus(-lam_ref[...])
    a = jnp.exp(log_a)
    b = jnp.sqrt(-jnp.tanh(log_a) * (a * a + 1.0)) * (ig * xc)
    return a, b


def _rglru_prompt_kernel(xb_ref, gate_ref, cw_ref, cb_ref, wbd_ref, bias_ref, lam_ref,
                         y_ref, hlast_ref, cbuf_ref, xpad, a_scr, b_scr, h_scr, *, tt):
    t = pl.program_id(1)
    pad = 8

    @pl.when(t == 0)
    def _():
        h_scr[...] = jnp.zeros_like(h_scr)
        xpad[0:pad, :] = jnp.zeros((pad, xpad.shape[1]), F32)

    xpad[pad:pad + tt, :] = xb_ref[...]
    xc = cb_ref[...]
    for i in range(CONV_WIDTH):
        o = pad - (CONV_WIDTH - 1) + i
        xc = xc + xpad[o:o + tt, :] * cw_ref[i:i + 1, :]
    cbuf_ref[...] = xpad[pad + tt - (CONV_WIDTH - 1):pad + tt, :]
    xpad[0:pad, :] = xpad[tt:tt + pad, :]
    a, b = _lru_gates(xc, wbd_ref, bias_ref, lam_ref)
    a_scr[...] = a
    b_scr[...] = b

    def body(r, h):
        h = a_scr[pl.ds(r, 1), :] * h + b_scr[pl.ds(r, 1), :]
        b_scr[pl.ds(r, 1), :] = h
        return h

    h = lax.fori_loop(0, tt, body, h_scr[...], unroll=8)
    h_scr[...] = h
    hlast_ref[...] = h
    y_ref[...] = (_gelu_tanh(gate_ref[...]) * b_scr[...]).astype(y_ref.dtype)


def _lru_weights(wa, ba, wx, bx):
    nblk, bd, _ = wa.shape
    w = nblk * bd
    eye = jnp.eye(nblk, dtype=F32)

    def bdiag(m):
        return jnp.einsum('kij,kl->kilj', m, eye).reshape(w, w)

    wbd = jnp.concatenate([bdiag(wa), bdiag(wx)], axis=1).astype(BF16)
    bias = jnp.concatenate([ba, bx]).reshape(1, 2 * w)
    return wbd, bias


def _rglru_prompt(xb, gate, conv_w, conv_b, wbd, bias, lam, *, nb, tt):
    m, w = xb.shape
    t = m // nb
    nt = t // tt
    row = lambda b, i: (b * nt + i, 0)
    return pl.pallas_call(
        functools.partial(_rglru_prompt_kernel, tt=tt),
        grid=(nb, nt),
        in_specs=[pl.BlockSpec((tt, w), row), pl.BlockSpec((tt, w), row), _full((CONV_WIDTH, w)),
                  _full((1, w)), _full((w, 2 * w)), _full((1, 2 * w)), _full((1, w))],
        out_specs=[pl.BlockSpec((tt, w), row), pl.BlockSpec((None, 1, w), lambda b, i: (b, 0, 0)),
                   pl.BlockSpec((None, CONV_WIDTH - 1, w), lambda b, i: (b, 0, 0))],
        out_shape=[jax.ShapeDtypeStruct((m, w), BF16), jax.ShapeDtypeStruct((nb, 1, w), F32),
                   jax.ShapeDtypeStruct((nb, CONV_WIDTH - 1, w), F32)],
        scratch_shapes=[pltpu.VMEM((tt + 8, w), F32), pltpu.VMEM((tt, w), F32), pltpu.VMEM((tt, w), F32),
                        pltpu.VMEM((1, w), F32)],
        compiler_params=_cparams("parallel", "arbitrary"),
    )(xb, gate, conv_w, conv_b.reshape(1, w), wbd, bias, lam.reshape(1, w))


def _post_kernel(x_ref, oh_ref, y2_ref, wo_ref, g_ref, wu_ref, wg_ref, wd_ref, out_ref,
                 x1_scr, hb_scr, acc_scr):
    j = pl.program_id(1)
    nh, _, hd = oh_ref.shape

    @pl.when(j == 0)
    def _():
        mix = _dot(y2_ref[...], wo_ref[nh * hd:, :])
        for h in range(nh):
            mix = mix + _dot(oh_ref[h], wo_ref[h * hd:(h + 1) * hd, :])
        x1 = x_ref[...] + mix
        x1_scr[...] = x1
        hb_scr[...] = _rms(x1, g_ref[...]).astype(BF16)
        acc_scr[...] = jnp.zeros_like(acc_scr)

    hb = hb_scr[...]
    u = _dot(hb, wu_ref[...])
    gg = _dot(hb, wg_ref[...])
    acc_scr[...] += _dot((_silu(u) * gg).astype(BF16), wd_ref[...])

    @pl.when(j == pl.num_programs(1) - 1)
    def _():
        out_ref[...] = x1_scr[...] + acc_scr[...]


def _post(x, oh, y2, w_out_bf, g, w_in_bf, w_dn_bf, *, tm, th):
    m, d = x.shape
    nh, _, hd = oh.shape
    hid = w_dn_bf.shape[0]
    nj = hid // th
    return pl.pallas_call(
        _post_kernel,
        grid=(m // tm, nj),
        in_specs=[pl.BlockSpec((tm, d), lambda i, j: (i, 0)),
                  pl.BlockSpec((nh, tm, hd), lambda i, j: (0, i, 0)),
                  pl.BlockSpec((tm, y2.shape[1]), lambda i, j: (i, 0)),
                  pl.BlockSpec(w_out_bf.shape, lambda i, j: (0, 0)),
                  pl.BlockSpec((1, d), lambda i, j: (0, 0)),
                  pl.BlockSpec((d, th), lambda i, j: (0, j)),
                  pl.BlockSpec((d, th), lambda i, j: (0, nj + j)),
                  pl.BlockSpec((th, d), lambda i, j: (j, 0))],
        out_specs=pl.BlockSpec((tm, d), lambda i, j: (i, 0)),
        out_shape=jax.ShapeDtypeStruct((m, d), F32),
        scratch_shapes=[pltpu.VMEM((tm, d), F32), pltpu.VMEM((tm, d), BF16), pltpu.VMEM((tm, d), F32)],
        compiler_params=_cparams("parallel", "arbitrary"),
    )(x, oh, y2, w_out_bf, g.reshape(1, d), w_in_bf, w_in_bf, w_dn_bf)


def _rope_tables(pos, nope, rope):
    half = rope // 2
    inv = ROPE_THETA ** (-jnp.arange(half, dtype=F32) / half)
    ang = pos.astype(F32)[:, None] * inv[None, :]
    n = pos.shape[0]
    lead = jnp.zeros((n, nope), F32)
    trail = jnp.zeros((n, LANES - nope - rope), F32)
    cos = jnp.concatenate([lead, jnp.cos(ang), jnp.cos(ang), trail], axis=1)
    sin = jnp.concatenate([lead, jnp.sin(ang), jnp.sin(ang), trail], axis=1)
    return cos, sin


def _rot_partner(w):
    half = w.shape[-1] // 2
    return jnp.concatenate([-w[..., half:], w[..., :half]], axis=-1)


def _cd_weights(w_in, w_uq, w_ukv, gq, gk, dt_bias, a_log, d_skip, *, kv_rank, q_rank, rope, inner, conv_dim):
    d = w_in.shape[0]
    nh, qk = w_uq.shape[1], w_uq.shape[2]
    nope = qk - rope
    vd = w_ukv.shape[2] - nope
    n_ssd = dt_bias.shape[0]
    assert qk <= LANES and kv_rank == LANES and n_ssd <= LANES
    o = 0
    cq = w_in[:, o:o + q_rank]; o += q_rank
    ckv = w_in[:, o:o + kv_rank]; o += kv_rank
    kpe = w_in[:, o:o + rope]; o += rope
    z = w_in[:, o:o + inner]; o += inner
    xbc = w_in[:, o:o + conv_dim]; o += conv_dim
    dt = w_in[:, o:o + n_ssd]

    def on_rope_lanes(w):
        return jnp.concatenate([jnp.zeros((d, nope), F32), w, jnp.zeros((d, LANES - qk), F32)], axis=1)

    w_pad = jnp.concatenate([cq, ckv, on_rope_lanes(kpe), on_rope_lanes(_rot_partner(kpe)), z, xbc,
                             dt, jnp.zeros((d, LANES - n_ssd), F32)], axis=1).astype(BF16)
    zq = jnp.zeros((q_rank, nh, LANES - qk), F32)
    wqm = jnp.concatenate([w_uq, zq], axis=2).reshape(q_rank, nh * LANES).astype(BF16)
    wqp = jnp.concatenate([jnp.zeros((q_rank, nh, nope), F32), _rot_partner(w_uq[:, :, nope:]), zq],
                          axis=2).reshape(q_rank, nh * LANES).astype(BF16)
    wk = w_ukv[:, :, :nope]
    wk128 = jnp.concatenate([wk, jnp.zeros((kv_rank, nh, LANES - nope), F32)], axis=2)
    pad1 = lambda v, fill=0.0: jnp.concatenate([v, jnp.full((LANES - v.shape[0],), fill, F32)]).reshape(1, LANES)
    return dict(
        w_pad=w_pad, wqm=wqm, wqp=wqp,
        wk128=wk128.reshape(kv_rank, nh * LANES).astype(BF16),
        wkt=wk.reshape(kv_rank, nh * nope).T.astype(BF16),
        wv=w_ukv[:, :, nope:].reshape(kv_rank, nh * vd).astype(BF16),
        gq128=pad1(gq * (math.log2(math.e) * qk ** -0.5)),
        gk128=pad1(gk), dtb128=pad1(dt_bias), alog128=pad1(a_log),
        dsk=jnp.repeat(d_skip, inner // n_ssd).reshape(1, inner),
        nh=nh, qk=qk, nope=nope, rope=rope, vd=vd, q_rank=q_rank, kv_rank=kv_rank, inner=inner,
        conv_dim=conv_dim, n_ssd=n_ssd)


def _cd_inproj_kernel(x_ref, g_ref, w_ref, gql_ref, gkl_ref, wqm_ref, wqp_ref, wk_ref, wv_ref, gq_ref, gk_ref,
                      cos_ref, sin_ref, qh_ref, kh_ref, vh_ref, lat_ref, z_ref, xbc_ref, dt_ref,
                      *, nh, qk, nope, rope, vd, q_rank, kv_rank, inner, conv_dim):
    h = _rms(x_ref[...], g_ref[...]).astype(BF16)
    p = _dot(h, w_ref[...])
    o = 0
    cq = p[:, o:o + q_rank]; o += q_rank
    ckv = p[:, o:o + kv_rank]; o += kv_rank
    kpe = p[:, o:o + LANES]; o += LANES
    kpe_rot = p[:, o:o + LANES]; o += LANES
    z_ref[...] = p[:, o:o + inner]; o += inner
    xbc_ref[...] = p[:, o:o + conv_dim]; o += conv_dim
    dt_ref[...] = p[:, o:o + LANES]

    cos_t = cos_ref[...]
    sin_t = sin_ref[...]
    lane = lax.broadcasted_iota(jnp.int32, (1, LANES), 1)
    cos_q = cos_t + (lane < nope).astype(F32)
    cqn = _rms(cq, gql_ref[...]).astype(BF16)
    qm = _dot(cqn, wqm_ref[...])
    qp = _dot(cqn, wqp_ref[...])
    for i in range(nh):
        qi = qm[:, i * LANES:(i + 1) * LANES] * cos_q + qp[:, i * LANES:(i + 1) * LANES] * sin_t
        ms = jnp.sum(qi * qi, axis=-1, keepdims=True) * (1.0 / qk)
        qh_ref[i] = (qi * lax.rsqrt(ms + NORM_EPS) * gq_ref[...]).astype(BF16)

    ckvn = _rms(ckv, gkl_ref[...])
    kper = kpe * cos_t + kpe_rot * sin_t
    lat_ref[:, :kv_rank] = ckvn
    lat_ref[:, kv_rank:] = kper[:, nope:nope + rope]
    cb = ckvn.astype(BF16)
    kn = _dot(cb, wk_ref[...])
    vv = _dot(cb, wv_ref[...])
    for i in range(nh):
        ki = kn[:, i * LANES:(i + 1) * LANES] + kper
        ms = jnp.sum(ki * ki, axis=-1, keepdims=True) * (1.0 / qk)
        kh_ref[i] = (ki * lax.rsqrt(ms + NORM_EPS) * gk_ref[...]).astype(BF16)
        vh_ref[i] = vv[:, i * vd:(i + 1) * vd].astype(BF16)


def _cd_inproj(x, g, cw, gql, gkl, cos, sin, *, tm):
    m, d = x.shape
    nh, vd, kv_rank, rope = cw["nh"], cw["vd"], cw["kv_rank"], cw["rope"]
    inner, conv_dim, q_rank = cw["inner"], cw["conv_dim"], cw["q_rank"]
    n_pos = cos.shape[0] // tm
    row = lambda i: (i, 0)
    head = lambda i: (0, i, 0)
    dims = {k: cw[k] for k in ("nh", "qk", "nope", "rope", "vd", "q_rank", "kv_rank", "inner", "conv_dim")}
    return pl.pallas_call(
        functools.partial(_cd_inproj_kernel, **dims),
        grid=(m // tm,),
        in_specs=[pl.BlockSpec((tm, d), row), _full((1, d)), _full(cw["w_pad"].shape), _full((1, q_rank)),
                  _full((1, kv_rank)), _full(cw["wqm"].shape), _full(cw["wqp"].shape), _full(cw["wk128"].shape),
                  _full(cw["wv"].shape), _full((1, LANES)), _full((1, LANES)),
                  pl.BlockSpec((tm, LANES), lambda i: (i % n_pos, 0)),
                  pl.BlockSpec((tm, LANES), lambda i: (i % n_pos, 0))],
        out_specs=[pl.BlockSpec((nh, tm, LANES), head), pl.BlockSpec((nh, tm, LANES), head),
                   pl.BlockSpec((nh, tm, vd), head), pl.BlockSpec((tm, kv_rank + rope), row),
                   pl.BlockSpec((tm, inner), row), pl.BlockSpec((tm, conv_dim), row),
                   pl.BlockSpec((tm, LANES), row)],
        out_shape=[jax.ShapeDtypeStruct((nh, m, LANES), BF16), jax.ShapeDtypeStruct((nh, m, LANES), BF16),
                   jax.ShapeDtypeStruct((nh, m, vd), BF16), jax.ShapeDtypeStruct((m, kv_rank + rope), F32),
                   jax.ShapeDtypeStruct((m, inner), F32), jax.ShapeDtypeStruct((m, conv_dim), F32),
                   jax.ShapeDtypeStruct((m, LANES), F32)],
        compiler_params=_cparams("parallel"),
    )(x, g.reshape(1, d), cw["w_pad"], gql.reshape(1, q_rank), gkl.reshape(1, kv_rank), cw["wqm"], cw["wqp"],
      cw["wk128"], cw["wv"], cw["gq128"], cw["gk128"], cos, sin)


_MLA_DEPTH = 3
_MASKED = -1e30


def _mla_prompt_kernel(qi_ref, kb_ref, kind_ref, q_ref, k_ref, v_ref, bias_ref, o_ref,
                       s_ring, p_ring, pv_ring, a_ring, linv_ring, m_scr, l_scr, acc_scr, *, tq, n_steps):
    halves = [slice(c * (tq // 2), (c + 1) * (tq // 2)) for c in range(2)]
    vd = v_ref.shape[-1]

    def blk(idx):
        return pl.ds(pl.multiple_of(idx * tq, tq), tq)

    for ring in (s_ring, p_ring, pv_ring, a_ring, linv_ring, l_scr, acc_scr):
        ring[...] = jnp.zeros_like(ring)
    m_scr[...] = jnp.full(m_scr.shape, -jnp.inf, F32)

    def step(t, ph):
        acc = a_ring[(ph + 1) % 4, :, :vd] * acc_scr[...] + pv_ring[(ph + 1) % 2]
        acc_scr[...] = acc
        o_ref[blk(qi_ref[t]), :] = (acc * linv_ring[(ph + 1) % 4, :, :vd]).astype(o_ref.dtype)
        v = v_ref[blk(kb_ref[t + 1]), :]
        outs = [_dot(p_ring[ph % 2, h, :], v) for h in halves]
        for h, o in zip(halves, outs):
            pv_ring[ph % 2, h, :] = o
        kind = kind_ref[t + 2]
        tiles = [slice(c * LANES, (c + 1) * LANES) for c in range(tq // LANES)]
        s = [s_ring[(ph + 1) % 2, :, c] + bias_ref[kind, :, c] for c in tiles]
        m_prev = jnp.where(kind == _SB_DIAG, -jnp.inf, m_scr[...])
        m_blk = jnp.max(functools.reduce(jnp.maximum, s), axis=-1, keepdims=True)
        m_new = jnp.maximum(m_prev, jnp.broadcast_to(m_blk, m_prev.shape))
        alpha = jnp.exp2(m_prev - m_new)
        p = [jnp.exp2(sc - m_new) for sc in s]
        l_blk = jnp.sum(functools.reduce(jnp.add, p), axis=-1, keepdims=True)
        l = alpha * l_scr[...] + jnp.broadcast_to(l_blk, m_prev.shape)
        m_scr[...] = m_new
        l_scr[...] = l
        a_ring[(ph + 3) % 4] = alpha
        linv_ring[(ph + 3) % 4] = 1.0 / l
        for c, pc in zip(tiles, p):
            p_ring[(ph + 1) % 2, :, c] = pc.astype(BF16)
        k = k_ref[blk(kb_ref[t + 3]), :]
        qrows = blk(qi_ref[t + 3])
        ss = [lax.dot_general(q_ref[qrows, :][h], k, _NT, preferred_element_type=F32) for h in halves]
        for h, sc in zip(halves, ss):
            s_ring[ph % 2, h, :] = sc

    def four_steps(j, carry):
        for ph in range(4):
            step(4 * j + ph, ph)
        return carry

    lax.fori_loop(0, n_steps // 4, four_steps, 0)


def _mla_prompt(qh, kh, vh, *, nb, tq):
    nh, m, dk = qh.shape
    vd = vh.shape[-1]
    t = m // nb
    n_blk = t // tq
    items = [(i, i if s == 0 else s - 1, _SB_DIAG if s == 0 else _SB_FULL) for i in range(n_blk) for s in range(i + 1)]
    n_steps = -(-(len(items) + _MLA_DEPTH) // 4) * 4
    items = [(0, 0, _SB_IDLE)] * _MLA_DEPTH + items
    items = items + [(n_blk - 1, 0, _SB_IDLE)] * (n_steps + _MLA_DEPTH - len(items))
    qi_tab, kb_tab, kind_tab = (jnp.asarray([it[c] for it in items], jnp.int32) for c in range(3))
    causal = lax.broadcasted_iota(jnp.int32, (tq, tq), 1) <= lax.broadcasted_iota(jnp.int32, (tq, tq), 0)
    bias = jnp.stack([jnp.where(causal, 0.0, _MASKED), jnp.zeros((tq, tq), F32), jnp.full((tq, tq), _MASKED, F32)])
    per_seq = lambda b, h, *_: (h, b, 0)
    grid_spec = pltpu.PrefetchScalarGridSpec(
        num_scalar_prefetch=3, grid=(nb, nh),
        in_specs=[pl.BlockSpec((None, t, dk), per_seq), pl.BlockSpec((None, t, dk), per_seq),
                  pl.BlockSpec((None, t, vd), per_seq), pl.BlockSpec((3, tq, tq), lambda b, h, *_: (0, 0, 0))],
        out_specs=pl.BlockSpec((None, t, vd), per_seq),
        scratch_shapes=[pltpu.VMEM((2, tq, tq), F32), pltpu.VMEM((2, tq, tq), BF16), pltpu.VMEM((2, tq, vd), F32),
                        pltpu.VMEM((4, tq, LANES), F32), pltpu.VMEM((4, tq, LANES), F32), pltpu.VMEM((tq, LANES), F32),
                        pltpu.VMEM((tq, LANES), F32), pltpu.VMEM((tq, vd), F32)])
    return pl.pallas_call(
        functools.partial(_mla_prompt_kernel, tq=tq, n_steps=n_steps),
        grid_spec=grid_spec,
        out_shape=jax.ShapeDtypeStruct((nh, m, vd), BF16),
        compiler_params=_cparams("parallel", "parallel"),
        name="mla_prompt",
    )(qi_tab, kb_tab, kind_tab, qh, kh, vh, bias)


def _causal_conv_tile(x_ref, xpad, cw_ref, cb_ref, cbuf_ref, rows):
    pad = 8
    xpad[pad:pad + rows, :] = x_ref[...]
    xc = cb_ref[...]
    for i in range(CONV_WIDTH):
        o = pad - (CONV_WIDTH - 1) + i
        xc = xc + xpad[o:o + rows, :] * cw_ref[i:i + 1, :]
    cbuf_ref[...] = xpad[pad + rows - (CONV_WIDTH - 1):pad + rows, :]
    xpad[0:pad, :] = xpad[rows:rows + pad, :]
    return xc


def _gated_group_norm(y, z, ng_ref, y_ref, n_groups):
    y = y * _silu(z)
    gs = y.shape[-1] // n_groups
    for g in range(n_groups):
        y_ref[:, g * gs:(g + 1) * gs] = _rms(y[:, g * gs:(g + 1) * gs], ng_ref[:, g * gs:(g + 1) * gs]).astype(y_ref.dtype)


def _ssd_prompt_kernel(xbc_ref, z_ref, dt_ref, cw_ref, cb_ref, dtb_ref, alog_ref, dsk_ref, ng_ref, ltri_ref,
                       y_ref, hlast_ref, cbuf_ref, xpad, state, yscr, *, q, nh, hd, ns, n_groups):
    c = pl.program_id(1)

    @pl.when(c == 0)
    def _():
        state[...] = jnp.zeros_like(state)
        xpad[0:8, :] = jnp.zeros((8, xpad.shape[1]), F32)

    xc = _silu(_causal_conv_tile(xbc_ref, xpad, cw_ref, cb_ref, cbuf_ref, q))
    inner = nh * hd
    hpg = nh // n_groups
    xs = xc[:, :inner]
    bm = xc[:, inner:inner + n_groups * ns]
    cm = xc[:, inner + n_groups * ns:]
    lane = lax.broadcasted_iota(jnp.int32, (1, LANES), 1)
    dtv = _softplus(dt_ref[...] + dtb_ref[...])
    a = jnp.where(lane < nh, -jnp.exp(alog_ref[...]), 0.0)
    ltri = ltri_ref[...]
    cum = sum(_dot(ltri, part) for part in _split3(dtv * a))
    cum_t = cum.T
    dt_t = dtv.T
    cum_last = cum[q - 1:q, :]
    to_end = jnp.exp(cum_last - cum) * dtv
    ecum = jnp.exp(cum)
    elast = jnp.exp(cum_last)
    causal = lax.broadcasted_iota(jnp.int32, (q, q), 0) >= lax.broadcasted_iota(jnp.int32, (q, q), 1)
    for g in range(n_groups):
        cmg = cm[:, g * ns:(g + 1) * ns].astype(BF16)
        bmg = bm[:, g * ns:(g + 1) * ns].astype(BF16)
        cb = lax.dot_general(cmg, bmg, _NT, preferred_element_type=F32)
        for r in range(g * hpg, (g + 1) * hpg):
            seg = cum[:, r:r + 1] - cum_t[r:r + 1, :]
            decay = jnp.exp(jnp.where(causal, seg, -jnp.inf))
            mix = cb * decay * dt_t[r:r + 1, :]
            xh = xs[:, r * hd:(r + 1) * hd]
            hprev = state[r]
            y = _dot(mix.astype(BF16), xh.astype(BF16))
            y = y + lax.dot_general(cmg, hprev.astype(BF16), _NT, preferred_element_type=F32) * ecum[:, r:r + 1]
            yscr[:, r * hd:(r + 1) * hd] = y + dsk_ref[:, r * hd:(r + 1) * hd] * xh
            xw = (xh * to_end[:, r:r + 1]).astype(BF16)
            state[r] = elast[:, r:r + 1] * hprev + lax.dot_general(xw, bmg, _TN, preferred_element_type=F32)
    hlast_ref[...] = state[...]
    _gated_group_norm(yscr[...], z_ref[...], ng_ref, y_ref, n_groups)


def _lower_tri_ones(n):
    r = lax.broadcasted_iota(jnp.int32, (n, n), 0)
    c = lax.broadcasted_iota(jnp.int32, (n, n), 1)
    return (r >= c).astype(BF16)


def _ssd_prompt(xbc, z, dt, conv_w, conv_b, cw, norm_g, *, nb, q, state_dim):
    m, conv_dim = xbc.shape
    inner, nh = cw["inner"], cw["n_ssd"]
    hd = inner // nh
    n_groups = (conv_dim - inner) // (2 * state_dim)
    t = m // nb
    nc = t // q
    row = lambda b, c: (b * nc + c, 0)
    return pl.pallas_call(
        functools.partial(_ssd_prompt_kernel, q=q, nh=nh, hd=hd, ns=state_dim, n_groups=n_groups),
        grid=(nb, nc),
        in_specs=[pl.BlockSpec((q, conv_dim), row), pl.BlockSpec((q, inner), row), pl.BlockSpec((q, LANES), row),
                  _full((CONV_WIDTH, conv_dim)), _full((1, conv_dim)), _full((1, LANES)), _full((1, LANES)),
                  _full((1, inner)), _full((1, inner)), _full((q, q))],
        out_specs=[pl.BlockSpec((q, inner), row),
                   pl.BlockSpec((None, nh, hd, state_dim), lambda b, c: (b, 0, 0, 0)),
                   pl.BlockSpec((None, CONV_WIDTH - 1, conv_dim), lambda b, c: (b, 0, 0))],
        out_shape=[jax.ShapeDtypeStruct((m, inner), BF16), jax.ShapeDtypeStruct((nb, nh, hd, state_dim), F32),
                   jax.ShapeDtypeStruct((nb, CONV_WIDTH - 1, conv_dim), F32)],
        scratch_shapes=[pltpu.VMEM((q + 8, conv_dim), F32), pltpu.VMEM((nh, hd, state_dim), F32),
                        pltpu.VMEM((q, inner), F32)],
        compiler_params=_cparams("parallel", "arbitrary"),
    )(xbc, z, dt, conv_w, conv_b.reshape(1, conv_dim), cw["dtb128"], cw["alog128"], cw["dsk"],
      norm_g.reshape(1, inner), _lower_tri_ones(q))


def _sb_sample_kernel(pt_ref, qbd_ref, uu_ref, *rest, npg, dk, grp):
    pages = rest[:npg]
    o_ref, z_scr, r_scr, acc_scr = rest[npg:]
    c = pl.program_id(1)
    nrow = qbd_ref.shape[0]
    n_kv = nrow // grp
    hd = dk // n_kv
    keys = z_scr.shape[1]

    @pl.when(c == 0)
    def _():
        r_scr[...] = jnp.zeros_like(r_scr)
        acc_scr[...] = jnp.zeros_like(acc_scr)

    qbd = qbd_ref[...]
    for j in range(npg):
        z_scr[j * nrow:(j + 1) * nrow, :] = _dot(qbd, pages[j][:dk, :].astype(BF16))
    nz = z_scr[...]
    hi, lo = _split2(_sb_log2_keep(nz))
    tl = _dot(jnp.concatenate([hi, lo], axis=1), uu_ref[...])
    tot = jnp.broadcast_to(tl[:, 0:1], tl.shape)
    r = r_scr[...]
    carries = [None] * npg
    for j in reversed(range(npg)):
        carries[j] = r
        r = r + tot[j * nrow:(j + 1) * nrow, :]
    r_scr[...] = r
    z_scr[...] = jnp.exp2(tl + jnp.concatenate(carries, axis=0) - nz)
    for j in range(npg):
        vt = pages[j][dk:, :]
        for g in range(grp):
            base = j * nrow + g * n_kv
            wexp = jnp.concatenate(
                [jnp.broadcast_to(z_scr[base + h:base + h + 1, :], (hd, keys)) for h in range(n_kv)], axis=0)
            acc_scr[g] += vt * wexp

    @pl.when(c == pl.num_programs(1) - 1)
    def _():
        for g in range(grp):
            o_ref[g:g + 1, :] = jnp.sum(acc_scr[g].T, axis=0, keepdims=True)


def _sb_sample(qh, cache_t, layer, page_table, *, n_kv, npg=32):
    n_q, s, hd = qh.shape
    grp = n_q // n_kv
    dk = n_kv * hd
    page = cache_t.shape[3]
    n_pages = page_table.shape[1]
    npg = min(npg, n_pages)
    nch = n_pages // npg
    eye = jnp.eye(n_kv, dtype=qh.dtype)
    qbd = jnp.einsum('hgsd,hk->sghkd', qh.reshape(n_kv, grp, s, hd), eye).reshape(s, n_q, dk)

    def page_spec(j):
        return pl.BlockSpec((None, None, 2 * dk, page),
                            lambda i, c, pt: (layer, pt[i, (nch - 1 - c) * npg + j], 0, 0))

    grid_spec = pltpu.PrefetchScalarGridSpec(
        num_scalar_prefetch=1, grid=(s, nch),
        in_specs=[pl.BlockSpec((None, n_q, dk), lambda i, c, pt: (i, 0, 0)),
                  pl.BlockSpec((2 * page, page), lambda i, c, pt: (0, 0))] + [page_spec(j) for j in range(npg)],
        out_specs=pl.BlockSpec((None, grp, dk), lambda i, c, pt: (i, 0, 0)),
        scratch_shapes=[pltpu.VMEM((npg * n_q, page), F32), pltpu.VMEM((n_q, page), F32),
                        pltpu.VMEM((grp, dk, page), F32)])
    og = pl.pallas_call(
        functools.partial(_sb_sample_kernel, npg=npg, dk=dk, grp=grp),
        grid_spec=grid_spec,
        out_shape=jax.ShapeDtypeStruct((s, grp, dk), F32),
        compiler_params=_cparams("parallel", "arbitrary"),
        name="sb_sample",
    )(page_table, qbd, _rev_cumsum_ones(page), *([cache_t] * npg))
    o = og.reshape(s, grp, n_kv, hd)
    return jnp.transpose(o, (2, 1, 0, 3)).reshape(n_q, s, hd).astype(BF16)


def _conv_step(x, buf_ref, nbuf_ref, cw_ref, cb_ref):
    xc = cb_ref[...]
    for i in range(CONV_WIDTH - 1):
        xc = xc + buf_ref[i] * cw_ref[i:i + 1, :]
        if i > 0:
            nbuf_ref[i - 1] = buf_ref[i]
    nbuf_ref[CONV_WIDTH - 2] = x
    return xc + x * cw_ref[CONV_WIDTH - 1:CONV_WIDTH, :]


def _rglru_step_kernel(xb_ref, gate_ref, buf_ref, h0_ref, cw_ref, cb_ref, wbd_ref, bias_ref, lam_ref,
                       y_ref, h_ref, nbuf_ref):
    xc = _conv_step(xb_ref[...], buf_ref, nbuf_ref, cw_ref, cb_ref)
    a, b = _lru_gates(xc, wbd_ref, bias_ref, lam_ref)
    h = a * h0_ref[...] + b
    h_ref[...] = h
    y_ref[...] = (_gelu_tanh(gate_ref[...]) * h).astype(y_ref.dtype)


def _rglru_step(xb, gate, buf, h0, conv_w, conv_b, wbd, bias, lam):
    s, w = xb.shape
    return pl.pallas_call(
        _rglru_step_kernel,
        out_shape=[jax.ShapeDtypeStruct((s, w), BF16), jax.ShapeDtypeStruct((s, w), F32),
                   jax.ShapeDtypeStruct((CONV_WIDTH - 1, s, w), F32)],
        compiler_params=pltpu.CompilerParams(vmem_limit_bytes=VMEM_LIMIT_BYTES),
    )(xb, gate, buf, h0, conv_w, conv_b.reshape(1, w), wbd, bias, lam.reshape(1, w))


def _mla_sample_kernel(pt_ref, qn_ref, qr_ref, new_ref, wkt_ref, wv_ref, *rest,
                       n_pages, page, kv_rank, rope, qk, nh, ppi):
    pages = rest[:n_pages]
    o_ref, ckv_scr, kpe_scr, s_scr = rest[n_pages:]
    n_iter = n_pages // ppi
    width = ppi * page
    nope = wkt_ref.shape[0] // nh
    for j in range(n_pages):
        lanes = slice((j % ppi) * page, (j % ppi + 1) * page)
        ckv_scr[j // ppi, :, lanes] = pages[j][:kv_rank, :].astype(BF16)
        kpe_scr[j // ppi, :, lanes] = pages[j][kv_rank:, :]
    qr = qr_ref[...]
    wkt = wkt_ref[...]
    q_lat = _dot(qn_ref[...], wkt).astype(BF16)

    def scores(ct, kp, w):
        kn = _dot(wkt, ct).reshape(nh, nope, w)
        ssq = jnp.sum(kn * kn, axis=1) + jnp.sum(kp * kp, axis=0, keepdims=True)
        s = _dot(q_lat, ct) + _dot(qr, kp.astype(BF16))
        return s * lax.rsqrt(ssq * (1.0 / qk) + NORM_EPS)

    new_t = jnp.broadcast_to(jnp.concatenate([new_ref[...], jnp.zeros((1, 2 * LANES - kv_rank - rope), F32)], axis=1),
                             (LANES, 2 * LANES)).T
    ct_new = new_t[:kv_rank, :].astype(BF16)
    first = lax.broadcasted_iota(jnp.int32, (nh, LANES), 1) == 0
    s_new = jnp.where(first, scores(ct_new, new_t[kv_rank:kv_rank + rope, :], LANES), -jnp.inf)

    def score_pass(i, m):
        s = scores(ckv_scr[i], kpe_scr[i], width)
        s_scr[i] = s
        return jnp.maximum(m, jnp.max(s, axis=1, keepdims=True))

    unroll = 2 if n_iter % 2 == 0 else 1
    m = lax.fori_loop(0, n_iter, score_pass, jnp.max(s_new, axis=1, keepdims=True), unroll=unroll)
    p_new = jnp.exp2(s_new - m)

    def value_pass(i, c):
        l, acc_t = c
        p = jnp.exp2(s_scr[i] - m)
        acc_t = acc_t + lax.dot_general(ckv_scr[i], p.astype(BF16), _NT, preferred_element_type=F32)
        return l + jnp.sum(p, axis=1, keepdims=True), acc_t

    l0 = jnp.sum(p_new, axis=1, keepdims=True)
    acc0 = lax.dot_general(ct_new, p_new.astype(BF16), _NT, preferred_element_type=F32)
    l, acc_t = lax.fori_loop(0, n_iter, value_pass, (l0, acc0), unroll=unroll)
    acc = jnp.concatenate([acc_t, jnp.zeros((kv_rank, LANES - nh), F32)], axis=1).T[:nh, :]
    hi, lo = _split2(acc / l)
    o_ref[...] = _dot(hi, wv_ref[...]) + _dot(lo, wv_ref[...])


def _mla_sample(qh, lat_new, cache, layer, page_table, cw, gk):
    nh, s, _ = qh.shape
    nope, rope, vd, kv_rank, qk = cw["nope"], cw["rope"], cw["vd"], cw["kv_rank"], cw["qk"]
    page = cache.shape[3]
    n_pages = page_table.shape[1]
    ppi = max(c for c in (1, 2, 4) if n_pages % c == 0)
    qg = jnp.swapaxes(qh.astype(F32) * cw["gk128"].reshape(1, 1, LANES), 0, 1)
    qn = jnp.einsum('shd,hk->shkd', qg[:, :, :nope], jnp.eye(nh, dtype=F32)).reshape(s, nh, nh * nope).astype(BF16)
    qr = qg[:, :, nope:qk].astype(BF16)
    new = lat_new.reshape(s, 1, kv_rank + rope)

    def page_spec(j):
        return pl.BlockSpec((None, None, kv_rank + rope, page), lambda i, pt: (layer, pt[i, j], 0, 0))

    grid_spec = pltpu.PrefetchScalarGridSpec(
        num_scalar_prefetch=1, grid=(s,),
        in_specs=[pl.BlockSpec((None, nh, nh * nope), lambda i, pt: (i, 0, 0)),
                  pl.BlockSpec((None, nh, rope), lambda i, pt: (i, 0, 0)),
                  pl.BlockSpec((None, 1, kv_rank + rope), lambda i, pt: (i, 0, 0)),
                  pl.BlockSpec(cw["wkt"].shape, lambda i, pt: (0, 0)),
                  pl.BlockSpec(cw["wv"].shape, lambda i, pt: (0, 0))] + [page_spec(j) for j in range(n_pages)],
        out_specs=pl.BlockSpec((None, nh, nh * vd), lambda i, pt: (i, 0, 0)),
        scratch_shapes=[pltpu.VMEM((n_pages // ppi, kv_rank, ppi * page), BF16),
                        pltpu.VMEM((n_pages // ppi, rope, ppi * page), F32),
                        pltpu.VMEM((n_pages // ppi, nh, ppi * page), F32)])
    om = pl.pallas_call(
        functools.partial(_mla_sample_kernel, n_pages=n_pages, page=page, kv_rank=kv_rank, rope=rope, qk=qk, nh=nh,
                          ppi=ppi),
        grid_spec=grid_spec,
        out_shape=jax.ShapeDtypeStruct((s, nh, nh * vd), F32),
        compiler_params=_cparams("parallel"),
        name="mla_sample",
    )(page_table, qn, qr, new, cw["wkt"], cw["wv"], *([cache] * n_pages))
    idx = jnp.arange(nh)
    o = om.reshape(s, nh, nh, vd)[:, idx, idx, :]
    return jnp.swapaxes(o, 0, 1).astype(BF16)


def _ssd_step_kernel(xbc_ref, z_ref, dt_ref, buf_ref, h0_ref, cw_ref, cb_ref, dtb_ref, alog_ref, dsk_ref, ng_ref,
                     y_ref, hnew_ref, nbuf_ref, xc_scr, xct_scr, dtt_scr, dat_scr, yoff_scr,
                     *, nh, hd, ns, n_groups):
    r = pl.program_id(0)
    inner = nh * hd
    hpg = nh // n_groups
    lane = lax.broadcasted_iota(jnp.int32, (1, LANES), 1)

    @pl.when(r == 0)
    def _():
        xc = _silu(_conv_step(xbc_ref[...], buf_ref, nbuf_ref, cw_ref, cb_ref))
        xc_scr[...] = xc
        xct_scr[...] = xc.T
        dtv = _softplus(dt_ref[...] + dtb_ref[...])
        a = jnp.where(lane < nh, -jnp.exp(alog_ref[...]), 0.0)
        dtt_scr[...] = dtv.T
        dat_scr[...] = jnp.exp(dtv * a).T

    g = r // hpg
    xt = xct_scr[pl.ds(pl.multiple_of(r * hd, hd), hd), :]
    bt = xct_scr[pl.ds(pl.multiple_of(inner + g * ns, ns), ns), :]
    ct = xct_scr[pl.ds(pl.multiple_of(inner + (n_groups + g) * ns, ns), ns), :]
    dar = dat_scr[pl.ds(r, 1), :]
    coef = xt * dtt_scr[pl.ds(r, 1), :]
    for p in range(hd):
        h0p = h0_ref[p * ns:(p + 1) * ns, :]
        hnew_ref[p * ns:(p + 1) * ns, :] = dar * h0p + coef[p:p + 1, :] * bt
        yoff_scr[pl.ds(r * hd + p, 1), :] = jnp.sum(ct * h0p, axis=0, keepdims=True)

    @pl.when(r == nh - 1)
    def _():
        xc = xc_scr[...]
        xs = xc[:, :inner]
        dtv = dtt_scr[...].T
        da = dat_scr[...].T
        per_head = lambda v: jnp.concatenate(
            [jnp.broadcast_to(v[:, i:i + 1], (v.shape[0], hd)) for i in range(nh)], axis=1)
        cb = []
        for gi in range(n_groups):
            bmg = xc[:, inner + gi * ns:inner + (gi + 1) * ns]
            cmg = xc[:, inner + (n_groups + gi) * ns:inner + (n_groups + gi + 1) * ns]
            cb.append(jnp.broadcast_to(jnp.sum(cmg * bmg, axis=-1, keepdims=True), (xc.shape[0], hpg * hd)))
        y = jnp.concatenate(cb, axis=1) * per_head(dtv) * xs + yoff_scr[...].T * per_head(da) + dsk_ref[...] * xs
        _gated_group_norm(y, z_ref[...], ng_ref, y_ref, n_groups)


def _ssd_step(xbc, z, dt, buf, h0, conv_w, conv_b, cw, norm_g):
    s, conv_dim = xbc.shape
    _, nh, hd, ns = h0.shape
    inner = cw["inner"]
    n_groups = (conv_dim - inner) // (2 * ns)
    sz = hd * ns
    h0t = jnp.transpose(h0, (1, 2, 3, 0)).reshape(nh * sz, s)
    y, hnew, nbuf = pl.pallas_call(
        functools.partial(_ssd_step_kernel, nh=nh, hd=hd, ns=ns, n_groups=n_groups),
        grid=(nh,),
        in_specs=[_full((s, conv_dim)), _full((s, inner)), _full((s, LANES)), _full((CONV_WIDTH - 1, s, conv_dim)),
                  pl.BlockSpec((sz, s), lambda r: (r, 0)), _full((CONV_WIDTH, conv_dim)), _full((1, conv_dim)),
                  _full((1, LANES)), _full((1, LANES)), _full((1, inner)), _full((1, inner))],
        out_specs=[_full((s, inner)), pl.BlockSpec((sz, s), lambda r: (r, 0)),
                   _full((CONV_WIDTH - 1, s, conv_dim))],
        out_shape=[jax.ShapeDtypeStruct((s, inner), BF16), jax.ShapeDtypeStruct((nh * sz, s), F32),
                   jax.ShapeDtypeStruct((CONV_WIDTH - 1, s, conv_dim), F32)],
        scratch_shapes=[pltpu.VMEM((s, conv_dim), F32), pltpu.VMEM((conv_dim, s), F32), pltpu.VMEM((LANES, s), F32),
                        pltpu.VMEM((LANES, s), F32), pltpu.VMEM((inner, s), F32)],
        compiler_params=_cparams("arbitrary"),
        name="ssd_step",
    )(xbc, z, dt, buf, h0t, conv_w, conv_b.reshape(1, conv_dim), cw["dtb128"], cw["alog128"],
      cw["dsk"], norm_g.reshape(1, inner))
    return y, jnp.transpose(hnew.reshape(nh, hd, ns, s), (3, 0, 1, 2)), nbuf


def _prompt_tiles(t, hidden):
    pick = lambda want: max(c for c in (8, 16, 32, 64, 128, 256, 512, 1024) if c <= want and t % c == 0)
    th = hidden // 2 if hidden % (2 * LANES) == 0 else hidden
    return dict(tm_proj=pick(512), tm_post=pick(512), tq_sb=pick(256), tq_mla=pick(512), tt_lru=pick(512),
                ssd_chunk=pick(128), th=th)


def kernel(x_prompt, x_sample, cache_sb_kv, cache_mla_kv, page_table, state_lru_h, state_lru_conv, state_ssm_h, state_ssm_conv, norm_mix, norm_ffn, ab_w_in, ab_w_out, sb_q_gain, sb_k_gain, lru_conv_w, lru_conv_b, lru_wa, lru_ba, lru_wx, lru_bx, lru_lambda, cd_w_in, cd_w_out, mla_q_lat_gain, mla_w_uq, mla_kv_lat_gain, mla_w_ukv, mla_q_gain, mla_k_gain, ssd_conv_w, ssd_conv_b, ssd_dt_bias, ssd_a_log, ssd_d, ssd_norm_gain, ffn_w_in, ffn_w_out):
    nb, t, d = x_prompt.shape
    ns = x_sample.shape[0]
    depth = norm_mix.shape[0]
    page = cache_sb_kv.shape[2]
    past = page_table.shape[1] * page
    n_kv, hd = cache_sb_kv.shape[4], cache_sb_kv.shape[5]
    n_q = (ab_w_in.shape[2] - 2 * n_kv * hd - 2 * lru_lambda.shape[1]) // hd
    w_lru = lru_lambda.shape[1]
    sizes = _prompt_tiles(t, ffn_w_out.shape[1])

    xp = x_prompt.reshape(nb * t, d)
    xs = x_sample.reshape(ns, d)
    cache_sb = jnp.transpose(cache_sb_kv, (0, 1, 3, 4, 5, 2)).reshape(cache_sb_kv.shape[:2] + (2 * n_kv * hd, page))
    cache_mla = jnp.transpose(cache_mla_kv, (0, 1, 3, 2))
    cos_p, sin_p = _rope_tables(jnp.arange(t), mla_q_gain.shape[1] - (cache_mla_kv.shape[3] - mla_kv_lat_gain.shape[1]),
                                cache_mla_kv.shape[3] - mla_kv_lat_gain.shape[1])
    cos_s, sin_s = _rope_tables(jnp.full((ns,), past), mla_q_gain.shape[1] - (cache_mla_kv.shape[3] - mla_kv_lat_gain.shape[1]),
                                cache_mla_kv.shape[3] - mla_kv_lat_gain.shape[1])
    outs = {k: [] for k in ("sb_p", "sb_s", "lh_p", "lh_s", "lc_p", "lc_s", "ml_p", "ml_s", "sh_p", "sh_s", "sc_p", "sc_s")}
    for li in range(depth):
        w_ffn_in = ffn_w_in[li].astype(BF16)
        w_ffn_out = ffn_w_out[li].astype(BF16)
        if li % 2 == 0:
            e = li // 2
            w_in = ab_w_in[e].astype(BF16)
            w_out = ab_w_out[e].astype(BF16)
            wbd, bias = _lru_weights(lru_wa[e], lru_ba[e], lru_wx[e], lru_bx[e])
            qh, kh, vh, kv, xb, gate = _ab_inproj(xp, norm_mix[li], w_in, sb_q_gain[e], sb_k_gain[e],
                                                  nq=n_q, nk=n_kv, hd=hd, w_lru=w_lru, tm=sizes["tm_proj"])
            oh = _sb_prompt(qh, kh, vh, nb=nb, tq=sizes["tq_sb"])
            y2, hl, cbuf = _rglru_prompt(xb, gate, lru_conv_w[e], lru_conv_b[e], wbd, bias, lru_lambda[e],
                                         nb=nb, tt=sizes["tt_lru"])
            xp = _post(xp, oh, y2, w_out, norm_ffn[li], w_ffn_in, w_ffn_out, tm=sizes["tm_post"], th=sizes["th"])
            outs["sb_p"].append(kv.reshape(nb, t, 2, n_kv, hd))
            outs["lh_p"].append(hl.reshape(nb, w_lru))
            outs["lc_p"].append(cbuf)
            qh, kh, vh, kv, xb, gate = _ab_inproj(xs, norm_mix[li], w_in, sb_q_gain[e], sb_k_gain[e],
                                                  nq=n_q, nk=n_kv, hd=hd, w_lru=w_lru, tm=ns)
            oh = _sb_sample(qh, cache_sb, e, page_table, n_kv=n_kv)
            y2, hl, cbuf = _rglru_step(xb, gate, jnp.swapaxes(state_lru_conv[e], 0, 1), state_lru_h[e],
                                       lru_conv_w[e], lru_conv_b[e], wbd, bias, lru_lambda[e])
            xs = _post(xs, oh, y2, w_out, norm_ffn[li], w_ffn_in, w_ffn_out, tm=ns, th=sizes["th"])
            outs["sb_s"].append(kv.reshape(ns, 1, 2, n_kv, hd))
            outs["lh_s"].append(hl)
            outs["lc_s"].append(jnp.swapaxes(cbuf, 0, 1))
        else:
            o = li // 2
            cw = _cd_weights(cd_w_in[o], mla_w_uq[o], mla_w_ukv[o], mla_q_gain[o], mla_k_gain[o], ssd_dt_bias[o],
                             ssd_a_log[o], ssd_d[o], kv_rank=mla_kv_lat_gain.shape[1], q_rank=mla_q_lat_gain.shape[1],
                             rope=cache_mla_kv.shape[3] - mla_kv_lat_gain.shape[1], inner=ssd_norm_gain.shape[1],
                             conv_dim=ssd_conv_w.shape[2])
            w_out = cd_w_out[o].astype(BF16)
            qh, kh, vh, lat, z, xbc, dt = _cd_inproj(xp, norm_mix[li], cw, mla_q_lat_gain[o], mla_kv_lat_gain[o],
                                                     cos_p, sin_p, tm=sizes["tm_proj"])
            oh = _mla_prompt(qh, kh, vh, nb=nb, tq=sizes["tq_mla"])
            y2, hl, cbuf = _ssd_prompt(xbc, z, dt, ssd_conv_w[o], ssd_conv_b[o], cw, ssd_norm_gain[o],
                                       nb=nb, q=sizes["ssd_chunk"], state_dim=state_ssm_h.shape[4])
            xp = _post(xp, oh, y2, w_out, norm_ffn[li], w_ffn_in, w_ffn_out, tm=sizes["tm_post"], th=sizes["th"])
            outs["ml_p"].append(lat.reshape(nb, t, lat.shape[1]))
            outs["sh_p"].append(hl)
            outs["sc_p"].append(cbuf)
            qh, kh, vh, lat, z, xbc, dt = _cd_inproj(xs, norm_mix[li], cw, mla_q_lat_gain[o], mla_kv_lat_gain[o],
                                                     cos_s, sin_s, tm=ns)
            oh = _mla_sample(qh, lat, cache_mla, o, page_table, cw, mla_k_gain[o])
            y2, hl, cbuf = _ssd_step(xbc, z, dt, jnp.swapaxes(state_ssm_conv[o], 0, 1), state_ssm_h[o],
                                     ssd_conv_w[o], ssd_conv_b[o], cw, ssd_norm_gain[o])
            xs = _post(xs, oh, y2, w_out, norm_ffn[li], w_ffn_in, w_ffn_out, tm=ns, th=sizes["th"])
            outs["ml_s"].append(lat.reshape(ns, 1, lat.shape[1]))
            outs["sh_s"].append(hl)
            outs["sc_s"].append(jnp.swapaxes(cbuf, 0, 1))
    st = {k: jnp.stack(v) for k, v in outs.items()}
    return (xp.reshape(nb, t, d), xs.reshape(ns, 1, d), st["sb_p"], st["sb_s"], st["lh_p"], st["lh_s"], st["lc_p"],
            st["lc_s"], st["ml_p"], st["ml_s"], st["sh_p"], st["sh_s"], st["sc_p"], st["sc_s"])
```

```python
import functools
import math

import jax
import jax.numpy as jnp
from jax import lax
from jax.experimental import pallas as pl
from jax.experimental.pallas import tpu as pltpu

F32 = jnp.float32
BF16 = jnp.bfloat16
NORM_EPS = 1e-6
LRU_C = 8.0
ROPE_THETA = 10000.0
CONV_WIDTH = 4
LANES = 128
SUBLANES = 8
VMEM_LIMIT_BYTES = 56 * 1024 * 1024
_NT = (((1,), (1,)), ((), ()))
_TN = (((0,), (0,)), ((), ()))


def _cparams(*sem):
    return pltpu.CompilerParams(dimension_semantics=sem, vmem_limit_bytes=VMEM_LIMIT_BYTES)


def _dot(a, b):
    return jnp.dot(a, b, preferred_element_type=F32)


def _split2(x):
    hi = lax.bitcast_convert_type(lax.bitcast_convert_type(x, jnp.uint32) & jnp.uint32(0xFFFF0000), F32)
    return hi.astype(BF16), (x - hi).astype(BF16)


def _split3(x):
    hi = x.astype(BF16)
    r = x - hi.astype(F32)
    mid = r.astype(BF16)
    lo = (r - mid.astype(F32)).astype(BF16)
    return hi, mid, lo


def _rms(x, g):
    return x * lax.rsqrt(jnp.mean(x * x, axis=-1, keepdims=True) + NORM_EPS) * g


def _sigmoid(x):
    return 1.0 / (1.0 + jnp.exp(-x))


def _silu(x):
    return x * _sigmoid(x)


def _softplus(x):
    return jnp.maximum(x, 0.0) + jnp.log(1.0 + jnp.exp(-jnp.abs(x)))


def _gelu_tanh(x):
    c = math.sqrt(2.0 / math.pi)
    return x * (0.5 * (1.0 + jnp.tanh(c * (x + 0.044715 * (x * x * x)))))


def _block_ones(n, blk, dtype=BF16):
    r = lax.broadcasted_iota(jnp.int32, (n, n), 0) // blk
    c = lax.broadcasted_iota(jnp.int32, (n, n), 1) // blk
    return (r == c).astype(dtype)


def _full(shape):
    nd = len(shape)
    return pl.BlockSpec(shape, lambda *_: (0,) * nd)


def _ab_inproj_kernel(x_ref, g_ref, w_ref, gq_ref, gk_ref, pq_ref, pk_ref,
                      qh_ref, kh_ref, vh_ref, kv_ref, xb_ref, gate_ref, *, nq, nk, hd):
    h = _rms(x_ref[...], g_ref[...]).astype(BF16)
    p = _dot(h, w_ref[...])
    dq, dk = nq * hd, nk * hd
    q = p[:, :dq]
    k = p[:, dq:dq + dk]
    v = p[:, dq + dk:dq + 2 * dk]

    def head_norm(t, ones_ref, gain):
        hi, lo = _split2(t * t)
        ms = (_dot(hi, ones_ref[...]) + _dot(lo, ones_ref[...])) * (1.0 / hd)
        return t * lax.rsqrt(ms + NORM_EPS) * gain

    qn = head_norm(q, pq_ref, gq_ref[...])
    kn = head_norm(k, pk_ref, gk_ref[...])
    for i in range(nq):
        qh_ref[i] = qn[:, i * hd:(i + 1) * hd].astype(BF16)
    for i in range(nk):
        kh_ref[i] = kn[:, i * hd:(i + 1) * hd].astype(BF16)
        vh_ref[i] = v[:, i * hd:(i + 1) * hd].astype(BF16)
    kv_ref[:, :dk] = kn
    kv_ref[:, dk:] = v
    w_lru = xb_ref.shape[-1]
    xb_ref[...] = p[:, dq + 2 * dk:dq + 2 * dk + w_lru]
    gate_ref[...] = p[:, dq + 2 * dk + w_lru:]


def _ab_inproj(x, g, w_bf, gq, gk, *, nq, nk, hd, w_lru, tm):
    m, d = x.shape
    n = w_bf.shape[1]
    dq, dk = nq * hd, nk * hd
    scale = -math.log2(math.e) * hd ** -0.5
    gq_t = (jnp.tile(gq, nq) * scale).reshape(1, dq)
    gk_t = jnp.tile(gk, nk).reshape(1, dk)
    row = lambda i: (i, 0)
    head = lambda i: (0, i, 0)
    return pl.pallas_call(
        functools.partial(_ab_inproj_kernel, nq=nq, nk=nk, hd=hd),
        grid=(m // tm,),
        in_specs=[pl.BlockSpec((tm, d), row), _full((1, d)), _full((d, n)), _full((1, dq)), _full((1, dk)),
                  _full((dq, dq)), _full((dk, dk))],
        out_specs=[pl.BlockSpec((nq, tm, hd), head), pl.BlockSpec((nk, tm, hd), head),
                   pl.BlockSpec((nk, tm, hd), head), pl.BlockSpec((tm, 2 * dk), row),
                   pl.BlockSpec((tm, w_lru), row), pl.BlockSpec((tm, w_lru), row)],
        out_shape=[jax.ShapeDtypeStruct((nq, m, hd), BF16), jax.ShapeDtypeStruct((nk, m, hd), BF16),
                   jax.ShapeDtypeStruct((nk, m, hd), BF16), jax.ShapeDtypeStruct((m, 2 * dk), F32),
                   jax.ShapeDtypeStruct((m, w_lru), F32), jax.ShapeDtypeStruct((m, w_lru), F32)],
        compiler_params=_cparams("parallel"),
    )(x, g.reshape(1, d), w_bf, gq_t, gk_t, _block_ones(dq, hd), _block_ones(dk, hd))


def _neg_abs(x):
    bits = lax.bitcast_convert_type(x, jnp.uint32) | jnp.uint32(0x80000000)
    return lax.bitcast_convert_type(bits, F32)


def _sb_log2_keep(nz):
    return jnp.minimum(nz, 0.0) - jnp.log2(1.0 + jnp.exp2(_neg_abs(nz)))


_SB_DIAG, _SB_FULL, _SB_IDLE = 0, 1, 2
_SB_DEPTH = 4
_MASKED = -1e30


def _sb_prompt_kernel(qi_ref, kb_ref, kind_ref, q_ref, k_ref, v_ref, uu_ref, bias_ref, o_ref,
                      nz_ring, x_ring, tail_ring, w_ring, r_scr, acc_scr, *, tq, grp, hd, n_steps):
    rows = grp * tq

    def blk(idx):
        return pl.ds(pl.multiple_of(idx * tq, tq), tq)

    for ring in (x_ring, tail_ring, w_ring, r_scr, acc_scr):
        ring[...] = jnp.zeros_like(ring)
    nz_ring[...] = jnp.full(nz_ring.shape, -_MASKED, F32)

    def step(t, ph):
        kind, qi = kind_ref[t], qi_ref[t]
        v = v_ref[blk(kb_ref[t]), :]
        carry_on = jnp.where(kind == _SB_DIAG, 0.0, 1.0)
        outs = [_dot(w_ring[ph % 2, g * tq:(g + 1) * tq, :], v) for g in range(grp)]
        for g in range(grp):
            acc = acc_scr[g * tq:(g + 1) * tq, :] * carry_on + outs[g]
            acc_scr[g * tq:(g + 1) * tq, :] = acc
            o_ref[g, blk(qi), :] = acc.astype(o_ref.dtype)
        kind = kind_ref[t + 1]
        tail = tail_ring[(ph + 1) % 2] + r_scr[...] * jnp.where(kind == _SB_DIAG, 0.0, 1.0)
        w = jnp.exp2(tail - nz_ring[(ph + 1) % 4])
        w_ring[(ph + 1) % 2] = w.astype(BF16)
        r_scr[...] = tail[:, 0:1]
        uu = uu_ref[...]
        tails = [_dot(x_ring[ph % 2, g * tq:(g + 1) * tq, :], uu) for g in range(grp)]
        for g in range(grp):
            tail_ring[ph % 2, g * tq:(g + 1) * tq, :] = tails[g]
        hi, lo = _split2(_sb_log2_keep(nz_ring[(ph + 3) % 4]))
        x_ring[(ph + 1) % 2, :, :tq] = hi
        x_ring[(ph + 1) % 2, :, tq:] = lo
        k = k_ref[blk(kb_ref[t + 4]), :]
        qrows = blk(qi_ref[t + 4])
        nzs = [lax.dot_general(q_ref[g, qrows, :], k, _NT, preferred_element_type=F32) for g in range(grp)]
        for g in range(grp):
            nz_ring[ph % 4, g * tq:(g + 1) * tq, :] = nzs[g] + bias_ref[kind_ref[t + 4], g * tq:(g + 1) * tq, :]

    def four_steps(j, carry):
        for ph in range(4):
            step(4 * j + ph, ph)
        return carry

    lax.fori_loop(0, n_steps // 4, four_steps, 0)


def _rev_cumsum_ones(n):
    s = lax.broadcasted_iota(jnp.int32, (n, n), 0)
    j = lax.broadcasted_iota(jnp.int32, (n, n), 1)
    u = (s >= j).astype(BF16)
    return jnp.concatenate([u, u], axis=0)


def _sb_prompt(qh, kh, vh, *, nb, tq):
    nq, m, hd = qh.shape
    nk = kh.shape[0]
    grp = nq // nk
    t = m // nb
    n_blk = t // tq
    rows = grp * tq
    items = [(i, i - s, _SB_DIAG if s == 0 else _SB_FULL) for i in range(n_blk) for s in range(i + 1)]
    n_steps = -(-(len(items) + _SB_DEPTH) // 4) * 4
    items = [(0, 0, _SB_IDLE)] * _SB_DEPTH + items
    items = items + [(n_blk - 1, 0, _SB_IDLE)] * (n_steps + _SB_DEPTH - len(items))
    qi_tab, kb_tab, kind_tab = (jnp.asarray([it[c] for it in items], jnp.int32) for c in range(3))
    t_idx = lax.broadcasted_iota(jnp.int32, (grp, tq, tq), 1).reshape(rows, tq)
    earlier = lax.broadcasted_iota(jnp.int32, (rows, tq), 1) < t_idx
    bias = jnp.stack([jnp.where(earlier, 0.0, -_MASKED), jnp.zeros((rows, tq), F32),
                      jnp.full((rows, tq), -_MASKED, F32)])
    per_seq = lambda b, h, *_: (h, b, 0)
    const = lambda nd: (lambda b, h, *_: (0,) * nd)
    grid_spec = pltpu.PrefetchScalarGridSpec(
        num_scalar_prefetch=3, grid=(nb, nk),
        in_specs=[pl.BlockSpec((grp, t, hd), per_seq), pl.BlockSpec((None, t, hd), per_seq),
                  pl.BlockSpec((None, t, hd), per_seq), pl.BlockSpec((2 * tq, tq), const(2)),
                  pl.BlockSpec((3, rows, tq), const(3))],
        out_specs=pl.BlockSpec((grp, t, hd), per_seq),
        scratch_shapes=[pltpu.VMEM((4, rows, tq), F32), pltpu.VMEM((2, rows, 2 * tq), BF16),
                        pltpu.VMEM((2, rows, tq), F32), pltpu.VMEM((2, rows, tq), BF16),
                        pltpu.VMEM((rows, 1), F32), pltpu.VMEM((rows, hd), F32)])
    return pl.pallas_call(
        functools.partial(_sb_prompt_kernel, tq=tq, grp=grp, hd=hd, n_steps=n_steps),
        grid_spec=grid_spec,
        out_shape=jax.ShapeDtypeStruct((nq, m, hd), BF16),
        compiler_params=_cparams("parallel", "parallel"),
        name="sb_prompt",
    )(qi_tab, kb_tab, kind_tab, qh, kh, vh, _rev_cumsum_ones(tq), bias)


def _lru_gates(xc, wbd_ref, bias_ref, lam_ref):
    w = xc.shape[-1]
    ra = _dot(xc.astype(BF16), wbd_ref[...]) + bias_ref[...]
    r = _sigmoid(ra[:, :w])
    ig = _sigmoid(ra[:, w:])
    log_a = (-LRU_C) * r * _softplus(-lam_ref[...])
    a = jnp.exp(log_a)
    b = jnp.sqrt(-jnp.tanh(log_a) * (a * a + 1.0)) * (ig * xc)
    return a, b


def _rglru_prompt_kernel(xb_ref, gate_ref, cw_ref, cb_ref, wbd_ref, bias_ref, lam_ref,
                         y_ref, hlast_ref, cbuf_ref, xpad, a_scr, b_scr, h_scr, *, tt):
    t = pl.program_id(1)
    pad = 8

    @pl.when(t == 0)
    def _():
        h_scr[...] = jnp.zeros_like(h_scr)
        xpad[0:pad, :] = jnp.zeros((pad, xpad.shape[1]), F32)

    xpad[pad:pad + tt, :] = xb_ref[...]
    xc = cb_ref[...]
    for i in range(CONV_WIDTH):
        o = pad - (CONV_WIDTH - 1) + i
        xc = xc + xpad[o:o + tt, :] * cw_ref[i:i + 1, :]
    cbuf_ref[...] = xpad[pad + tt - (CONV_WIDTH - 1):pad + tt, :]
    xpad[0:pad, :] = xpad[tt:tt + pad, :]
    a, b = _lru_gates(xc, wbd_ref, bias_ref, lam_ref)
    a_scr[...] = a
    b_scr[...] = b

    def body(r, h):
        h = a_scr[pl.ds(r, 1), :] * h + b_scr[pl.ds(r, 1), :]
        b_scr[pl.ds(r, 1), :] = h
        return h

    h = lax.fori_loop(0, tt, body, h_scr[...], unroll=8)
    h_scr[...] = h
    hlast_ref[...] = h
    y_ref[...] = (_gelu_tanh(gate_ref[...]) * b_scr[...]).astype(y_ref.dtype)


def _lru_weights(wa, ba, wx, bx):
    nblk, bd, _ = wa.shape
    w = nblk * bd
    eye = jnp.eye(nblk, dtype=F32)

    def bdiag(m):
        return jnp.einsum('kij,kl->kilj', m, eye).reshape(w, w)

    wbd = jnp.concatenate([bdiag(wa), bdiag(wx)], axis=1).astype(BF16)
    bias = jnp.concatenate([ba, bx]).reshape(1, 2 * w)
    return wbd, bias


def _rglru_prompt(xb, gate, conv_w, conv_b, wbd, bias, lam, *, nb, tt):
    m, w = xb.shape
    t = m // nb
    nt = t // tt
    row = lambda b, i: (b * nt + i, 0)
    return pl.pallas_call(
        functools.partial(_rglru_prompt_kernel, tt=tt),
        grid=(nb, nt),
        in_specs=[pl.BlockSpec((tt, w), row), pl.BlockSpec((tt, w), row), _full((CONV_WIDTH, w)),
                  _full((1, w)), _full((w, 2 * w)), _full((1, 2 * w)), _full((1, w))],
        out_specs=[pl.BlockSpec((tt, w), row), pl.BlockSpec((None, 1, w), lambda b, i: (b, 0, 0)),
                   pl.BlockSpec((None, CONV_WIDTH - 1, w), lambda b, i: (b, 0, 0))],
        out_shape=[jax.ShapeDtypeStruct((m, w), BF16), jax.ShapeDtypeStruct((nb, 1, w), F32),
                   jax.ShapeDtypeStruct((nb, CONV_WIDTH - 1, w), F32)],
        scratch_shapes=[pltpu.VMEM((tt + 8, w), F32), pltpu.VMEM((tt, w), F32), pltpu.VMEM((tt, w), F32),
                        pltpu.VMEM((1, w), F32)],
        compiler_params=_cparams("parallel", "arbitrary"),
    )(xb, gate, conv_w, conv_b.reshape(1, w), wbd, bias, lam.reshape(1, w))


def _post_kernel(x_ref, oh_ref, y2_ref, wo_ref, g_ref, wu_ref, wg_ref, wd_ref, out_ref,
                 x1_scr, hb_scr, acc_scr):
    j = pl.program_id(1)
    nh, _, hd = oh_ref.shape

    @pl.when(j == 0)
    def _():
        heads = jnp.concatenate([oh_ref[h] for h in range(nh)], axis=1)
        mix = _dot(y2_ref[...], wo_ref[nh * hd:, :]) + _dot(heads, wo_ref[:nh * hd, :])
        x1 = x_ref[...] + mix
        x1_scr[...] = x1
        hb_scr[...] = _rms(x1, g_ref[...]).astype(BF16)
        acc_scr[...] = jnp.zeros_like(acc_scr)

    hb = hb_scr[...]
    u = _dot(hb, wu_ref[...])
    gg = _dot(hb, wg_ref[...])
    acc_scr[...] += _dot((_silu(u) * gg).astype(BF16), wd_ref[...])

    @pl.when(j == pl.num_programs(1) - 1)
    def _():
        out_ref[...] = x1_scr[...] + acc_scr[...]


def _post(x, oh, y2, w_out_bf, g, w_in_bf, w_dn_bf, *, tm, th):
    m, d = x.shape
    nh, _, hd = oh.shape
    hid = w_dn_bf.shape[0]
    nj = hid // th
    return pl.pallas_call(
        _post_kernel,
        grid=(m // tm, nj),
        in_specs=[pl.BlockSpec((tm, d), lambda i, j: (i, 0)),
                  pl.BlockSpec((nh, tm, hd), lambda i, j: (0, i, 0)),
                  pl.BlockSpec((tm, y2.shape[1]), lambda i, j: (i, 0)),
                  pl.BlockSpec(w_out_bf.shape, lambda i, j: (0, 0)),
                  pl.BlockSpec((1, d), lambda i, j: (0, 0)),
                  pl.BlockSpec((d, th), lambda i, j: (0, j)),
                  pl.BlockSpec((d, th), lambda i, j: (0, nj + j)),
                  pl.BlockSpec((th, d), lambda i, j: (j, 0))],
        out_specs=pl.BlockSpec((tm, d), lambda i, j: (i, 0)),
        out_shape=jax.ShapeDtypeStruct((m, d), F32),
        scratch_shapes=[pltpu.VMEM((tm, d), F32), pltpu.VMEM((tm, d), BF16), pltpu.VMEM((tm, d), F32)],
        compiler_params=_cparams("parallel", "arbitrary"),
    )(x, oh, y2, w_out_bf, g.reshape(1, d), w_in_bf, w_in_bf, w_dn_bf)


def _rope_tables(pos, nope, rope):
    half = rope // 2
    inv = ROPE_THETA ** (-jnp.arange(half, dtype=F32) / half)
    ang = pos.astype(F32)[:, None] * inv[None, :]
    n = pos.shape[0]
    lead = jnp.zeros((n, nope), F32)
    trail = jnp.zeros((n, LANES - nope - rope), F32)
    cos = jnp.concatenate([lead, jnp.cos(ang), jnp.cos(ang), trail], axis=1)
    sin = jnp.concatenate([lead, jnp.sin(ang), jnp.sin(ang), trail], axis=1)
    return cos, sin


def _rot_partner(w):
    half = w.shape[-1] // 2
    return jnp.concatenate([-w[..., half:], w[..., :half]], axis=-1)


def _cd_weights(w_in, w_uq, w_ukv, gq, gk, dt_bias, a_log, d_skip, *, kv_rank, q_rank, rope, inner, conv_dim):
    d = w_in.shape[0]
    nh, qk = w_uq.shape[1], w_uq.shape[2]
    nope = qk - rope
    vd = w_ukv.shape[2] - nope
    n_ssd = dt_bias.shape[0]
    assert qk <= LANES and kv_rank == LANES and n_ssd <= LANES
    o = 0
    cq = w_in[:, o:o + q_rank]; o += q_rank
    ckv = w_in[:, o:o + kv_rank]; o += kv_rank
    kpe = w_in[:, o:o + rope]; o += rope
    z = w_in[:, o:o + inner]; o += inner
    xbc = w_in[:, o:o + conv_dim]; o += conv_dim
    dt = w_in[:, o:o + n_ssd]

    def on_rope_lanes(w):
        return jnp.concatenate([jnp.zeros((d, nope), F32), w, jnp.zeros((d, LANES - qk), F32)], axis=1)

    w_pad = jnp.concatenate([cq, ckv, on_rope_lanes(kpe), on_rope_lanes(_rot_partner(kpe)), z, xbc,
                             dt, jnp.zeros((d, LANES - n_ssd), F32)], axis=1).astype(BF16)
    zq = jnp.zeros((q_rank, nh, LANES - qk), F32)
    wqm = jnp.concatenate([w_uq, zq], axis=2).reshape(q_rank, nh * LANES).astype(BF16)
    wqp = jnp.concatenate([jnp.zeros((q_rank, nh, nope), F32), _rot_partner(w_uq[:, :, nope:]), zq],
                          axis=2).reshape(q_rank, nh * LANES).astype(BF16)
    wk = w_ukv[:, :, :nope]
    wk128 = jnp.concatenate([wk, jnp.zeros((kv_rank, nh, LANES - nope), F32)], axis=2)
    pad1 = lambda v, fill=0.0: jnp.concatenate([v, jnp.full((LANES - v.shape[0],), fill, F32)]).reshape(1, LANES)
    return dict(
        w_pad=w_pad, wqm=wqm, wqp=wqp,
        wk128=wk128.reshape(kv_rank, nh * LANES).astype(BF16),
        wkt=wk.reshape(kv_rank, nh * nope).T.astype(BF16),
        wv=w_ukv[:, :, nope:].reshape(kv_rank, nh * vd).astype(BF16),
        gq128=pad1(gq * (math.log2(math.e) * qk ** -0.5)),
        gk128=pad1(gk), dtb128=pad1(dt_bias), alog128=pad1(a_log),
        dsk=jnp.repeat(d_skip, inner // n_ssd).reshape(1, inner),
        nh=nh, qk=qk, nope=nope, rope=rope, vd=vd, q_rank=q_rank, kv_rank=kv_rank, inner=inner,
        conv_dim=conv_dim, n_ssd=n_ssd)


def _cd_inproj_kernel(x_ref, g_ref, w_ref, gql_ref, gkl_ref, wqm_ref, wqp_ref, wk_ref, wv_ref, gq_ref, gk_ref,
                      cos_ref, sin_ref, qh_ref, kh_ref, vh_ref, lat_ref, z_ref, xbc_ref, dt_ref,
                      *, nh, qk, nope, rope, vd, q_rank, kv_rank, inner, conv_dim):
    h = _rms(x_ref[...], g_ref[...]).astype(BF16)
    p = _dot(h, w_ref[...])
    o = 0
    cq = p[:, o:o + q_rank]; o += q_rank
    ckv = p[:, o:o + kv_rank]; o += kv_rank
    kpe = p[:, o:o + LANES]; o += LANES
    kpe_rot = p[:, o:o + LANES]; o += LANES
    z_ref[...] = p[:, o:o + inner]; o += inner
    xbc_ref[...] = p[:, o:o + conv_dim]; o += conv_dim
    dt_ref[...] = p[:, o:o + LANES]

    cos_t = cos_ref[...]
    sin_t = sin_ref[...]
    lane = lax.broadcasted_iota(jnp.int32, (1, LANES), 1)
    cos_q = cos_t + (lane < nope).astype(F32)
    cqn = _rms(cq, gql_ref[...]).astype(BF16)
    qm = _dot(cqn, wqm_ref[...])
    qp = _dot(cqn, wqp_ref[...])
    for i in range(nh):
        qi = qm[:, i * LANES:(i + 1) * LANES] * cos_q + qp[:, i * LANES:(i + 1) * LANES] * sin_t
        ms = jnp.sum(qi * qi, axis=-1, keepdims=True) * (1.0 / qk)
        qh_ref[i] = (qi * lax.rsqrt(ms + NORM_EPS) * gq_ref[...]).astype(BF16)

    ckvn = _rms(ckv, gkl_ref[...])
    kper = kpe * cos_t + kpe_rot * sin_t
    lat_ref[:, :kv_rank] = ckvn
    lat_ref[:, kv_rank:] = kper[:, nope:nope + rope]
    cb = ckvn.astype(BF16)
    kn = _dot(cb, wk_ref[...])
    vv = _dot(cb, wv_ref[...])
    for i in range(nh):
        ki = kn[:, i * LANES:(i + 1) * LANES] + kper
        ms = jnp.sum(ki * ki, axis=-1, keepdims=True) * (1.0 / qk)
        kh_ref[i] = (ki * lax.rsqrt(ms + NORM_EPS) * gk_ref[...]).astype(BF16)
        vh_ref[i] = vv[:, i * vd:(i + 1) * vd].astype(BF16)


def _cd_inproj(x, g, cw, gql, gkl, cos, sin, *, tm):
    m, d = x.shape
    nh, vd, kv_rank, rope = cw["nh"], cw["vd"], cw["kv_rank"], cw["rope"]
    inner, conv_dim, q_rank = cw["inner"], cw["conv_dim"], cw["q_rank"]
    n_pos = cos.shape[0] // tm
    row = lambda i: (i, 0)
    head = lambda i: (0, i, 0)
    dims = {k: cw[k] for k in ("nh", "qk", "nope", "rope", "vd", "q_rank", "kv_rank", "inner", "conv_dim")}
    return pl.pallas_call(
        functools.partial(_cd_inproj_kernel, **dims),
        grid=(m // tm,),
        in_specs=[pl.BlockSpec((tm, d), row), _full((1, d)), _full(cw["w_pad"].shape), _full((1, q_rank)),
                  _full((1, kv_rank)), _full(cw["wqm"].shape), _full(cw["wqp"].shape), _full(cw["wk128"].shape),
                  _full(cw["wv"].shape), _full((1, LANES)), _full((1, LANES)),
                  pl.BlockSpec((tm, LANES), lambda i: (i % n_pos, 0)),
                  pl.BlockSpec((tm, LANES), lambda i: (i % n_pos, 0))],
        out_specs=[pl.BlockSpec((nh, tm, LANES), head), pl.BlockSpec((nh, tm, LANES), head),
                   pl.BlockSpec((nh, tm, vd), head), pl.BlockSpec((tm, kv_rank + rope), row),
                   pl.BlockSpec((tm, inner), row), pl.BlockSpec((tm, conv_dim), row),
                   pl.BlockSpec((tm, LANES), row)],
        out_shape=[jax.ShapeDtypeStruct((nh, m, LANES), BF16), jax.ShapeDtypeStruct((nh, m, LANES), BF16),
                   jax.ShapeDtypeStruct((nh, m, vd), BF16), jax.ShapeDtypeStruct((m, kv_rank + rope), F32),
                   jax.ShapeDtypeStruct((m, inner), F32), jax.ShapeDtypeStruct((m, conv_dim), F32),
                   jax.ShapeDtypeStruct((m, LANES), F32)],
        compiler_params=_cparams("parallel"),
    )(x, g.reshape(1, d), cw["w_pad"], gql.reshape(1, q_rank), gkl.reshape(1, kv_rank), cw["wqm"], cw["wqp"],
      cw["wk128"], cw["wv"], cw["gq128"], cw["gk128"], cos, sin)


_MLA_DEPTH = 3


def _mla_prompt_kernel(qi_ref, kb_ref, kind_ref, q_ref, k_ref, v_ref, bias_ref, o_ref,
                       s_ring, p_ring, pv_ring, a_ring, linv_ring, m_scr, l_scr, acc_scr, *, tq, n_steps):
    halves = [slice(c * (tq // 2), (c + 1) * (tq // 2)) for c in range(2)]
    vd = v_ref.shape[-1]

    def blk(idx):
        return pl.ds(pl.multiple_of(idx * tq, tq), tq)

    for ring in (s_ring, p_ring, pv_ring, a_ring, linv_ring, l_scr, acc_scr):
        ring[...] = jnp.zeros_like(ring)
    m_scr[...] = jnp.full(m_scr.shape, -jnp.inf, F32)

    def step(t, ph):
        acc = a_ring[(ph + 1) % 4, :, :vd] * acc_scr[...] + pv_ring[(ph + 1) % 2]
        acc_scr[...] = acc
        o_ref[blk(qi_ref[t]), :] = (acc * linv_ring[(ph + 1) % 4, :, :vd]).astype(o_ref.dtype)
        v = v_ref[blk(kb_ref[t + 1]), :]
        outs = [_dot(p_ring[ph % 2, h, :], v) for h in halves]
        for h, o in zip(halves, outs):
            pv_ring[ph % 2, h, :] = o
        kind = kind_ref[t + 2]
        tiles = [slice(c * LANES, (c + 1) * LANES) for c in range(tq // LANES)]
        s = [s_ring[(ph + 1) % 2, :, c] + bias_ref[kind, :, c] for c in tiles]
        m_prev = jnp.where(kind == _SB_DIAG, -jnp.inf, m_scr[...])
        m_blk = jnp.max(functools.reduce(jnp.maximum, s), axis=-1, keepdims=True)
        m_new = jnp.maximum(m_prev, jnp.broadcast_to(m_blk, m_prev.shape))
        alpha = jnp.exp2(m_prev - m_new)
        p = [jnp.exp2(sc - m_new) for sc in s]
        l_blk = jnp.sum(functools.reduce(jnp.add, p), axis=-1, keepdims=True)
        l = alpha * l_scr[...] + jnp.broadcast_to(l_blk, m_prev.shape)
        m_scr[...] = m_new
        l_scr[...] = l
        a_ring[(ph + 3) % 4] = alpha
        linv_ring[(ph + 3) % 4] = 1.0 / l
        for c, pc in zip(tiles, p):
            p_ring[(ph + 1) % 2, :, c] = pc.astype(BF16)
        k = k_ref[blk(kb_ref[t + 3]), :]
        qrows = blk(qi_ref[t + 3])
        ss = [lax.dot_general(q_ref[qrows, :][h], k, _NT, preferred_element_type=F32) for h in halves]
        for h, sc in zip(halves, ss):
            s_ring[ph % 2, h, :] = sc

    def four_steps(j, carry):
        for ph in range(4):
            step(4 * j + ph, ph)
        return carry

    lax.fori_loop(0, n_steps // 4, four_steps, 0)


def _mla_prompt(qh, kh, vh, *, nb, tq):
    nh, m, dk = qh.shape
    vd = vh.shape[-1]
    t = m // nb
    n_blk = t // tq
    items = [(i, i if s == 0 else s - 1, _SB_DIAG if s == 0 else _SB_FULL) for i in range(n_blk) for s in range(i + 1)]
    n_steps = -(-(len(items) + _MLA_DEPTH) // 4) * 4
    items = [(0, 0, _SB_IDLE)] * _MLA_DEPTH + items
    items = items + [(n_blk - 1, 0, _SB_IDLE)] * (n_steps + _MLA_DEPTH - len(items))
    qi_tab, kb_tab, kind_tab = (jnp.asarray([it[c] for it in items], jnp.int32) for c in range(3))
    causal = lax.broadcasted_iota(jnp.int32, (tq, tq), 1) <= lax.broadcasted_iota(jnp.int32, (tq, tq), 0)
    bias = jnp.stack([jnp.where(causal, 0.0, _MASKED), jnp.zeros((tq, tq), F32), jnp.full((tq, tq), _MASKED, F32)])
    per_seq = lambda b, h, *_: (h, b, 0)
    grid_spec = pltpu.PrefetchScalarGridSpec(
        num_scalar_prefetch=3, grid=(nb, nh),
        in_specs=[pl.BlockSpec((None, t, dk), per_seq), pl.BlockSpec((None, t, dk), per_seq),
                  pl.BlockSpec((None, t, vd), per_seq), pl.BlockSpec((3, tq, tq), lambda b, h, *_: (0, 0, 0))],
        out_specs=pl.BlockSpec((None, t, vd), per_seq),
        scratch_shapes=[pltpu.VMEM((2, tq, tq), F32), pltpu.VMEM((2, tq, tq), BF16), pltpu.VMEM((2, tq, vd), F32),
                        pltpu.VMEM((4, tq, LANES), F32), pltpu.VMEM((4, tq, LANES), F32), pltpu.VMEM((tq, LANES), F32),
                        pltpu.VMEM((tq, LANES), F32), pltpu.VMEM((tq, vd), F32)])
    return pl.pallas_call(
        functools.partial(_mla_prompt_kernel, tq=tq, n_steps=n_steps),
        grid_spec=grid_spec,
        out_shape=jax.ShapeDtypeStruct((nh, m, vd), BF16),
        compiler_params=_cparams("parallel", "parallel"),
        name="mla_prompt",
    )(qi_tab, kb_tab, kind_tab, qh, kh, vh, bias)


def _causal_conv_tile(x_ref, xpad, cw_ref, cb_ref, cbuf_ref, rows):
    pad = 8
    xpad[pad:pad + rows, :] = x_ref[...]
    xc = cb_ref[...]
    for i in range(CONV_WIDTH):
        o = pad - (CONV_WIDTH - 1) + i
        xc = xc + xpad[o:o + rows, :] * cw_ref[i:i + 1, :]
    cbuf_ref[...] = xpad[pad + rows - (CONV_WIDTH - 1):pad + rows, :]
    xpad[0:pad, :] = xpad[rows:rows + pad, :]
    return xc


def _gated_group_norm(y, z, ng_ref, y_ref, n_groups):
    y = y * _silu(z)
    gs = y.shape[-1] // n_groups
    for g in range(n_groups):
        y_ref[:, g * gs:(g + 1) * gs] = _rms(y[:, g * gs:(g + 1) * gs], ng_ref[:, g * gs:(g + 1) * gs]).astype(y_ref.dtype)


def _ssd_prompt_kernel(xbc_ref, z_ref, dt_ref, cw_ref, cb_ref, dtb_ref, alog_ref, dsk_ref, ng_ref, ltri_ref,
                       y_ref, hlast_ref, cbuf_ref, xpad, state, yscr, *, q, nh, hd, ns, n_groups):
    c = pl.program_id(1)

    @pl.when(c == 0)
    def _():
        state[...] = jnp.zeros_like(state)
        xpad[0:8, :] = jnp.zeros((8, xpad.shape[1]), F32)

    xc = _silu(_causal_conv_tile(xbc_ref, xpad, cw_ref, cb_ref, cbuf_ref, q))
    inner = nh * hd
    hpg = nh // n_groups
    xs = xc[:, :inner]
    bm = xc[:, inner:inner + n_groups * ns]
    cm = xc[:, inner + n_groups * ns:]
    lane = lax.broadcasted_iota(jnp.int32, (1, LANES), 1)
    dtv = _softplus(dt_ref[...] + dtb_ref[...])
    a = jnp.where(lane < nh, -jnp.exp(alog_ref[...]), 0.0)
    ltri = ltri_ref[...]
    cum = sum(_dot(ltri, part) for part in _split3(dtv * a))
    cum_t = cum.T
    dt_t = dtv.T
    cum_last = cum[q - 1:q, :]
    to_end = jnp.exp(cum_last - cum) * dtv
    ecum = jnp.exp(cum)
    elast = jnp.exp(cum_last)
    causal = lax.broadcasted_iota(jnp.int32, (q, q), 0) >= lax.broadcasted_iota(jnp.int32, (q, q), 1)
    for g in range(n_groups):
        cmg = cm[:, g * ns:(g + 1) * ns].astype(BF16)
        bmg = bm[:, g * ns:(g + 1) * ns].astype(BF16)
        cb = lax.dot_general(cmg, bmg, _NT, preferred_element_type=F32)
        for r in range(g * hpg, (g + 1) * hpg):
            seg = cum[:, r:r + 1] - cum_t[r:r + 1, :]
            decay = jnp.exp(jnp.where(causal, seg, -jnp.inf))
            mix = cb * decay * dt_t[r:r + 1, :]
            xh = xs[:, r * hd:(r + 1) * hd]
            hprev = state[r]
            y = _dot(mix.astype(BF16), xh.astype(BF16))
            y = y + lax.dot_general(cmg, hprev.astype(BF16), _NT, preferred_element_type=F32) * ecum[:, r:r + 1]
            yscr[:, r * hd:(r + 1) * hd] = y + dsk_ref[:, r * hd:(r + 1) * hd] * xh
            xw = (xh * to_end[:, r:r + 1]).astype(BF16)
            state[r] = elast[:, r:r + 1] * hprev + lax.dot_general(xw, bmg, _TN, preferred_element_type=F32)
    hlast_ref[...] = state[...]
    _gated_group_norm(yscr[...], z_ref[...], ng_ref, y_ref, n_groups)


def _lower_tri_ones(n):
    r = lax.broadcasted_iota(jnp.int32, (n, n), 0)
    c = lax.broadcasted_iota(jnp.int32, (n, n), 1)
    return (r >= c).astype(BF16)


def _ssd_prompt(xbc, z, dt, conv_w, conv_b, cw, norm_g, *, nb, q, state_dim):
    m, conv_dim = xbc.shape
    inner, nh = cw["inner"], cw["n_ssd"]
    hd = inner // nh
    n_groups = (conv_dim - inner) // (2 * state_dim)
    t = m // nb
    nc = t // q
    row = lambda b, c: (b * nc + c, 0)
    return pl.pallas_call(
        functools.partial(_ssd_prompt_kernel, q=q, nh=nh, hd=hd, ns=state_dim, n_groups=n_groups),
        grid=(nb, nc),
        in_specs=[pl.BlockSpec((q, conv_dim), row), pl.BlockSpec((q, inner), row), pl.BlockSpec((q, LANES), row),
                  _full((CONV_WIDTH, conv_dim)), _full((1, conv_dim)), _full((1, LANES)), _full((1, LANES)),
                  _full((1, inner)), _full((1, inner)), _full((q, q))],
        out_specs=[pl.BlockSpec((q, inner), row),
                   pl.BlockSpec((None, nh, hd, state_dim), lambda b, c: (b, 0, 0, 0)),
                   pl.BlockSpec((None, CONV_WIDTH - 1, conv_dim), lambda b, c: (b, 0, 0))],
        out_shape=[jax.ShapeDtypeStruct((m, inner), BF16), jax.ShapeDtypeStruct((nb, nh, hd, state_dim), F32),
                   jax.ShapeDtypeStruct((nb, CONV_WIDTH - 1, conv_dim), F32)],
        scratch_shapes=[pltpu.VMEM((q + 8, conv_dim), F32), pltpu.VMEM((nh, hd, state_dim), F32),
                        pltpu.VMEM((q, inner), F32)],
        compiler_params=_cparams("parallel", "arbitrary"),
    )(xbc, z, dt, conv_w, conv_b.reshape(1, conv_dim), cw["dtb128"], cw["alog128"], cw["dsk"],
      norm_g.reshape(1, inner), _lower_tri_ones(q))


def _sb_sample_kernel(pt_ref, qbd_ref, uu_ref, *rest, npg, dk, grp):
    pages = rest[:npg]
    o_ref, z_scr, r_scr, acc_scr, wb_scr = rest[npg:]
    c = pl.program_id(1)
    nrow = qbd_ref.shape[0]
    n_kv = nrow // grp
    hd = dk // n_kv
    keys = z_scr.shape[1]

    @pl.when(c == 0)
    def _():
        r_scr[...] = jnp.zeros_like(r_scr)
        acc_scr[...] = jnp.zeros_like(acc_scr)

    qbd = qbd_ref[...]
    for j in range(npg):
        z_scr[j * nrow:(j + 1) * nrow, :] = _dot(qbd, pages[j][:dk, :].astype(BF16))
    nz = z_scr[...]
    hi, lo = _split2(_sb_log2_keep(nz))
    tl = _dot(jnp.concatenate([hi, lo], axis=1), uu_ref[...])
    tot = jnp.broadcast_to(tl[:, 0:1], tl.shape)
    r = r_scr[...]
    carries = [None] * npg
    for j in reversed(range(npg)):
        carries[j] = r
        r = r + tot[j * nrow:(j + 1) * nrow, :]
    r_scr[...] = r
    z_scr[...] = jnp.exp2(tl + jnp.concatenate(carries, axis=0) - nz)
    sub = wb_scr.shape[1]
    for r in range(npg * nrow):
        wb_scr[r] = jnp.broadcast_to(z_scr[r:r + 1, :], (sub, keys))
    for rg in range(dk // sub):
        h = rg * sub // hd
        rows = slice(rg * sub, (rg + 1) * sub)
        acc = [acc_scr[g, rows, :] for g in range(grp)]
        for j in range(npg):
            vt = pages[j][dk + rg * sub:dk + (rg + 1) * sub, :]
            for g in range(grp):
                acc[g] = acc[g] + vt * wb_scr[j * nrow + g * n_kv + h]
        for g in range(grp):
            acc_scr[g, rows, :] = acc[g]

    @pl.when(c == pl.num_programs(1) - 1)
    def _():
        for g in range(grp):
            o_ref[g:g + 1, :] = jnp.sum(acc_scr[g].T, axis=0, keepdims=True)


def _sb_sample(qh, cache_t, layer, page_table, *, n_kv, npg=32):
    n_q, s, hd = qh.shape
    grp = n_q // n_kv
    dk = n_kv * hd
    page = cache_t.shape[3]
    n_pages = page_table.shape[1]
    npg = min(npg, n_pages)
    nch = n_pages // npg
    eye = jnp.eye(n_kv, dtype=qh.dtype)
    qbd = jnp.einsum('hgsd,hk->sghkd', qh.reshape(n_kv, grp, s, hd), eye).reshape(s, n_q, dk)

    def page_spec(j):
        return pl.BlockSpec((None, None, 2 * dk, page),
                            lambda i, c, pt: (layer, pt[i, (nch - 1 - c) * npg + j], 0, 0))

    grid_spec = pltpu.PrefetchScalarGridSpec(
        num_scalar_prefetch=1, grid=(s, nch),
        in_specs=[pl.BlockSpec((None, n_q, dk), lambda i, c, pt: (i, 0, 0)),
                  pl.BlockSpec((2 * page, page), lambda i, c, pt: (0, 0))] + [page_spec(j) for j in range(npg)],
        out_specs=pl.BlockSpec((None, grp, dk), lambda i, c, pt: (i, 0, 0)),
        scratch_shapes=[pltpu.VMEM((npg * n_q, page), F32), pltpu.VMEM((n_q, page), F32),
                        pltpu.VMEM((grp, dk, page), F32), pltpu.VMEM((npg * n_q, SUBLANES, page), F32)])
    og = pl.pallas_call(
        functools.partial(_sb_sample_kernel, npg=npg, dk=dk, grp=grp),
        grid_spec=grid_spec,
        out_shape=jax.ShapeDtypeStruct((s, grp, dk), F32),
        compiler_params=_cparams("parallel", "arbitrary"),
        name="sb_sample",
    )(page_table, qbd, _rev_cumsum_ones(page), *([cache_t] * npg))
    o = og.reshape(s, grp, n_kv, hd)
    return jnp.transpose(o, (2, 1, 0, 3)).reshape(n_q, s, hd).astype(BF16)


def _conv_step(x, buf_ref, nbuf_ref, cw_ref, cb_ref):
    xc = cb_ref[...]
    for i in range(CONV_WIDTH - 1):
        xc = xc + buf_ref[i] * cw_ref[i:i + 1, :]
        if i > 0:
            nbuf_ref[i - 1] = buf_ref[i]
    nbuf_ref[CONV_WIDTH - 2] = x
    return xc + x * cw_ref[CONV_WIDTH - 1:CONV_WIDTH, :]


def _rglru_step_kernel(xb_ref, gate_ref, buf_ref, h0_ref, cw_ref, cb_ref, wbd_ref, bias_ref, lam_ref,
                       y_ref, h_ref, nbuf_ref):
    xc = _conv_step(xb_ref[...], buf_ref, nbuf_ref, cw_ref, cb_ref)
    a, b = _lru_gates(xc, wbd_ref, bias_ref, lam_ref)
    h = a * h0_ref[...] + b
    h_ref[...] = h
    y_ref[...] = (_gelu_tanh(gate_ref[...]) * h).astype(y_ref.dtype)


def _rglru_step(xb, gate, buf, h0, conv_w, conv_b, wbd, bias, lam):
    s, w = xb.shape
    return pl.pallas_call(
        _rglru_step_kernel,
        out_shape=[jax.ShapeDtypeStruct((s, w), BF16), jax.ShapeDtypeStruct((s, w), F32),
                   jax.ShapeDtypeStruct((CONV_WIDTH - 1, s, w), F32)],
        compiler_params=pltpu.CompilerParams(vmem_limit_bytes=VMEM_LIMIT_BYTES),
    )(xb, gate, buf, h0, conv_w, conv_b.reshape(1, w), wbd, bias, lam.reshape(1, w))


def _mla_sample_kernel(pt_ref, qn_ref, qr_ref, new_ref, wkt_ref, wv_ref, *rest,
                       n_pages, page, kv_rank, rope, qk, nh, ppi):
    pages = rest[:n_pages]
    o_ref, ckv_scr, kpe_scr, s_scr = rest[n_pages:]
    n_iter = n_pages // ppi
    width = ppi * page
    nope = wkt_ref.shape[0] // nh
    for j in range(n_pages):
        lanes = slice((j % ppi) * page, (j % ppi + 1) * page)
        ckv_scr[j // ppi, :, lanes] = pages[j][:kv_rank, :].astype(BF16)
        kpe_scr[j // ppi, :, lanes] = pages[j][kv_rank:, :]
    qr = qr_ref[...]
    wkt = wkt_ref[...]
    q_lat = _dot(qn_ref[...], wkt).astype(BF16)

    def scores(kn, ct, kp):
        kn = kn.reshape(nh, nope, kn.shape[-1])
        ssq = jnp.sum(kn * kn, axis=1) + jnp.sum(kp * kp, axis=0, keepdims=True)
        s = _dot(q_lat, ct) + _dot(qr, kp.astype(BF16))
        return s * lax.rsqrt(ssq * (1.0 / qk) + NORM_EPS)

    new_t = jnp.broadcast_to(jnp.concatenate([new_ref[...], jnp.zeros((1, 2 * LANES - kv_rank - rope), F32)], axis=1),
                             (LANES, 2 * LANES)).T
    ct_new = new_t[:kv_rank, :].astype(BF16)
    first = lax.broadcasted_iota(jnp.int32, (nh, LANES), 1) == 0
    s_new = jnp.where(first, scores(_dot(wkt, ct_new), ct_new, new_t[kv_rank:kv_rank + rope, :]), -jnp.inf)

    def score_pass(i, m):
        ct = ckv_scr[i]
        s = scores(_dot(wkt, ct), ct, kpe_scr[i])
        s_scr[i] = s
        return jnp.maximum(m, jnp.max(s, axis=1, keepdims=True))

    unroll = 2 if n_iter % 2 == 0 else 1
    m = lax.fori_loop(0, n_iter, score_pass, jnp.max(s_new, axis=1, keepdims=True), unroll=unroll)
    p_new = jnp.exp2(s_new - m)

    def value_pass(i, c):
        l, acc_t = c
        p = jnp.exp2(s_scr[i] - m)
        acc_t = acc_t + lax.dot_general(ckv_scr[i], p.astype(BF16), _NT, preferred_element_type=F32)
        return l + jnp.sum(p, axis=1, keepdims=True), acc_t

    l0 = jnp.sum(p_new, axis=1, keepdims=True)
    acc0 = lax.dot_general(ct_new, p_new.astype(BF16), _NT, preferred_element_type=F32)
    l, acc_t = lax.fori_loop(0, n_iter, value_pass, (l0, acc0), unroll=unroll)
    acc = jnp.concatenate([acc_t, jnp.zeros((kv_rank, LANES - nh), F32)], axis=1).T[:nh, :]
    hi, lo = _split2(acc / l)
    o_ref[...] = _dot(hi, wv_ref[...]) + _dot(lo, wv_ref[...])


def _mla_sample(qh, lat_new, cache, layer, page_table, cw, gk):
    nh, s, _ = qh.shape
    nope, rope, vd, kv_rank, qk = cw["nope"], cw["rope"], cw["vd"], cw["kv_rank"], cw["qk"]
    page = cache.shape[3]
    n_pages = page_table.shape[1]
    ppi = max(c for c in (1, 2, 4) if n_pages % c == 0)
    qg = jnp.swapaxes(qh.astype(F32) * cw["gk128"].reshape(1, 1, LANES), 0, 1)
    qn = jnp.einsum('shd,hk->shkd', qg[:, :, :nope], jnp.eye(nh, dtype=F32)).reshape(s, nh, nh * nope).astype(BF16)
    qr = qg[:, :, nope:qk].astype(BF16)
    new = lat_new.reshape(s, 1, kv_rank + rope)

    def page_spec(j):
        return pl.BlockSpec((None, None, kv_rank + rope, page), lambda i, pt: (layer, pt[i, j], 0, 0))

    grid_spec = pltpu.PrefetchScalarGridSpec(
        num_scalar_prefetch=1, grid=(s,),
        in_specs=[pl.BlockSpec((None, nh, nh * nope), lambda i, pt: (i, 0, 0)),
                  pl.BlockSpec((None, nh, rope), lambda i, pt: (i, 0, 0)),
                  pl.BlockSpec((None, 1, kv_rank + rope), lambda i, pt: (i, 0, 0)),
                  pl.BlockSpec(cw["wkt"].shape, lambda i, pt: (0, 0)),
                  pl.BlockSpec(cw["wv"].shape, lambda i, pt: (0, 0))] + [page_spec(j) for j in range(n_pages)],
        out_specs=pl.BlockSpec((None, nh, nh * vd), lambda i, pt: (i, 0, 0)),
        scratch_shapes=[pltpu.VMEM((n_pages // ppi, kv_rank, ppi * page), BF16),
                        pltpu.VMEM((n_pages // ppi, rope, ppi * page), F32),
                        pltpu.VMEM((n_pages // ppi, nh, ppi * page), F32)])
    om = pl.pallas_call(
        functools.partial(_mla_sample_kernel, n_pages=n_pages, page=page, kv_rank=kv_rank, rope=rope, qk=qk, nh=nh,
                          ppi=ppi),
        grid_spec=grid_spec,
        out_shape=jax.ShapeDtypeStruct((s, nh, nh * vd), F32),
        compiler_params=_cparams("parallel"),
        name="mla_sample",
    )(page_table, qn, qr, new, cw["wkt"], cw["wv"], *([cache] * n_pages))
    idx = jnp.arange(nh)
    o = om.reshape(s, nh, nh, vd)[:, idx, idx, :]
    return jnp.swapaxes(o, 0, 1).astype(BF16)


def _ssd_step_kernel(xbc_ref, z_ref, dt_ref, buf_ref, h0_ref, cw_ref, cb_ref, dtb_ref, alog_ref, dsk_ref, ng_ref,
                     y_ref, hnew_ref, nbuf_ref, xc_scr, xct_scr, dtt_scr, dat_scr, yoff_scr,
                     *, nh, hd, ns, n_groups):
    r = pl.program_id(0)
    inner = nh * hd
    hpg = nh // n_groups
    lane = lax.broadcasted_iota(jnp.int32, (1, LANES), 1)

    @pl.when(r == 0)
    def _():
        xc = _silu(_conv_step(xbc_ref[...], buf_ref, nbuf_ref, cw_ref, cb_ref))
        xc_scr[...] = xc
        xct_scr[...] = xc.T
        dtv = _softplus(dt_ref[...] + dtb_ref[...])
        a = jnp.where(lane < nh, -jnp.exp(alog_ref[...]), 0.0)
        dtt_scr[...] = dtv.T
        dat_scr[...] = jnp.exp(dtv * a).T

    g = r // hpg
    xt = xct_scr[pl.ds(pl.multiple_of(r * hd, hd), hd), :]
    bt = xct_scr[pl.ds(pl.multiple_of(inner + g * ns, ns), ns), :]
    ct = xct_scr[pl.ds(pl.multiple_of(inner + (n_groups + g) * ns, ns), ns), :]
    dar = dat_scr[pl.ds(r, 1), :]
    coef = xt * dtt_scr[pl.ds(r, 1), :]
    for p in range(hd):
        h0p = h0_ref[p * ns:(p + 1) * ns, :]
        hnew_ref[p * ns:(p + 1) * ns, :] = dar * h0p + coef[p:p + 1, :] * bt
        yoff_scr[pl.ds(r * hd + p, 1), :] = jnp.sum(ct * h0p, axis=0, keepdims=True)

    @pl.when(r == nh - 1)
    def _():
        xc = xc_scr[...]
        xs = xc[:, :inner]
        dtv = dtt_scr[...].T
        da = dat_scr[...].T
        per_head = lambda v: jnp.concatenate(
            [jnp.broadcast_to(v[:, i:i + 1], (v.shape[0], hd)) for i in range(nh)], axis=1)
        cb = []
        for gi in range(n_groups):
            bmg = xc[:, inner + gi * ns:inner + (gi + 1) * ns]
            cmg = xc[:, inner + (n_groups + gi) * ns:inner + (n_groups + gi + 1) * ns]
            cb.append(jnp.broadcast_to(jnp.sum(cmg * bmg, axis=-1, keepdims=True), (xc.shape[0], hpg * hd)))
        y = jnp.concatenate(cb, axis=1) * per_head(dtv) * xs + yoff_scr[...].T * per_head(da) + dsk_ref[...] * xs
        _gated_group_norm(y, z_ref[...], ng_ref, y_ref, n_groups)


def _ssd_step(xbc, z, dt, buf, h0, conv_w, conv_b, cw, norm_g):
    s, conv_dim = xbc.shape
    _, nh, hd, ns = h0.shape
    inner = cw["inner"]
    n_groups = (conv_dim - inner) // (2 * ns)
    sz = hd * ns
    h0t = jnp.transpose(h0, (1, 2, 3, 0)).reshape(nh * sz, s)
    y, hnew, nbuf = pl.pallas_call(
        functools.partial(_ssd_step_kernel, nh=nh, hd=hd, ns=ns, n_groups=n_groups),
        grid=(nh,),
        in_specs=[_full((s, conv_dim)), _full((s, inner)), _full((s, LANES)), _full((CONV_WIDTH - 1, s, conv_dim)),
                  pl.BlockSpec((sz, s), lambda r: (r, 0)), _full((CONV_WIDTH, conv_dim)), _full((1, conv_dim)),
                  _full((1, LANES)), _full((1, LANES)), _full((1, inner)), _full((1, inner))],
        out_specs=[_full((s, inner)), pl.BlockSpec((sz, s), lambda r: (r, 0)),
                   _full((CONV_WIDTH - 1, s, conv_dim))],
        out_shape=[jax.ShapeDtypeStruct((s, inner), BF16), jax.ShapeDtypeStruct((nh * sz, s), F32),
                   jax.ShapeDtypeStruct((CONV_WIDTH - 1, s, conv_dim), F32)],
        scratch_shapes=[pltpu.VMEM((s, conv_dim), F32), pltpu.VMEM((conv_dim, s), F32), pltpu.VMEM((LANES, s), F32),
                        pltpu.VMEM((LANES, s), F32), pltpu.VMEM((inner, s), F32)],
        compiler_params=_cparams("arbitrary"),
        name="ssd_step",
    )(xbc, z, dt, buf, h0t, conv_w, conv_b.reshape(1, conv_dim), cw["dtb128"], cw["alog128"],
      cw["dsk"], norm_g.reshape(1, inner))
    return y, jnp.transpose(hnew.reshape(nh, hd, ns, s), (3, 0, 1, 2)), nbuf


def _prompt_tiles(t, hidden):
    pick = lambda want: max(c for c in (8, 16, 32, 64, 128, 256, 512, 1024) if c <= want and t % c == 0)
    th = hidden // 2 if hidden % (2 * LANES) == 0 else hidden
    return dict(tm_proj=pick(512), tm_post=pick(512), tq_sb=pick(256), tq_mla=pick(512), tt_lru=pick(512),
                ssd_chunk=pick(128), th=th)


def kernel(x_prompt, x_sample, cache_sb_kv, cache_mla_kv, page_table, state_lru_h, state_lru_conv, state_ssm_h, state_ssm_conv, norm_mix, norm_ffn, ab_w_in, ab_w_out, sb_q_gain, sb_k_gain, lru_conv_w, lru_conv_b, lru_wa, lru_ba, lru_wx, lru_bx, lru_lambda, cd_w_in, cd_w_out, mla_q_lat_gain, mla_w_uq, mla_kv_lat_gain, mla_w_ukv, mla_q_gain, mla_k_gain, ssd_conv_w, ssd_conv_b, ssd_dt_bias, ssd_a_log, ssd_d, ssd_norm_gain, ffn_w_in, ffn_w_out):
    nb, t, d = x_prompt.shape
    ns = x_sample.shape[0]
    depth = norm_mix.shape[0]
    page = cache_sb_kv.shape[2]
    past = page_table.shape[1] * page
    n_kv, hd = cache_sb_kv.shape[4], cache_sb_kv.shape[5]
    n_q = (ab_w_in.shape[2] - 2 * n_kv * hd - 2 * lru_lambda.shape[1]) // hd
    w_lru = lru_lambda.shape[1]
    sizes = _prompt_tiles(t, ffn_w_out.shape[1])

    xp = x_prompt.reshape(nb * t, d)
    xs = x_sample.reshape(ns, d)
    cache_sb = jnp.transpose(cache_sb_kv, (0, 1, 3, 4, 5, 2)).reshape(cache_sb_kv.shape[:2] + (2 * n_kv * hd, page))
    cache_mla = jnp.transpose(cache_mla_kv, (0, 1, 3, 2))
    cos_p, sin_p = _rope_tables(jnp.arange(t), mla_q_gain.shape[1] - (cache_mla_kv.shape[3] - mla_kv_lat_gain.shape[1]),
                                cache_mla_kv.shape[3] - mla_kv_lat_gain.shape[1])
    cos_s, sin_s = _rope_tables(jnp.full((ns,), past), mla_q_gain.shape[1] - (cache_mla_kv.shape[3] - mla_kv_lat_gain.shape[1]),
                                cache_mla_kv.shape[3] - mla_kv_lat_gain.shape[1])
    outs = {k: [] for k in ("sb_p", "sb_s", "lh_p", "lh_s", "lc_p", "lc_s", "ml_p", "ml_s", "sh_p", "sh_s", "sc_p", "sc_s")}
    for li in range(depth):
        w_ffn_in = ffn_w_in[li].astype(BF16)
        w_ffn_out = ffn_w_out[li].astype(BF16)
        if li % 2 == 0:
            e = li // 2
            w_in = ab_w_in[e].astype(BF16)
            w_out = ab_w_out[e].astype(BF16)
            wbd, bias = _lru_weights(lru_wa[e], lru_ba[e], lru_wx[e], lru_bx[e])
            qh, kh, vh, kv, xb, gate = _ab_inproj(xp, norm_mix[li], w_in, sb_q_gain[e], sb_k_gain[e],
                                                  nq=n_q, nk=n_kv, hd=hd, w_lru=w_lru, tm=sizes["tm_proj"])
            oh = _sb_prompt(qh, kh, vh, nb=nb, tq=sizes["tq_sb"])
            y2, hl, cbuf = _rglru_prompt(xb, gate, lru_conv_w[e], lru_conv_b[e], wbd, bias, lru_lambda[e],
                                         nb=nb, tt=sizes["tt_lru"])
            xp = _post(xp, oh, y2, w_out, norm_ffn[li], w_ffn_in, w_ffn_out, tm=sizes["tm_post"], th=sizes["th"])
            outs["sb_p"].append(kv.reshape(nb, t, 2, n_kv, hd))
            outs["lh_p"].append(hl.reshape(nb, w_lru))
            outs["lc_p"].append(cbuf)
            qh, kh, vh, kv, xb, gate = _ab_inproj(xs, norm_mix[li], w_in, sb_q_gain[e], sb_k_gain[e],
                                                  nq=n_q, nk=n_kv, hd=hd, w_lru=w_lru, tm=ns)
            oh = _sb_sample(qh, cache_sb, e, page_table, n_kv=n_kv)
            y2, hl, cbuf = _rglru_step(xb, gate, jnp.swapaxes(state_lru_conv[e], 0, 1), state_lru_h[e],
                                       lru_conv_w[e], lru_conv_b[e], wbd, bias, lru_lambda[e])
            xs = _post(xs, oh, y2, w_out, norm_ffn[li], w_ffn_in, w_ffn_out, tm=ns, th=sizes["th"])
            outs["sb_s"].append(kv.reshape(ns, 1, 2, n_kv, hd))
            outs["lh_s"].append(hl)
            outs["lc_s"].append(jnp.swapaxes(cbuf, 0, 1))
        else:
            o = li // 2
            cw = _cd_weights(cd_w_in[o], mla_w_uq[o], mla_w_ukv[o], mla_q_gain[o], mla_k_gain[o], ssd_dt_bias[o],
                             ssd_a_log[o], ssd_d[o], kv_rank=mla_kv_lat_gain.shape[1], q_rank=mla_q_lat_gain.shape[1],
                             rope=cache_mla_kv.shape[3] - mla_kv_lat_gain.shape[1], inner=ssd_norm_gain.shape[1],
                             conv_dim=ssd_conv_w.shape[2])
            w_out = cd_w_out[o].astype(BF16)
            qh, kh, vh, lat, z, xbc, dt = _cd_inproj(xp, norm_mix[li], cw, mla_q_lat_gain[o], mla_kv_lat_gain[o],
                                                     cos_p, sin_p, tm=sizes["tm_proj"])
            oh = _mla_prompt(qh, kh, vh, nb=nb, tq=sizes["tq_mla"])
            y2, hl, cbuf = _ssd_prompt(xbc, z, dt, ssd_conv_w[o], ssd_conv_b[o], cw, ssd_norm_gain[o],
                                       nb=nb, q=sizes["ssd_chunk"], state_dim=state_ssm_h.shape[4])
            xp = _post(xp, oh, y2, w_out, norm_ffn[li], w_ffn_in, w_ffn_out, tm=sizes["tm_post"], th=sizes["th"])
            outs["ml_p"].append(lat.reshape(nb, t, lat.shape[1]))
            outs["sh_p"].append(hl)
            outs["sc_p"].append(cbuf)
            qh, kh, vh, lat, z, xbc, dt = _cd_inproj(xs, norm_mix[li], cw, mla_q_lat_gain[o], mla_kv_lat_gain[o],
                                                     cos_s, sin_s, tm=ns)
            oh = _mla_sample(qh, lat, cache_mla, o, page_table, cw, mla_k_gain[o])
            y2, hl, cbuf = _ssd_step(xbc, z, dt, jnp.swapaxes(state_ssm_conv[o], 0, 1), state_ssm_h[o],
                                     ssd_conv_w[o], ssd_conv_b[o], cw, ssd_norm_gain[o])
            xs = _post(xs, oh, y2, w_out, norm_ffn[li], w_ffn_in, w_ffn_out, tm=ns, th=sizes["th"])
            outs["ml_s"].append(lat.reshape(ns, 1, lat.shape[1]))
            outs["sh_s"].append(hl)
            outs["sc_s"].append(jnp.swapaxes(cbuf, 0, 1))
    st = {k: jnp.stack(v) for k, v in outs.items()}
    return (xp.reshape(nb, t, d), xs.reshape(ns, 1, d), st["sb_p"], st["sb_s"], st["lh_p"], st["lh_s"], st["lc_p"],
            st["lc_s"], st["ml_p"], st["ml_s"], st["sh_p"], st["sh_s"], st["sc_p"], st["sc_s"])
```

```python
import functools
import math

import jax
import jax.numpy as jnp
from jax import lax
from jax.experimental import pallas as pl
from jax.experimental.pallas import tpu as pltpu

F32 = jnp.float32
BF16 = jnp.bfloat16
NORM_EPS = 1e-6
LRU_C = 8.0
ROPE_THETA = 10000.0
CONV_WIDTH = 4
LANES = 128
SUBLANES = 8
VMEM_LIMIT_BYTES = 56 * 1024 * 1024
_NT = (((1,), (1,)), ((), ()))
_TN = (((0,), (0,)), ((), ()))


def _cparams(*sem):
    return pltpu.CompilerParams(dimension_semantics=sem, vmem_limit_bytes=VMEM_LIMIT_BYTES)


def _dot(a, b):
    return jnp.dot(a, b, preferred_element_type=F32)


def _split2(x):
    hi = lax.bitcast_convert_type(lax.bitcast_convert_type(x, jnp.uint32) & jnp.uint32(0xFFFF0000), F32)
    return hi.astype(BF16), (x - hi).astype(BF16)


def _split3(x):
    hi = x.astype(BF16)
    r = x - hi.astype(F32)
    mid = r.astype(BF16)
    lo = (r - mid.astype(F32)).astype(BF16)
    return hi, mid, lo


def _rms(x, g):
    return x * lax.rsqrt(jnp.mean(x * x, axis=-1, keepdims=True) + NORM_EPS) * g


def _sigmoid(x):
    return 1.0 / (1.0 + jnp.exp(-x))


def _silu(x):
    return x * _sigmoid(x)


def _softplus(x):
    return jnp.maximum(x, 0.0) + jnp.log(1.0 + jnp.exp(-jnp.abs(x)))


def _gelu_tanh(x):
    c = math.sqrt(2.0 / math.pi)
    return x * (0.5 * (1.0 + jnp.tanh(c * (x + 0.044715 * (x * x * x)))))


def _block_ones(n, blk, dtype=BF16):
    r = lax.broadcasted_iota(jnp.int32, (n, n), 0) // blk
    c = lax.broadcasted_iota(jnp.int32, (n, n), 1) // blk
    return (r == c).astype(dtype)


def _full(shape):
    nd = len(shape)
    return pl.BlockSpec(shape, lambda *_: (0,) * nd)


def _ab_inproj_kernel(x_ref, g_ref, w_ref, gq_ref, gk_ref, pq_ref, pk_ref,
                      qh_ref, kh_ref, vh_ref, kv_ref, xb_ref, gate_ref, *, nq, nk, hd):
    h = _rms(x_ref[...], g_ref[...]).astype(BF16)
    p = _dot(h, w_ref[...])
    dq, dk = nq * hd, nk * hd
    q = p[:, :dq]
    k = p[:, dq:dq + dk]
    v = p[:, dq + dk:dq + 2 * dk]

    def head_norm(t, ones_ref, gain):
        hi, lo = _split2(t * t)
        ms = (_dot(hi, ones_ref[...]) + _dot(lo, ones_ref[...])) * (1.0 / hd)
        return t * lax.rsqrt(ms + NORM_EPS) * gain

    qn = head_norm(q, pq_ref, gq_ref[...])
    kn = head_norm(k, pk_ref, gk_ref[...])
    for i in range(nq):
        qh_ref[i] = qn[:, i * hd:(i + 1) * hd].astype(BF16)
    for i in range(nk):
        kh_ref[i] = kn[:, i * hd:(i + 1) * hd].astype(BF16)
        vh_ref[i] = v[:, i * hd:(i + 1) * hd].astype(BF16)
    kv_ref[:, :dk] = kn
    kv_ref[:, dk:] = v
    w_lru = xb_ref.shape[-1]
    xb_ref[...] = p[:, dq + 2 * dk:dq + 2 * dk + w_lru]
    gate_ref[...] = p[:, dq + 2 * dk + w_lru:]


def _ab_inproj(x, g, w_bf, gq, gk, *, nq, nk, hd, w_lru, tm):
    m, d = x.shape
    n = w_bf.shape[1]
    dq, dk = nq * hd, nk * hd
    scale = -math.log2(math.e) * hd ** -0.5
    gq_t = (jnp.tile(gq, nq) * scale).reshape(1, dq)
    gk_t = jnp.tile(gk, nk).reshape(1, dk)
    row = lambda i: (i, 0)
    head = lambda i: (0, i, 0)
    return pl.pallas_call(
        functools.partial(_ab_inproj_kernel, nq=nq, nk=nk, hd=hd),
        grid=(m // tm,),
        in_specs=[pl.BlockSpec((tm, d), row), _full((1, d)), _full((d, n)), _full((1, dq)), _full((1, dk)),
                  _full((dq, dq)), _full((dk, dk))],
        out_specs=[pl.BlockSpec((nq, tm, hd), head), pl.BlockSpec((nk, tm, hd), head),
                   pl.BlockSpec((nk, tm, hd), head), pl.BlockSpec((tm, 2 * dk), row),
                   pl.BlockSpec((tm, w_lru), row), pl.BlockSpec((tm, w_lru), row)],
        out_shape=[jax.ShapeDtypeStruct((nq, m, hd), BF16), jax.ShapeDtypeStruct((nk, m, hd), BF16),
                   jax.ShapeDtypeStruct((nk, m, hd), BF16), jax.ShapeDtypeStruct((m, 2 * dk), F32),
                   jax.ShapeDtypeStruct((m, w_lru), F32), jax.ShapeDtypeStruct((m, w_lru), F32)],
        compiler_params=_cparams("parallel"),
    )(x, g.reshape(1, d), w_bf, gq_t, gk_t, _block_ones(dq, hd), _block_ones(dk, hd))


def _neg_abs(x):
    bits = lax.bitcast_convert_type(x, jnp.uint32) | jnp.uint32(0x80000000)
    return lax.bitcast_convert_type(bits, F32)


def _sb_log2_keep(nz):
    return jnp.minimum(nz, 0.0) - jnp.log2(1.0 + jnp.exp2(_neg_abs(nz)))


_SB_DIAG, _SB_FULL, _SB_IDLE = 0, 1, 2
_SB_DEPTH = 4
_STEP_GROUP = 8
_MASKED = -1e30


def _sb_prompt_kernel(qi_ref, kb_ref, kind_ref, q_ref, k_ref, v_ref, uu_ref, bias_ref, o_ref,
                      nz_ring, x_ring, tail_ring, w_ring, r_scr, acc_scr, *, tq, grp, hd, n_steps):
    rows = grp * tq

    def blk(idx):
        return pl.ds(pl.multiple_of(idx * tq, tq), tq)

    for ring in (x_ring, tail_ring, w_ring, r_scr, acc_scr):
        ring[...] = jnp.zeros_like(ring)
    nz_ring[...] = jnp.full(nz_ring.shape, -_MASKED, F32)

    def step(t, ph):
        kind, qi = kind_ref[t], qi_ref[t]
        v = v_ref[blk(kb_ref[t]), :]
        carry_on = jnp.where(kind == _SB_DIAG, 0.0, 1.0)
        outs = [_dot(w_ring[ph % 2, g * tq:(g + 1) * tq, :], v) for g in range(grp)]
        for g in range(grp):
            acc = acc_scr[g * tq:(g + 1) * tq, :] * carry_on + outs[g]
            acc_scr[g * tq:(g + 1) * tq, :] = acc
            o_ref[g, blk(qi), :] = acc.astype(o_ref.dtype)
        kind = kind_ref[t + 1]
        tail = tail_ring[(ph + 1) % 2] + r_scr[...] * jnp.where(kind == _SB_DIAG, 0.0, 1.0)
        w = jnp.exp2(tail - nz_ring[(ph + 1) % 4])
        w_ring[(ph + 1) % 2] = w.astype(BF16)
        r_scr[...] = tail[:, 0:1]
        uu = uu_ref[...]
        tails = [_dot(x_ring[ph % 2, g * tq:(g + 1) * tq, :], uu) for g in range(grp)]
        for g in range(grp):
            tail_ring[ph % 2, g * tq:(g + 1) * tq, :] = tails[g]
        hi, lo = _split2(_sb_log2_keep(nz_ring[(ph + 3) % 4]))
        x_ring[(ph + 1) % 2, :, :tq] = hi
        x_ring[(ph + 1) % 2, :, tq:] = lo
        k = k_ref[blk(kb_ref[t + 4]), :]
        qrows = blk(qi_ref[t + 4])
        nzs = [lax.dot_general(q_ref[g, qrows, :], k, _NT, preferred_element_type=F32) for g in range(grp)]
        for g in range(grp):
            nz_ring[ph % 4, g * tq:(g + 1) * tq, :] = nzs[g] + bias_ref[kind_ref[t + 4], g * tq:(g + 1) * tq, :]

    def step_group(j, carry):
        for ph in range(_STEP_GROUP):
            step(_STEP_GROUP * j + ph, ph % 4)
        return carry

    lax.fori_loop(0, n_steps // _STEP_GROUP, step_group, 0)


def _rev_cumsum_ones(n):
    s = lax.broadcasted_iota(jnp.int32, (n, n), 0)
    j = lax.broadcasted_iota(jnp.int32, (n, n), 1)
    u = (s >= j).astype(BF16)
    return jnp.concatenate([u, u], axis=0)


def _sb_prompt(qh, kh, vh, *, nb, tq):
    nq, m, hd = qh.shape
    nk = kh.shape[0]
    grp = nq // nk
    t = m // nb
    n_blk = t // tq
    rows = grp * tq
    items = [(i, i - s, _SB_DIAG if s == 0 else _SB_FULL) for i in range(n_blk) for s in range(i + 1)]
    n_steps = -(-(len(items) + _SB_DEPTH) // _STEP_GROUP) * _STEP_GROUP
    items = [(0, 0, _SB_IDLE)] * _SB_DEPTH + items
    items = items + [(n_blk - 1, 0, _SB_IDLE)] * (n_steps + _SB_DEPTH - len(items))
    qi_tab, kb_tab, kind_tab = (jnp.asarray([it[c] for it in items], jnp.int32) for c in range(3))
    t_idx = lax.broadcasted_iota(jnp.int32, (grp, tq, tq), 1).reshape(rows, tq)
    earlier = lax.broadcasted_iota(jnp.int32, (rows, tq), 1) < t_idx
    bias = jnp.stack([jnp.where(earlier, 0.0, -_MASKED), jnp.zeros((rows, tq), F32),
                      jnp.full((rows, tq), -_MASKED, F32)])
    per_seq = lambda b, h, *_: (h, b, 0)
    const = lambda nd: (lambda b, h, *_: (0,) * nd)
    grid_spec = pltpu.PrefetchScalarGridSpec(
        num_scalar_prefetch=3, grid=(nb, nk),
        in_specs=[pl.BlockSpec((grp, t, hd), per_seq), pl.BlockSpec((None, t, hd), per_seq),
                  pl.BlockSpec((None, t, hd), per_seq), pl.BlockSpec((2 * tq, tq), const(2)),
                  pl.BlockSpec((3, rows, tq), const(3))],
        out_specs=pl.BlockSpec((grp, t, hd), per_seq),
        scratch_shapes=[pltpu.VMEM((4, rows, tq), F32), pltpu.VMEM((2, rows, 2 * tq), BF16),
                        pltpu.VMEM((2, rows, tq), F32), pltpu.VMEM((2, rows, tq), BF16),
                        pltpu.VMEM((rows, 1), F32), pltpu.VMEM((rows, hd), F32)])
    return pl.pallas_call(
        functools.partial(_sb_prompt_kernel, tq=tq, grp=grp, hd=hd, n_steps=n_steps),
        grid_spec=grid_spec,
        out_shape=jax.ShapeDtypeStruct((nq, m, hd), BF16),
        compiler_params=_cparams("parallel", "parallel"),
        name="sb_prompt",
    )(qi_tab, kb_tab, kind_tab, qh, kh, vh, _rev_cumsum_ones(tq), bias)


def _lru_gates(xc, wbd_ref, bias_ref, lam_ref):
    w = xc.shape[-1]
    ra = _dot(xc.astype(BF16), wbd_ref[...]) + bias_ref[...]
    r = _sigmoid(ra[:, :w])
    ig = _sigmoid(ra[:, w:])
    log_a = (-LRU_C) * r * _softplus(-lam_ref[...])
    a = jnp.exp(log_a)
    b = jnp.sqrt(-jnp.tanh(log_a) * (a * a + 1.0)) * (ig * xc)
    return a, b


def _rglru_prompt_kernel(xb_ref, gate_ref, cw_ref, cb_ref, wbd_ref, bias_ref, lam_ref,
                         y_ref, hlast_ref, cbuf_ref, xpad, a_scr, b_scr, h_scr, *, tt):
    t = pl.program_id(1)
    pad = 8

    @pl.when(t == 0)
    def _():
        h_scr[...] = jnp.zeros_like(h_scr)
        xpad[0:pad, :] = jnp.zeros((pad, xpad.shape[1]), F32)

    xpad[pad:pad + tt, :] = xb_ref[...]
    xc = cb_ref[...]
    for i in range(CONV_WIDTH):
        o = pad - (CONV_WIDTH - 1) + i
        xc = xc + xpad[o:o + tt, :] * cw_ref[i:i + 1, :]
    cbuf_ref[...] = xpad[pad + tt - (CONV_WIDTH - 1):pad + tt, :]
    xpad[0:pad, :] = xpad[tt:tt + pad, :]
    a, b = _lru_gates(xc, wbd_ref, bias_ref, lam_ref)
    a_scr[...] = a
    b_scr[...] = b

    def body(r, h):
        h = a_scr[pl.ds(r, 1), :] * h + b_scr[pl.ds(r, 1), :]
        b_scr[pl.ds(r, 1), :] = h
        return h

    h = lax.fori_loop(0, tt, body, h_scr[...], unroll=8)
    h_scr[...] = h
    hlast_ref[...] = h
    y_ref[...] = (_gelu_tanh(gate_ref[...]) * b_scr[...]).astype(y_ref.dtype)


def _lru_weights(wa, ba, wx, bx):
    nblk, bd, _ = wa.shape
    w = nblk * bd
    eye = jnp.eye(nblk, dtype=F32)

    def bdiag(m):
        return jnp.einsum('kij,kl->kilj', m, eye).reshape(w, w)

    wbd = jnp.concatenate([bdiag(wa), bdiag(wx)], axis=1).astype(BF16)
    bias = jnp.concatenate([ba, bx]).reshape(1, 2 * w)
    return wbd, bias


def _rglru_prompt(xb, gate, conv_w, conv_b, wbd, bias, lam, *, nb, tt):
    m, w = xb.shape
    t = m // nb
    nt = t // tt
    row = lambda b, i: (b * nt + i, 0)
    return pl.pallas_call(
        functools.partial(_rglru_prompt_kernel, tt=tt),
        grid=(nb, nt),
        in_specs=[pl.BlockSpec((tt, w), row), pl.BlockSpec((tt, w), row), _full((CONV_WIDTH, w)),
                  _full((1, w)), _full((w, 2 * w)), _full((1, 2 * w)), _full((1, w))],
        out_specs=[pl.BlockSpec((tt, w), row), pl.BlockSpec((None, 1, w), lambda b, i: (b, 0, 0)),
                   pl.BlockSpec((None, CONV_WIDTH - 1, w), lambda b, i: (b, 0, 0))],
        out_shape=[jax.ShapeDtypeStruct((m, w), BF16), jax.ShapeDtypeStruct((nb, 1, w), F32),
                   jax.ShapeDtypeStruct((nb, CONV_WIDTH - 1, w), F32)],
        scratch_shapes=[pltpu.VMEM((tt + 8, w), F32), pltpu.VMEM((tt, w), F32), pltpu.VMEM((tt, w), F32),
                        pltpu.VMEM((1, w), F32)],
        compiler_params=_cparams("parallel", "arbitrary"),
    )(xb, gate, conv_w, conv_b.reshape(1, w), wbd, bias, lam.reshape(1, w))


def _post_kernel(x_ref, oh_ref, y2_ref, wo_ref, g_ref, wu_ref, wg_ref, wd_ref, out_ref,
                 x1_scr, hb_scr, acc_scr):
    j = pl.program_id(1)
    nh, _, hd = oh_ref.shape

    @pl.when(j == 0)
    def _():
        heads = jnp.concatenate([oh_ref[h] for h in range(nh)], axis=1)
        mix = _dot(y2_ref[...], wo_ref[nh * hd:, :]) + _dot(heads, wo_ref[:nh * hd, :])
        x1 = x_ref[...] + mix
        x1_scr[...] = x1
        hb_scr[...] = _rms(x1, g_ref[...]).astype(BF16)
        acc_scr[...] = jnp.zeros_like(acc_scr)

    hb = hb_scr[...]
    u = _dot(hb, wu_ref[...])
    gg = _dot(hb, wg_ref[...])
    acc_scr[...] += _dot((_silu(u) * gg).astype(BF16), wd_ref[...])

    @pl.when(j == pl.num_programs(1) - 1)
    def _():
        out_ref[...] = x1_scr[...] + acc_scr[...]


def _post(x, oh, y2, w_out_bf, g, w_in_bf, w_dn_bf, *, tm, th):
    m, d = x.shape
    nh, _, hd = oh.shape
    hid = w_dn_bf.shape[0]
    nj = hid // th
    return pl.pallas_call(
        _post_kernel,
        grid=(m // tm, nj),
        in_specs=[pl.BlockSpec((tm, d), lambda i, j: (i, 0)),
                  pl.BlockSpec((nh, tm, hd), lambda i, j: (0, i, 0)),
                  pl.BlockSpec((tm, y2.shape[1]), lambda i, j: (i, 0)),
                  pl.BlockSpec(w_out_bf.shape, lambda i, j: (0, 0)),
                  pl.BlockSpec((1, d), lambda i, j: (0, 0)),
                  pl.BlockSpec((d, th), lambda i, j: (0, j)),
                  pl.BlockSpec((d, th), lambda i, j: (0, nj + j)),
                  pl.BlockSpec((th, d), lambda i, j: (j, 0))],
        out_specs=pl.BlockSpec((tm, d), lambda i, j: (i, 0)),
        out_shape=jax.ShapeDtypeStruct((m, d), F32),
        scratch_shapes=[pltpu.VMEM((tm, d), F32), pltpu.VMEM((tm, d), BF16), pltpu.VMEM((tm, d), F32)],
        compiler_params=_cparams("parallel", "arbitrary"),
    )(x, oh, y2, w_out_bf, g.reshape(1, d), w_in_bf, w_in_bf, w_dn_bf)


def _rope_tables(pos, nope, rope):
    half = rope // 2
    inv = ROPE_THETA ** (-jnp.arange(half, dtype=F32) / half)
    ang = pos.astype(F32)[:, None] * inv[None, :]
    n = pos.shape[0]
    lead = jnp.zeros((n, nope), F32)
    trail = jnp.zeros((n, LANES - nope - rope), F32)
    cos = jnp.concatenate([lead, jnp.cos(ang), jnp.cos(ang), trail], axis=1)
    sin = jnp.concatenate([lead, jnp.sin(ang), jnp.sin(ang), trail], axis=1)
    return cos, sin


def _rot_partner(w):
    half = w.shape[-1] // 2
    return jnp.concatenate([-w[..., half:], w[..., :half]], axis=-1)


def _cd_weights(w_in, w_uq, w_ukv, gq, gk, dt_bias, a_log, d_skip, *, kv_rank, q_rank, rope, inner, conv_dim):
    d = w_in.shape[0]
    nh, qk = w_uq.shape[1], w_uq.shape[2]
    nope = qk - rope
    vd = w_ukv.shape[2] - nope
    n_ssd = dt_bias.shape[0]
    assert qk <= LANES and kv_rank == LANES and n_ssd <= LANES
    o = 0
    cq = w_in[:, o:o + q_rank]; o += q_rank
    ckv = w_in[:, o:o + kv_rank]; o += kv_rank
    kpe = w_in[:, o:o + rope]; o += rope
    z = w_in[:, o:o + inner]; o += inner
    xbc = w_in[:, o:o + conv_dim]; o += conv_dim
    dt = w_in[:, o:o + n_ssd]

    def on_rope_lanes(w):
        return jnp.concatenate([jnp.zeros((d, nope), F32), w, jnp.zeros((d, LANES - qk), F32)], axis=1)

    w_pad = jnp.concatenate([cq, ckv, on_rope_lanes(kpe), on_rope_lanes(_rot_partner(kpe)), z, xbc,
                             dt, jnp.zeros((d, LANES - n_ssd), F32)], axis=1).astype(BF16)
    zq = jnp.zeros((q_rank, nh, LANES - qk), F32)
    wqm = jnp.concatenate([w_uq, zq], axis=2).reshape(q_rank, nh * LANES).astype(BF16)
    wqp = jnp.concatenate([jnp.zeros((q_rank, nh, nope), F32), _rot_partner(w_uq[:, :, nope:]), zq],
                          axis=2).reshape(q_rank, nh * LANES).astype(BF16)
    wk = w_ukv[:, :, :nope]
    wk128 = jnp.concatenate([wk, jnp.zeros((kv_rank, nh, LANES - nope), F32)], axis=2)
    pad1 = lambda v, fill=0.0: jnp.concatenate([v, jnp.full((LANES - v.shape[0],), fill, F32)]).reshape(1, LANES)
    return dict(
        w_pad=w_pad, wqm=wqm, wqp=wqp,
        wk128=wk128.reshape(kv_rank, nh * LANES).astype(BF16),
        wkt=wk.reshape(kv_rank, nh * nope).T.astype(BF16),
        wv=w_ukv[:, :, nope:].reshape(kv_rank, nh * vd).astype(BF16),
        gq128=pad1(gq * (math.log2(math.e) * qk ** -0.5)),
        gk128=pad1(gk), dtb128=pad1(dt_bias), alog128=pad1(a_log),
        dsk=jnp.repeat(d_skip, inner // n_ssd).reshape(1, inner),
        nh=nh, qk=qk, nope=nope, rope=rope, vd=vd, q_rank=q_rank, kv_rank=kv_rank, inner=inner,
        conv_dim=conv_dim, n_ssd=n_ssd)


def _cd_inproj_kernel(x_ref, g_ref, w_ref, gql_ref, gkl_ref, wqm_ref, wqp_ref, wk_ref, wv_ref, gq_ref, gk_ref,
                      cos_ref, sin_ref, qh_ref, kh_ref, vh_ref, lat_ref, z_ref, xbc_ref, dt_ref,
                      *, nh, qk, nope, rope, vd, q_rank, kv_rank, inner, conv_dim):
    h = _rms(x_ref[...], g_ref[...]).astype(BF16)
    p = _dot(h, w_ref[...])
    o = 0
    cq = p[:, o:o + q_rank]; o += q_rank
    ckv = p[:, o:o + kv_rank]; o += kv_rank
    kpe = p[:, o:o + LANES]; o += LANES
    kpe_rot = p[:, o:o + LANES]; o += LANES
    z_ref[...] = p[:, o:o + inner]; o += inner
    xbc_ref[...] = p[:, o:o + conv_dim]; o += conv_dim
    dt_ref[...] = p[:, o:o + LANES]

    cos_t = cos_ref[...]
    sin_t = sin_ref[...]
    lane = lax.broadcasted_iota(jnp.int32, (1, LANES), 1)
    cos_q = cos_t + (lane < nope).astype(F32)
    cqn = _rms(cq, gql_ref[...]).astype(BF16)
    qm = _dot(cqn, wqm_ref[...])
    qp = _dot(cqn, wqp_ref[...])
    for i in range(nh):
        qi = qm[:, i * LANES:(i + 1) * LANES] * cos_q + qp[:, i * LANES:(i + 1) * LANES] * sin_t
        ms = jnp.sum(qi * qi, axis=-1, keepdims=True) * (1.0 / qk)
        qh_ref[i] = (qi * lax.rsqrt(ms + NORM_EPS) * gq_ref[...]).astype(BF16)

    ckvn = _rms(ckv, gkl_ref[...])
    kper = kpe * cos_t + kpe_rot * sin_t
    lat_ref[:, :kv_rank] = ckvn
    lat_ref[:, kv_rank:] = kper[:, nope:nope + rope]
    cb = ckvn.astype(BF16)
    kn = _dot(cb, wk_ref[...])
    vv = _dot(cb, wv_ref[...])
    for i in range(nh):
        ki = kn[:, i * LANES:(i + 1) * LANES] + kper
        ms = jnp.sum(ki * ki, axis=-1, keepdims=True) * (1.0 / qk)
        kh_ref[i] = (ki * lax.rsqrt(ms + NORM_EPS) * gk_ref[...]).astype(BF16)
        vh_ref[i] = vv[:, i * vd:(i + 1) * vd].astype(BF16)


def _cd_inproj(x, g, cw, gql, gkl, cos, sin, *, tm):
    m, d = x.shape
    nh, vd, kv_rank, rope = cw["nh"], cw["vd"], cw["kv_rank"], cw["rope"]
    inner, conv_dim, q_rank = cw["inner"], cw["conv_dim"], cw["q_rank"]
    n_pos = cos.shape[0] // tm
    row = lambda i: (i, 0)
    head = lambda i: (0, i, 0)
    dims = {k: cw[k] for k in ("nh", "qk", "nope", "rope", "vd", "q_rank", "kv_rank", "inner", "conv_dim")}
    return pl.pallas_call(
        functools.partial(_cd_inproj_kernel, **dims),
        grid=(m // tm,),
        in_specs=[pl.BlockSpec((tm, d), row), _full((1, d)), _full(cw["w_pad"].shape), _full((1, q_rank)),
                  _full((1, kv_rank)), _full(cw["wqm"].shape), _full(cw["wqp"].shape), _full(cw["wk128"].shape),
                  _full(cw["wv"].shape), _full((1, LANES)), _full((1, LANES)),
                  pl.BlockSpec((tm, LANES), lambda i: (i % n_pos, 0)),
                  pl.BlockSpec((tm, LANES), lambda i: (i % n_pos, 0))],
        out_specs=[pl.BlockSpec((nh, tm, LANES), head), pl.BlockSpec((nh, tm, LANES), head),
                   pl.BlockSpec((nh, tm, vd), head), pl.BlockSpec((tm, kv_rank + rope), row),
                   pl.BlockSpec((tm, inner), row), pl.BlockSpec((tm, conv_dim), row),
                   pl.BlockSpec((tm, LANES), row)],
        out_shape=[jax.ShapeDtypeStruct((nh, m, LANES), BF16), jax.ShapeDtypeStruct((nh, m, LANES), BF16),
                   jax.ShapeDtypeStruct((nh, m, vd), BF16), jax.ShapeDtypeStruct((m, kv_rank + rope), F32),
                   jax.ShapeDtypeStruct((m, inner), F32), jax.ShapeDtypeStruct((m, conv_dim), F32),
                   jax.ShapeDtypeStruct((m, LANES), F32)],
        compiler_params=_cparams("parallel"),
    )(x, g.reshape(1, d), cw["w_pad"], gql.reshape(1, q_rank), gkl.reshape(1, kv_rank), cw["wqm"], cw["wqp"],
      cw["wk128"], cw["wv"], cw["gq128"], cw["gk128"], cos, sin)


_MLA_DEPTH = 3
_MLA_GROUP = 4


def _mla_prompt_kernel(qi_ref, kb_ref, kind_ref, q_ref, k_ref, v_ref, bias_ref, o_ref,
                       s_ring, p_ring, pv_ring, a_ring, linv_ring, m_scr, l_scr, acc_scr, *, tq, n_steps):
    halves = [slice(c * (tq // 2), (c + 1) * (tq // 2)) for c in range(2)]
    vd = v_ref.shape[-1]

    def blk(idx):
        return pl.ds(pl.multiple_of(idx * tq, tq), tq)

    for ring in (s_ring, p_ring, pv_ring, a_ring, linv_ring, l_scr, acc_scr):
        ring[...] = jnp.zeros_like(ring)
    m_scr[...] = jnp.full(m_scr.shape, -jnp.inf, F32)

    def step(t, ph):
        acc = a_ring[(ph + 1) % 4, :, :vd] * acc_scr[...] + pv_ring[(ph + 1) % 2]
        acc_scr[...] = acc
        o_ref[blk(qi_ref[t]), :] = (acc * linv_ring[(ph + 1) % 4, :, :vd]).astype(o_ref.dtype)
        v = v_ref[blk(kb_ref[t + 1]), :]
        outs = [_dot(p_ring[ph % 2, h, :], v) for h in halves]
        for h, o in zip(halves, outs):
            pv_ring[ph % 2, h, :] = o
        kind = kind_ref[t + 2]
        tiles = [slice(c * LANES, (c + 1) * LANES) for c in range(tq // LANES)]
        s = [s_ring[(ph + 1) % 2, :, c] + bias_ref[kind, :, c] for c in tiles]
        m_prev = jnp.where(kind == _SB_DIAG, -jnp.inf, m_scr[...])
        m_blk = jnp.max(functools.reduce(jnp.maximum, s), axis=-1, keepdims=True)
        m_new = jnp.maximum(m_prev, jnp.broadcast_to(m_blk, m_prev.shape))
        alpha = jnp.exp2(m_prev - m_new)
        p = [jnp.exp2(sc - m_new) for sc in s]
        l_blk = jnp.sum(functools.reduce(jnp.add, p), axis=-1, keepdims=True)
        l = alpha * l_scr[...] + jnp.broadcast_to(l_blk, m_prev.shape)
        m_scr[...] = m_new
        l_scr[...] = l
        a_ring[(ph + 3) % 4] = alpha
        linv_ring[(ph + 3) % 4] = 1.0 / l
        for c, pc in zip(tiles, p):
            p_ring[(ph + 1) % 2, :, c] = pc.astype(BF16)
        k = k_ref[blk(kb_ref[t + 3]), :]
        qrows = blk(qi_ref[t + 3])
        ss = [lax.dot_general(q_ref[qrows, :][h], k, _NT, preferred_element_type=F32) for h in halves]
        for h, sc in zip(halves, ss):
            s_ring[ph % 2, h, :] = sc

    def step_group(j, carry):
        for ph in range(_MLA_GROUP):
            step(_MLA_GROUP * j + ph, ph % 4)
        return carry

    lax.fori_loop(0, n_steps // _MLA_GROUP, step_group, 0)


def _mla_prompt(qh, kh, vh, *, nb, tq):
    nh, m, dk = qh.shape
    vd = vh.shape[-1]
    t = m // nb
    n_blk = t // tq
    items = [(i, i if s == 0 else s - 1, _SB_DIAG if s == 0 else _SB_FULL) for i in range(n_blk) for s in range(i + 1)]
    n_steps = -(-(len(items) + _MLA_DEPTH) // _MLA_GROUP) * _MLA_GROUP
    items = [(0, 0, _SB_IDLE)] * _MLA_DEPTH + items
    items = items + [(n_blk - 1, 0, _SB_IDLE)] * (n_steps + _MLA_DEPTH - len(items))
    qi_tab, kb_tab, kind_tab = (jnp.asarray([it[c] for it in items], jnp.int32) for c in range(3))
    causal = lax.broadcasted_iota(jnp.int32, (tq, tq), 1) <= lax.broadcasted_iota(jnp.int32, (tq, tq), 0)
    bias = jnp.stack([jnp.where(causal, 0.0, _MASKED), jnp.zeros((tq, tq), F32), jnp.full((tq, tq), _MASKED, F32)])
    per_seq = lambda b, h, *_: (h, b, 0)
    grid_spec = pltpu.PrefetchScalarGridSpec(
        num_scalar_prefetch=3, grid=(nb, nh),
        in_specs=[pl.BlockSpec((None, t, dk), per_seq), pl.BlockSpec((None, t, dk), per_seq),
                  pl.BlockSpec((None, t, vd), per_seq), pl.BlockSpec((3, tq, tq), lambda b, h, *_: (0, 0, 0))],
        out_specs=pl.BlockSpec((None, t, vd), per_seq),
        scratch_shapes=[pltpu.VMEM((2, tq, tq), F32), pltpu.VMEM((2, tq, tq), BF16), pltpu.VMEM((2, tq, vd), F32),
                        pltpu.VMEM((4, tq, LANES), F32), pltpu.VMEM((4, tq, LANES), F32), pltpu.VMEM((tq, LANES), F32),
                        pltpu.VMEM((tq, LANES), F32), pltpu.VMEM((tq, vd), F32)])
    return pl.pallas_call(
        functools.partial(_mla_prompt_kernel, tq=tq, n_steps=n_steps),
        grid_spec=grid_spec,
        out_shape=jax.ShapeDtypeStruct((nh, m, vd), BF16),
        compiler_params=_cparams("parallel", "parallel"),
        name="mla_prompt",
    )(qi_tab, kb_tab, kind_tab, qh, kh, vh, bias)


def _causal_conv_tile(x_ref, xpad, cw_ref, cb_ref, cbuf_ref, rows):
    pad = 8
    xpad[pad:pad + rows, :] = x_ref[...]
    xc = cb_ref[...]
    for i in range(CONV_WIDTH):
        o = pad - (CONV_WIDTH - 1) + i
        xc = xc + xpad[o:o + rows, :] * cw_ref[i:i + 1, :]
    cbuf_ref[...] = xpad[pad + rows - (CONV_WIDTH - 1):pad + rows, :]
    xpad[0:pad, :] = xpad[rows:rows + pad, :]
    return xc


def _gated_group_norm(y, z, ng_ref, y_ref, n_groups):
    y = y * _silu(z)
    gs = y.shape[-1] // n_groups
    for g in range(n_groups):
        y_ref[:, g * gs:(g + 1) * gs] = _rms(y[:, g * gs:(g + 1) * gs], ng_ref[:, g * gs:(g + 1) * gs]).astype(y_ref.dtype)


def _ssd_prompt_kernel(xbc_ref, z_ref, dt_ref, cw_ref, cb_ref, dtb_ref, alog_ref, dsk_ref, ng_ref, ltri_ref,
                       y_ref, hlast_ref, cbuf_ref, xpad, state, yscr, *, q, nh, hd, ns, n_groups):
    c = pl.program_id(1)

    @pl.when(c == 0)
    def _():
        state[...] = jnp.zeros_like(state)
        xpad[0:8, :] = jnp.zeros((8, xpad.shape[1]), F32)

    xc = _silu(_causal_conv_tile(xbc_ref, xpad, cw_ref, cb_ref, cbuf_ref, q))
    inner = nh * hd
    hpg = nh // n_groups
    xs = xc[:, :inner]
    bm = xc[:, inner:inner + n_groups * ns]
    cm = xc[:, inner + n_groups * ns:]
    lane = lax.broadcasted_iota(jnp.int32, (1, LANES), 1)
    dtv = _softplus(dt_ref[...] + dtb_ref[...])
    a = jnp.where(lane < nh, -jnp.exp(alog_ref[...]), 0.0)
    ltri = ltri_ref[...]
    cum = sum(_dot(ltri, part) for part in _split3(dtv * a))
    cum_t = cum.T
    dt_t = dtv.T
    cum_last = cum[q - 1:q, :]
    to_end = jnp.exp(cum_last - cum) * dtv
    ecum = jnp.exp(cum)
    elast = jnp.exp(cum_last)
    causal = lax.broadcasted_iota(jnp.int32, (q, q), 0) >= lax.broadcasted_iota(jnp.int32, (q, q), 1)
    for g in range(n_groups):
        cmg = cm[:, g * ns:(g + 1) * ns].astype(BF16)
        bmg = bm[:, g * ns:(g + 1) * ns].astype(BF16)
        cb = lax.dot_general(cmg, bmg, _NT, preferred_element_type=F32)
        for r in range(g * hpg, (g + 1) * hpg):
            seg = cum[:, r:r + 1] - cum_t[r:r + 1, :]
            decay = jnp.exp(jnp.where(causal, seg, -jnp.inf))
            mix = cb * decay * dt_t[r:r + 1, :]
            xh = xs[:, r * hd:(r + 1) * hd]
            hprev = state[r]
            y = _dot(mix.astype(BF16), xh.astype(BF16))
            y = y + lax.dot_general(cmg, hprev.astype(BF16), _NT, preferred_element_type=F32) * ecum[:, r:r + 1]
            yscr[:, r * hd:(r + 1) * hd] = y + dsk_ref[:, r * hd:(r + 1) * hd] * xh
            xw = (xh * to_end[:, r:r + 1]).astype(BF16)
            state[r] = elast[:, r:r + 1] * hprev + lax.dot_general(xw, bmg, _TN, preferred_element_type=F32)
    hlast_ref[...] = state[...]
    _gated_group_norm(yscr[...], z_ref[...], ng_ref, y_ref, n_groups)


def _lower_tri_ones(n):
    r = lax.broadcasted_iota(jnp.int32, (n, n), 0)
    c = lax.broadcasted_iota(jnp.int32, (n, n), 1)
    return (r >= c).astype(BF16)


def _ssd_prompt(xbc, z, dt, conv_w, conv_b, cw, norm_g, *, nb, q, state_dim):
    m, conv_dim = xbc.shape
    inner, nh = cw["inner"], cw["n_ssd"]
    hd = inner // nh
    n_groups = (conv_dim - inner) // (2 * state_dim)
    t = m // nb
    nc = t // q
    row = lambda b, c: (b * nc + c, 0)
    return pl.pallas_call(
        functools.partial(_ssd_prompt_kernel, q=q, nh=nh, hd=hd, ns=state_dim, n_groups=n_groups),
        grid=(nb, nc),
        in_specs=[pl.BlockSpec((q, conv_dim), row), pl.BlockSpec((q, inner), row), pl.BlockSpec((q, LANES), row),
                  _full((CONV_WIDTH, conv_dim)), _full((1, conv_dim)), _full((1, LANES)), _full((1, LANES)),
                  _full((1, inner)), _full((1, inner)), _full((q, q))],
        out_specs=[pl.BlockSpec((q, inner), row),
                   pl.BlockSpec((None, nh, hd, state_dim), lambda b, c: (b, 0, 0, 0)),
                   pl.BlockSpec((None, CONV_WIDTH - 1, conv_dim), lambda b, c: (b, 0, 0))],
        out_shape=[jax.ShapeDtypeStruct((m, inner), BF16), jax.ShapeDtypeStruct((nb, nh, hd, state_dim), F32),
                   jax.ShapeDtypeStruct((nb, CONV_WIDTH - 1, conv_dim), F32)],
        scratch_shapes=[pltpu.VMEM((q + 8, conv_dim), F32), pltpu.VMEM((nh, hd, state_dim), F32),
                        pltpu.VMEM((q, inner), F32)],
        compiler_params=_cparams("parallel", "arbitrary"),
        name="ssd_prompt",
    )(xbc, z, dt, conv_w, conv_b.reshape(1, conv_dim), cw["dtb128"], cw["alog128"], cw["dsk"],
      norm_g.reshape(1, inner), _lower_tri_ones(q))


def _sb_sample_kernel(pt_ref, qbd_ref, uu_ref, *rest, npg, dk, grp):
    pages = rest[:npg]
    o_ref, z_scr, r_scr, acc_scr, wb_scr = rest[npg:]
    c = pl.program_id(1)
    nrow = qbd_ref.shape[0]
    n_kv = nrow // grp
    hd = dk // n_kv
    keys = z_scr.shape[1]

    @pl.when(c == 0)
    def _():
        r_scr[...] = jnp.zeros_like(r_scr)
        acc_scr[...] = jnp.zeros_like(acc_scr)

    qbd = qbd_ref[...]
    for j in range(npg):
        z_scr[j * nrow:(j + 1) * nrow, :] = _dot(qbd, pages[j][:dk, :].astype(BF16))
    nz = z_scr[...]
    hi, lo = _split2(_sb_log2_keep(nz))
    tl = _dot(jnp.concatenate([hi, lo], axis=1), uu_ref[...])
    tot = jnp.broadcast_to(tl[:, 0:1], tl.shape)
    r = r_scr[...]
    carries = [None] * npg
    for j in reversed(range(npg)):
        carries[j] = r
        r = r + tot[j * nrow:(j + 1) * nrow, :]
    r_scr[...] = r
    z_scr[...] = jnp.exp2(tl + jnp.concatenate(carries, axis=0) - nz)
    sub = wb_scr.shape[1]
    for r in range(npg * nrow):
        wb_scr[r] = jnp.broadcast_to(z_scr[r:r + 1, :], (sub, keys))
    for rg in range(dk // sub):
        h = rg * sub // hd
        rows = slice(rg * sub, (rg + 1) * sub)
        acc = [acc_scr[g, rows, :] for g in range(grp)]
        for j in range(npg):
            vt = pages[j][dk + rg * sub:dk + (rg + 1) * sub, :]
            for g in range(grp):
                acc[g] = acc[g] + vt * wb_scr[j * nrow + g * n_kv + h]
        for g in range(grp):
            acc_scr[g, rows, :] = acc[g]

    @pl.when(c == pl.num_programs(1) - 1)
    def _():
        for g in range(grp):
            o_ref[g:g + 1, :] = jnp.sum(acc_scr[g].T, axis=0, keepdims=True)


def _sb_sample(qh, cache_t, layer, page_table, *, n_kv, npg=32):
    n_q, s, hd = qh.shape
    grp = n_q // n_kv
    dk = n_kv * hd
    page = cache_t.shape[3]
    n_pages = page_table.shape[1]
    npg = min(npg, n_pages)
    nch = n_pages // npg
    eye = jnp.eye(n_kv, dtype=qh.dtype)
    qbd = jnp.einsum('hgsd,hk->sghkd', qh.reshape(n_kv, grp, s, hd), eye).reshape(s, n_q, dk)

    def page_spec(j):
        return pl.BlockSpec((None, None, 2 * dk, page),
                            lambda i, c, pt: (layer, pt[i, (nch - 1 - c) * npg + j], 0, 0))

    grid_spec = pltpu.PrefetchScalarGridSpec(
        num_scalar_prefetch=1, grid=(s, nch),
        in_specs=[pl.BlockSpec((None, n_q, dk), lambda i, c, pt: (i, 0, 0)),
                  pl.BlockSpec((2 * page, page), lambda i, c, pt: (0, 0))] + [page_spec(j) for j in range(npg)],
        out_specs=pl.BlockSpec((None, grp, dk), lambda i, c, pt: (i, 0, 0)),
        scratch_shapes=[pltpu.VMEM((npg * n_q, page), F32), pltpu.VMEM((n_q, page), F32),
                        pltpu.VMEM((grp, dk, page), F32), pltpu.VMEM((npg * n_q, SUBLANES, page), F32)])
    og = pl.pallas_call(
        functools.partial(_sb_sample_kernel, npg=npg, dk=dk, grp=grp),
        grid_spec=grid_spec,
        out_shape=jax.ShapeDtypeStruct((s, grp, dk), F32),
        compiler_params=_cparams("parallel", "arbitrary"),
        name="sb_sample",
    )(page_table, qbd, _rev_cumsum_ones(page), *([cache_t] * npg))
    o = og.reshape(s, grp, n_kv, hd)
    return jnp.transpose(o, (2, 1, 0, 3)).reshape(n_q, s, hd).astype(BF16)


def _conv_step(x, buf_ref, nbuf_ref, cw_ref, cb_ref):
    xc = cb_ref[...]
    for i in range(CONV_WIDTH - 1):
        xc = xc + buf_ref[i] * cw_ref[i:i + 1, :]
        if i > 0:
            nbuf_ref[i - 1] = buf_ref[i]
    nbuf_ref[CONV_WIDTH - 2] = x
    return xc + x * cw_ref[CONV_WIDTH - 1:CONV_WIDTH, :]


def _rglru_step_kernel(xb_ref, gate_ref, buf_ref, h0_ref, cw_ref, cb_ref, wbd_ref, bias_ref, lam_ref,
                       y_ref, h_ref, nbuf_ref):
    xc = _conv_step(xb_ref[...], buf_ref, nbuf_ref, cw_ref, cb_ref)
    a, b = _lru_gates(xc, wbd_ref, bias_ref, lam_ref)
    h = a * h0_ref[...] + b
    h_ref[...] = h
    y_ref[...] = (_gelu_tanh(gate_ref[...]) * h).astype(y_ref.dtype)


def _rglru_step(xb, gate, buf, h0, conv_w, conv_b, wbd, bias, lam):
    s, w = xb.shape
    return pl.pallas_call(
        _rglru_step_kernel,
        out_shape=[jax.ShapeDtypeStruct((s, w), BF16), jax.ShapeDtypeStruct((s, w), F32),
                   jax.ShapeDtypeStruct((CONV_WIDTH - 1, s, w), F32)],
        compiler_params=pltpu.CompilerParams(vmem_limit_bytes=VMEM_LIMIT_BYTES),
    )(xb, gate, buf, h0, conv_w, conv_b.reshape(1, w), wbd, bias, lam.reshape(1, w))


def _mla_sample_kernel(pt_ref, qn_ref, qr_ref, new_ref, wkt_ref, wv_ref, *rest,
                       n_pages, page, kv_rank, rope, qk, nh, ppi):
    pages = rest[:n_pages]
    o_ref, ckv_scr, kpe_scr, s_scr = rest[n_pages:]
    n_iter = n_pages // ppi
    width = ppi * page
    nope = wkt_ref.shape[0] // nh
    for j in range(n_pages):
        lanes = slice((j % ppi) * page, (j % ppi + 1) * page)
        ckv_scr[j // ppi, :, lanes] = pages[j][:kv_rank, :].astype(BF16)
        kpe_scr[j // ppi, :, lanes] = pages[j][kv_rank:, :]
    qr = qr_ref[...]
    wkt = wkt_ref[...]
    q_lat = _dot(qn_ref[...], wkt).astype(BF16)

    def scores(kn, ct, kp):
        kn = kn.reshape(nh, nope, kn.shape[-1])
        ssq = jnp.sum(kn * kn, axis=1) + jnp.sum(kp * kp, axis=0, keepdims=True)
        s = _dot(q_lat, ct) + _dot(qr, kp.astype(BF16))
        return s * lax.rsqrt(ssq * (1.0 / qk) + NORM_EPS)

    new_t = jnp.broadcast_to(jnp.concatenate([new_ref[...], jnp.zeros((1, 2 * LANES - kv_rank - rope), F32)], axis=1),
                             (LANES, 2 * LANES)).T
    ct_new = new_t[:kv_rank, :].astype(BF16)
    first = lax.broadcasted_iota(jnp.int32, (nh, LANES), 1) == 0
    s_new = jnp.where(first, scores(_dot(wkt, ct_new), ct_new, new_t[kv_rank:kv_rank + rope, :]), -jnp.inf)

    def score_pass(i, m):
        ct = ckv_scr[i]
        s = scores(_dot(wkt, ct), ct, kpe_scr[i])
        s_scr[i] = s
        return jnp.maximum(m, jnp.max(s, axis=1, keepdims=True))

    unroll = max(c for c in (1, 2, 4, 8) if n_iter % c == 0)
    m = lax.fori_loop(0, n_iter, score_pass, jnp.max(s_new, axis=1, keepdims=True), unroll=unroll)
    p_new = jnp.exp2(s_new - m)

    def value_pass(i, c):
        l, acc_t = c
        p = jnp.exp2(s_scr[i] - m)
        acc_t = acc_t + lax.dot_general(ckv_scr[i], p.astype(BF16), _NT, preferred_element_type=F32)
        return l + jnp.sum(p, axis=1, keepdims=True), acc_t

    l0 = jnp.sum(p_new, axis=1, keepdims=True)
    acc0 = lax.dot_general(ct_new, p_new.astype(BF16), _NT, preferred_element_type=F32)
    l, acc_t = lax.fori_loop(0, n_iter, value_pass, (l0, acc0), unroll=unroll)
    acc = jnp.concatenate([acc_t, jnp.zeros((kv_rank, LANES - nh), F32)], axis=1).T[:nh, :]
    hi, lo = _split2(acc / l)
    o_ref[...] = _dot(hi, wv_ref[...]) + _dot(lo, wv_ref[...])


def _mla_sample(qh, lat_new, cache, layer, page_table, cw, gk):
    nh, s, _ = qh.shape
    nope, rope, vd, kv_rank, qk = cw["nope"], cw["rope"], cw["vd"], cw["kv_rank"], cw["qk"]
    page = cache.shape[3]
    n_pages = page_table.shape[1]
    ppi = max(c for c in (1, 2, 4) if n_pages % c == 0)
    qg = jnp.swapaxes(qh.astype(F32) * cw["gk128"].reshape(1, 1, LANES), 0, 1)
    qn = jnp.einsum('shd,hk->shkd', qg[:, :, :nope], jnp.eye(nh, dtype=F32)).reshape(s, nh, nh * nope).astype(BF16)
    qr = qg[:, :, nope:qk].astype(BF16)
    new = lat_new.reshape(s, 1, kv_rank + rope)

    def page_spec(j):
        return pl.BlockSpec((None, None, kv_rank + rope, page), lambda i, pt: (layer, pt[i, j], 0, 0))

    grid_spec = pltpu.PrefetchScalarGridSpec(
        num_scalar_prefetch=1, grid=(s,),
        in_specs=[pl.BlockSpec((None, nh, nh * nope), lambda i, pt: (i, 0, 0)),
                  pl.BlockSpec((None, nh, rope), lambda i, pt: (i, 0, 0)),
                  pl.BlockSpec((None, 1, kv_rank + rope), lambda i, pt: (i, 0, 0)),
                  pl.BlockSpec(cw["wkt"].shape, lambda i, pt: (0, 0)),
                  pl.BlockSpec(cw["wv"].shape, lambda i, pt: (0, 0))] + [page_spec(j) for j in range(n_pages)],
        out_specs=pl.BlockSpec((None, nh, nh * vd), lambda i, pt: (i, 0, 0)),
        scratch_shapes=[pltpu.VMEM((n_pages // ppi, kv_rank, ppi * page), BF16),
                        pltpu.VMEM((n_pages // ppi, rope, ppi * page), F32),
                        pltpu.VMEM((n_pages // ppi, nh, ppi * page), F32)])
    om = pl.pallas_call(
        functools.partial(_mla_sample_kernel, n_pages=n_pages, page=page, kv_rank=kv_rank, rope=rope, qk=qk, nh=nh,
                          ppi=ppi),
        grid_spec=grid_spec,
        out_shape=jax.ShapeDtypeStruct((s, nh, nh * vd), F32),
        compiler_params=_cparams("parallel"),
        name="mla_sample",
    )(page_table, qn, qr, new, cw["wkt"], cw["wv"], *([cache] * n_pages))
    idx = jnp.arange(nh)
    o = om.reshape(s, nh, nh, vd)[:, idx, idx, :]
    return jnp.swapaxes(o, 0, 1).astype(BF16)


def _ssd_step_kernel(xbc_ref, z_ref, dt_ref, buf_ref, h0_ref, cw_ref, cb_ref, dtb_ref, alog_ref, dsk_ref, ng_ref,
                     y_ref, hnew_ref, nbuf_ref, xc_scr, xct_scr, dtt_scr, dat_scr, yoff_scr,
                     *, nh, hd, ns, n_groups):
    r = pl.program_id(0)
    inner = nh * hd
    hpg = nh // n_groups
    lane = lax.broadcasted_iota(jnp.int32, (1, LANES), 1)

    @pl.when(r == 0)
    def _():
        xc = _silu(_conv_step(xbc_ref[...], buf_ref, nbuf_ref, cw_ref, cb_ref))
        xc_scr[...] = xc
        xct_scr[...] = xc.T
        dtv = _softplus(dt_ref[...] + dtb_ref[...])
        a = jnp.where(lane < nh, -jnp.exp(alog_ref[...]), 0.0)
        dtt_scr[...] = dtv.T
        dat_scr[...] = jnp.exp(dtv * a).T

    g = r // hpg
    xt = xct_scr[pl.ds(pl.multiple_of(r * hd, hd), hd), :]
    bt = xct_scr[pl.ds(pl.multiple_of(inner + g * ns, ns), ns), :]
    ct = xct_scr[pl.ds(pl.multiple_of(inner + (n_groups + g) * ns, ns), ns), :]
    dar = dat_scr[pl.ds(r, 1), :]
    coef = xt * dtt_scr[pl.ds(r, 1), :]
    for p in range(hd):
        h0p = h0_ref[p * ns:(p + 1) * ns, :]
        hnew_ref[p * ns:(p + 1) * ns, :] = dar * h0p + coef[p:p + 1, :] * bt
        yoff_scr[pl.ds(r * hd + p, 1), :] = jnp.sum(ct * h0p, axis=0, keepdims=True)

    @pl.when(r == nh - 1)
    def _():
        xc = xc_scr[...]
        xs = xc[:, :inner]
        dtv = dtt_scr[...].T
        da = dat_scr[...].T
        per_head = lambda v: jnp.concatenate(
            [jnp.broadcast_to(v[:, i:i + 1], (v.shape[0], hd)) for i in range(nh)], axis=1)
        cb = []
        for gi in range(n_groups):
            bmg = xc[:, inner + gi * ns:inner + (gi + 1) * ns]
            cmg = xc[:, inner + (n_groups + gi) * ns:inner + (n_groups + gi + 1) * ns]
            cb.append(jnp.broadcast_to(jnp.sum(cmg * bmg, axis=-1, keepdims=True), (xc.shape[0], hpg * hd)))
        y = jnp.concatenate(cb, axis=1) * per_head(dtv) * xs + yoff_scr[...].T * per_head(da) + dsk_ref[...] * xs
        _gated_group_norm(y, z_ref[...], ng_ref, y_ref, n_groups)


def _ssd_step(xbc, z, dt, buf, h0, conv_w, conv_b, cw, norm_g):
    s, conv_dim = xbc.shape
    _, nh, hd, ns = h0.shape
    inner = cw["inner"]
    n_groups = (conv_dim - inner) // (2 * ns)
    sz = hd * ns
    h0t = jnp.transpose(h0, (1, 2, 3, 0)).reshape(nh * sz, s)
    y, hnew, nbuf = pl.pallas_call(
        functools.partial(_ssd_step_kernel, nh=nh, hd=hd, ns=ns, n_groups=n_groups),
        grid=(nh,),
        in_specs=[_full((s, conv_dim)), _full((s, inner)), _full((s, LANES)), _full((CONV_WIDTH - 1, s, conv_dim)),
                  pl.BlockSpec((sz, s), lambda r: (r, 0)), _full((CONV_WIDTH, conv_dim)), _full((1, conv_dim)),
                  _full((1, LANES)), _full((1, LANES)), _full((1, inner)), _full((1, inner))],
        out_specs=[_full((s, inner)), pl.BlockSpec((sz, s), lambda r: (r, 0)),
                   _full((CONV_WIDTH - 1, s, conv_dim))],
        out_shape=[jax.ShapeDtypeStruct((s, inner), BF16), jax.ShapeDtypeStruct((nh * sz, s), F32),
                   jax.ShapeDtypeStruct((CONV_WIDTH - 1, s, conv_dim), F32)],
        scratch_shapes=[pltpu.VMEM((s, conv_dim), F32), pltpu.VMEM((conv_dim, s), F32), pltpu.VMEM((LANES, s), F32),
                        pltpu.VMEM((LANES, s), F32), pltpu.VMEM((inner, s), F32)],
        compiler_params=_cparams("arbitrary"),
        name="ssd_step",
    )(xbc, z, dt, buf, h0t, conv_w, conv_b.reshape(1, conv_dim), cw["dtb128"], cw["alog128"],
      cw["dsk"], norm_g.reshape(1, inner))
    return y, jnp.transpose(hnew.reshape(nh, hd, ns, s), (3, 0, 1, 2)), nbuf


def _prompt_tiles(t, hidden):
    pick = lambda want: max(c for c in (8, 16, 32, 64, 128, 256, 512, 1024) if c <= want and t % c == 0)
    th = hidden // 2 if hidden % (2 * LANES) == 0 else hidden
    return dict(tm_proj=pick(512), tm_post=pick(512), tq_sb=pick(256), tq_mla=pick(512), tt_lru=pick(512),
                ssd_chunk=pick(128), th=th)


def kernel(x_prompt, x_sample, cache_sb_kv, cache_mla_kv, page_table, state_lru_h, state_lru_conv, state_ssm_h, state_ssm_conv, norm_mix, norm_ffn, ab_w_in, ab_w_out, sb_q_gain, sb_k_gain, lru_conv_w, lru_conv_b, lru_wa, lru_ba, lru_wx, lru_bx, lru_lambda, cd_w_in, cd_w_out, mla_q_lat_gain, mla_w_uq, mla_kv_lat_gain, mla_w_ukv, mla_q_gain, mla_k_gain, ssd_conv_w, ssd_conv_b, ssd_dt_bias, ssd_a_log, ssd_d, ssd_norm_gain, ffn_w_in, ffn_w_out):
    nb, t, d = x_prompt.shape
    ns = x_sample.shape[0]
    depth = norm_mix.shape[0]
    page = cache_sb_kv.shape[2]
    past = page_table.shape[1] * page
    n_kv, hd = cache_sb_kv.shape[4], cache_sb_kv.shape[5]
    n_q = (ab_w_in.shape[2] - 2 * n_kv * hd - 2 * lru_lambda.shape[1]) // hd
    w_lru = lru_lambda.shape[1]
    sizes = _prompt_tiles(t, ffn_w_out.shape[1])

    xp = x_prompt.reshape(nb * t, d)
    xs = x_sample.reshape(ns, d)
    cache_sb = jnp.transpose(cache_sb_kv, (0, 1, 3, 4, 5, 2)).reshape(cache_sb_kv.shape[:2] + (2 * n_kv * hd, page))
    cache_mla = jnp.transpose(cache_mla_kv, (0, 1, 3, 2))
    cos_p, sin_p = _rope_tables(jnp.arange(t), mla_q_gain.shape[1] - (cache_mla_kv.shape[3] - mla_kv_lat_gain.shape[1]),
                                cache_mla_kv.shape[3] - mla_kv_lat_gain.shape[1])
    cos_s, sin_s = _rope_tables(jnp.full((ns,), past), mla_q_gain.shape[1] - (cache_mla_kv.shape[3] - mla_kv_lat_gain.shape[1]),
                                cache_mla_kv.shape[3] - mla_kv_lat_gain.shape[1])
    outs = {k: [] for k in ("sb_p", "sb_s", "lh_p", "lh_s", "lc_p", "lc_s", "ml_p", "ml_s", "sh_p", "sh_s", "sc_p", "sc_s")}
    for li in range(depth):
        w_ffn_in = ffn_w_in[li].astype(BF16)
        w_ffn_out = ffn_w_out[li].astype(BF16)
        if li % 2 == 0:
            e = li // 2
            w_in = ab_w_in[e].astype(BF16)
            w_out = ab_w_out[e].astype(BF16)
            wbd, bias = _lru_weights(lru_wa[e], lru_ba[e], lru_wx[e], lru_bx[e])
            qh, kh, vh, kv, xb, gate = _ab_inproj(xp, norm_mix[li], w_in, sb_q_gain[e], sb_k_gain[e],
                                                  nq=n_q, nk=n_kv, hd=hd, w_lru=w_lru, tm=sizes["tm_proj"])
            oh = _sb_prompt(qh, kh, vh, nb=nb, tq=sizes["tq_sb"])
            y2, hl, cbuf = _rglru_prompt(xb, gate, lru_conv_w[e], lru_conv_b[e], wbd, bias, lru_lambda[e],
                                         nb=nb, tt=sizes["tt_lru"])
            xp = _post(xp, oh, y2, w_out, norm_ffn[li], w_ffn_in, w_ffn_out, tm=sizes["tm_post"], th=sizes["th"])
            outs["sb_p"].append(kv.reshape(nb, t, 2, n_kv, hd))
            outs["lh_p"].append(hl.reshape(nb, w_lru))
            outs["lc_p"].append(cbuf)
            qh, kh, vh, kv, xb, gate = _ab_inproj(xs, norm_mix[li], w_in, sb_q_gain[e], sb_k_gain[e],
                                                  nq=n_q, nk=n_kv, hd=hd, w_lru=w_lru, tm=ns)
            oh = _sb_sample(qh, cache_sb, e, page_table, n_kv=n_kv)
            y2, hl, cbuf = _rglru_step(xb, gate, jnp.swapaxes(state_lru_conv[e], 0, 1), state_lru_h[e],
                                       lru_conv_w[e], lru_conv_b[e], wbd, bias, lru_lambda[e])
            xs = _post(xs, oh, y2, w_out, norm_ffn[li], w_ffn_in, w_ffn_out, tm=ns, th=sizes["th"])
            outs["sb_s"].append(kv.reshape(ns, 1, 2, n_kv, hd))
            outs["lh_s"].append(hl)
            outs["lc_s"].append(jnp.swapaxes(cbuf, 0, 1))
        else:
            o = li // 2
            cw = _cd_weights(cd_w_in[o], mla_w_uq[o], mla_w_ukv[o], mla_q_gain[o], mla_k_gain[o], ssd_dt_bias[o],
                             ssd_a_log[o], ssd_d[o], kv_rank=mla_kv_lat_gain.shape[1], q_rank=mla_q_lat_gain.shape[1],
                             rope=cache_mla_kv.shape[3] - mla_kv_lat_gain.shape[1], inner=ssd_norm_gain.shape[1],
                             conv_dim=ssd_conv_w.shape[2])
            w_out = cd_w_out[o].astype(BF16)
            qh, kh, vh, lat, z, xbc, dt = _cd_inproj(xp, norm_mix[li], cw, mla_q_lat_gain[o], mla_kv_lat_gain[o],
                                                     cos_p, sin_p, tm=sizes["tm_proj"])
            oh = _mla_prompt(qh, kh, vh, nb=nb, tq=sizes["tq_mla"])
            y2, hl, cbuf = _ssd_prompt(xbc, z, dt, ssd_conv_w[o], ssd_conv_b[o], cw, ssd_norm_gain[o],
                                       nb=nb, q=sizes["ssd_chunk"], state_dim=state_ssm_h.shape[4])
            xp = _post(xp, oh, y2, w_out, norm_ffn[li], w_ffn_in, w_ffn_out, tm=sizes["tm_post"], th=sizes["th"])
            outs["ml_p"].append(lat.reshape(nb, t, lat.shape[1]))
            outs["sh_p"].append(hl)
            outs["sc_p"].append(cbuf)
            qh, kh, vh, lat, z, xbc, dt = _cd_inproj(xs, norm_mix[li], cw, mla_q_lat_gain[o], mla_kv_lat_gain[o],
                                                     cos_s, sin_s, tm=ns)
            oh = _mla_sample(qh, lat, cache_mla, o, page_table, cw, mla_k_gain[o])
            y2, hl, cbuf = _ssd_step(xbc, z, dt, jnp.swapaxes(state_ssm_conv[o], 0, 1), state_ssm_h[o],
                                     ssd_conv_w[o], ssd_conv_b[o], cw, ssd_norm_gain[o])
            xs = _post(xs, oh, y2, w_out, norm_ffn[li], w_ffn_in, w_ffn_out, tm=ns, th=sizes["th"])
            outs["ml_s"].append(lat.reshape(ns, 1, lat.shape[1]))
            outs["sh_s"].append(hl)
            outs["sc_s"].append(jnp.swapaxes(cbuf, 0, 1))
    st = {k: jnp.stack(v) for k, v in outs.items()}
    return (xp.reshape(nb, t, d), xs.reshape(ns, 1, d), st["sb_p"], st["sb_s"], st["lh_p"], st["lh_s"], st["lc_p"],
            st["lc_s"], st["ml_p"], st["ml_s"], st["sh_p"], st["sh_s"], st["sc_p"], st["sc_s"])
```

```python
import functools
import math

import jax
import jax.numpy as jnp
from jax import lax
from jax.experimental import pallas as pl
from jax.experimental.pallas import tpu as pltpu

F32 = jnp.float32
BF16 = jnp.bfloat16
NORM_EPS = 1e-6
LRU_C = 8.0
ROPE_THETA = 10000.0
CONV_WIDTH = 4
LANES = 128
SUBLANES = 8
VMEM_LIMIT_BYTES = 56 * 1024 * 1024
_NT = (((1,), (1,)), ((), ()))
_TN = (((0,), (0,)), ((), ()))


def _cparams(*sem):
    return pltpu.CompilerParams(dimension_semantics=sem, vmem_limit_bytes=VMEM_LIMIT_BYTES)


def _dot(a, b):
    return jnp.dot(a, b, preferred_element_type=F32)


def _split2(x):
    hi = lax.bitcast_convert_type(lax.bitcast_convert_type(x, jnp.uint32) & jnp.uint32(0xFFFF0000), F32)
    return hi.astype(BF16), (x - hi).astype(BF16)


def _split3(x):
    hi = x.astype(BF16)
    r = x - hi.astype(F32)
    mid = r.astype(BF16)
    lo = (r - mid.astype(F32)).astype(BF16)
    return hi, mid, lo


def _rms(x, g):
    return x * lax.rsqrt(jnp.mean(x * x, axis=-1, keepdims=True) + NORM_EPS) * g


def _sigmoid(x):
    return 1.0 / (1.0 + jnp.exp(-x))


def _silu(x):
    return x * _sigmoid(x)


def _softplus(x):
    return jnp.maximum(x, 0.0) + jnp.log(1.0 + jnp.exp(-jnp.abs(x)))


def _gelu_tanh(x):
    c = math.sqrt(2.0 / math.pi)
    return x * (0.5 * (1.0 + jnp.tanh(c * (x + 0.044715 * (x * x * x)))))


def _block_ones(n, blk, dtype=BF16):
    r = lax.broadcasted_iota(jnp.int32, (n, n), 0) // blk
    c = lax.broadcasted_iota(jnp.int32, (n, n), 1) // blk
    return (r == c).astype(dtype)


def _full(shape):
    nd = len(shape)
    return pl.BlockSpec(shape, lambda *_: (0,) * nd)


def _ab_inproj_kernel(x_ref, g_ref, w_ref, gq_ref, gk_ref, pq_ref, pk_ref,
                      qh_ref, kh_ref, vh_ref, kv_ref, xb_ref, gate_ref, *, nq, nk, hd):
    h = _rms(x_ref[...], g_ref[...]).astype(BF16)
    p = _dot(h, w_ref[...])
    dq, dk = nq * hd, nk * hd
    q = p[:, :dq]
    k = p[:, dq:dq + dk]
    v = p[:, dq + dk:dq + 2 * dk]

    def head_norm(t, ones_ref, gain):
        hi, lo = _split2(t * t)
        ms = (_dot(hi, ones_ref[...]) + _dot(lo, ones_ref[...])) * (1.0 / hd)
        return t * lax.rsqrt(ms + NORM_EPS) * gain

    qn = head_norm(q, pq_ref, gq_ref[...])
    kn = head_norm(k, pk_ref, gk_ref[...])
    for i in range(nq):
        qh_ref[i] = qn[:, i * hd:(i + 1) * hd].astype(BF16)
    for i in range(nk):
        kh_ref[i] = kn[:, i * hd:(i + 1) * hd].astype(BF16)
        vh_ref[i] = v[:, i * hd:(i + 1) * hd].astype(BF16)
    kv_ref[:, :dk] = kn
    kv_ref[:, dk:] = v
    w_lru = xb_ref.shape[-1]
    xb_ref[...] = p[:, dq + 2 * dk:dq + 2 * dk + w_lru]
    gate_ref[...] = p[:, dq + 2 * dk + w_lru:]


def _ab_inproj(x, g, w_bf, gq, gk, *, nq, nk, hd, w_lru, tm):
    m, d = x.shape
    n = w_bf.shape[1]
    dq, dk = nq * hd, nk * hd
    scale = -math.log2(math.e) * hd ** -0.5
    gq_t = (jnp.tile(gq, nq) * scale).reshape(1, dq)
    gk_t = jnp.tile(gk, nk).reshape(1, dk)
    row = lambda i: (i, 0)
    head = lambda i: (0, i, 0)
    return pl.pallas_call(
        functools.partial(_ab_inproj_kernel, nq=nq, nk=nk, hd=hd),
        grid=(m // tm,),
        in_specs=[pl.BlockSpec((tm, d), row), _full((1, d)), _full((d, n)), _full((1, dq)), _full((1, dk)),
                  _full((dq, dq)), _full((dk, dk))],
        out_specs=[pl.BlockSpec((nq, tm, hd), head), pl.BlockSpec((nk, tm, hd), head),
                   pl.BlockSpec((nk, tm, hd), head), pl.BlockSpec((tm, 2 * dk), row),
                   pl.BlockSpec((tm, w_lru), row), pl.BlockSpec((tm, w_lru), row)],
        out_shape=[jax.ShapeDtypeStruct((nq, m, hd), BF16), jax.ShapeDtypeStruct((nk, m, hd), BF16),
                   jax.ShapeDtypeStruct((nk, m, hd), BF16), jax.ShapeDtypeStruct((m, 2 * dk), F32),
                   jax.ShapeDtypeStruct((m, w_lru), F32), jax.ShapeDtypeStruct((m, w_lru), F32)],
        compiler_params=_cparams("parallel"),
    )(x, g.reshape(1, d), w_bf, gq_t, gk_t, _block_ones(dq, hd), _block_ones(dk, hd))


def _neg_abs(x):
    bits = lax.bitcast_convert_type(x, jnp.uint32) | jnp.uint32(0x80000000)
    return lax.bitcast_convert_type(bits, F32)


def _sb_log2_keep(nz):
    return jnp.minimum(nz, 0.0) - jnp.log2(1.0 + jnp.exp2(_neg_abs(nz)))


_SB_DIAG, _SB_FULL, _SB_IDLE = 0, 1, 2
_SB_DEPTH = 4
_STEP_GROUP = 8
_MASKED = -1e30


def _sb_prompt_kernel(qi_ref, kb_ref, kind_ref, q_ref, k_ref, v_ref, uu_ref, bias_ref, o_ref,
                      nz_ring, x_ring, tail_ring, w_ring, r_scr, acc_scr, *, tq, grp, hd, n_steps):
    rows = grp * tq

    def blk(idx):
        return pl.ds(pl.multiple_of(idx * tq, tq), tq)

    for ring in (x_ring, tail_ring, w_ring, r_scr, acc_scr):
        ring[...] = jnp.zeros_like(ring)
    nz_ring[...] = jnp.full(nz_ring.shape, -_MASKED, F32)

    def step(t, ph):
        kind, qi = kind_ref[t], qi_ref[t]
        v = v_ref[blk(kb_ref[t]), :]
        carry_on = jnp.where(kind == _SB_DIAG, 0.0, 1.0)
        outs = [_dot(w_ring[ph % 2, g * tq:(g + 1) * tq, :], v) for g in range(grp)]
        for g in range(grp):
            acc = acc_scr[g * tq:(g + 1) * tq, :] * carry_on + outs[g]
            acc_scr[g * tq:(g + 1) * tq, :] = acc
            o_ref[g, blk(qi), :] = acc.astype(o_ref.dtype)
        kind = kind_ref[t + 1]
        tail = tail_ring[(ph + 1) % 2] + r_scr[...] * jnp.where(kind == _SB_DIAG, 0.0, 1.0)
        w = jnp.exp2(tail - nz_ring[(ph + 1) % 4])
        w_ring[(ph + 1) % 2] = w.astype(BF16)
        r_scr[...] = tail[:, 0:1]
        uu = uu_ref[...]
        tails = [_dot(x_ring[ph % 2, g * tq:(g + 1) * tq, :], uu) for g in range(grp)]
        for g in range(grp):
            tail_ring[ph % 2, g * tq:(g + 1) * tq, :] = tails[g]
        hi, lo = _split2(_sb_log2_keep(nz_ring[(ph + 3) % 4]))
        x_ring[(ph + 1) % 2, :, :tq] = hi
        x_ring[(ph + 1) % 2, :, tq:] = lo
        k = k_ref[blk(kb_ref[t + 4]), :]
        qrows = blk(qi_ref[t + 4])
        nzs = [lax.dot_general(q_ref[g, qrows, :], k, _NT, preferred_element_type=F32) for g in range(grp)]
        for g in range(grp):
            nz_ring[ph % 4, g * tq:(g + 1) * tq, :] = nzs[g] + bias_ref[kind_ref[t + 4], g * tq:(g + 1) * tq, :]

    def step_group(j, carry):
        for ph in range(_STEP_GROUP):
            step(_STEP_GROUP * j + ph, ph % 4)
        return carry

    lax.fori_loop(0, n_steps // _STEP_GROUP, step_group, 0)


def _rev_cumsum_ones(n):
    s = lax.broadcasted_iota(jnp.int32, (n, n), 0)
    j = lax.broadcasted_iota(jnp.int32, (n, n), 1)
    u = (s >= j).astype(BF16)
    return jnp.concatenate([u, u], axis=0)


def _sb_prompt(qh, kh, vh, *, nb, tq):
    nq, m, hd = qh.shape
    nk = kh.shape[0]
    grp = nq // nk
    t = m // nb
    n_blk = t // tq
    rows = grp * tq
    items = [(i, i - s, _SB_DIAG if s == 0 else _SB_FULL) for i in range(n_blk) for s in range(i + 1)]
    n_steps = -(-(len(items) + _SB_DEPTH) // _STEP_GROUP) * _STEP_GROUP
    items = [(0, 0, _SB_IDLE)] * _SB_DEPTH + items
    items = items + [(n_blk - 1, 0, _SB_IDLE)] * (n_steps + _SB_DEPTH - len(items))
    qi_tab, kb_tab, kind_tab = (jnp.asarray([it[c] for it in items], jnp.int32) for c in range(3))
    t_idx = lax.broadcasted_iota(jnp.int32, (grp, tq, tq), 1).reshape(rows, tq)
    earlier = lax.broadcasted_iota(jnp.int32, (rows, tq), 1) < t_idx
    bias = jnp.stack([jnp.where(earlier, 0.0, -_MASKED), jnp.zeros((rows, tq), F32),
                      jnp.full((rows, tq), -_MASKED, F32)])
    per_seq = lambda b, h, *_: (h, b, 0)
    const = lambda nd: (lambda b, h, *_: (0,) * nd)
    grid_spec = pltpu.PrefetchScalarGridSpec(
        num_scalar_prefetch=3, grid=(nb, nk),
        in_specs=[pl.BlockSpec((grp, t, hd), per_seq), pl.BlockSpec((None, t, hd), per_seq),
                  pl.BlockSpec((None, t, hd), per_seq), pl.BlockSpec((2 * tq, tq), const(2)),
                  pl.BlockSpec((3, rows, tq), const(3))],
        out_specs=pl.BlockSpec((grp, t, hd), per_seq),
        scratch_shapes=[pltpu.VMEM((4, rows, tq), F32), pltpu.VMEM((2, rows, 2 * tq), BF16),
                        pltpu.VMEM((2, rows, tq), F32), pltpu.VMEM((2, rows, tq), BF16),
                        pltpu.VMEM((rows, 1), F32), pltpu.VMEM((rows, hd), F32)])
    return pl.pallas_call(
        functools.partial(_sb_prompt_kernel, tq=tq, grp=grp, hd=hd, n_steps=n_steps),
        grid_spec=grid_spec,
        out_shape=jax.ShapeDtypeStruct((nq, m, hd), BF16),
        compiler_params=_cparams("parallel", "parallel"),
        name="sb_prompt",
    )(qi_tab, kb_tab, kind_tab, qh, kh, vh, _rev_cumsum_ones(tq), bias)


def _lru_gates(xc, wbd_ref, bias_ref, lam_ref):
    w = xc.shape[-1]
    ra = _dot(xc.astype(BF16), wbd_ref[...]) + bias_ref[...]
    r = _sigmoid(ra[:, :w])
    ig = _sigmoid(ra[:, w:])
    log_a = (-LRU_C) * r * _softplus(-lam_ref[...])
    a = jnp.exp(log_a)
    b = jnp.sqrt(-jnp.tanh(log_a) * (a * a + 1.0)) * (ig * xc)
    return a, b


def _rglru_prompt_kernel(xb_ref, gate_ref, cw_ref, cb_ref, wbd_ref, bias_ref, lam_ref,
                         y_ref, hlast_ref, cbuf_ref, xpad, a_scr, b_scr, h_scr, *, tt):
    t = pl.program_id(1)

    @pl.when(t == 0)
    def _():
        h_scr[...] = jnp.zeros_like(h_scr)
        xpad[0:SUBLANES, :] = jnp.zeros((SUBLANES, xpad.shape[1]), F32)

    xc = _causal_conv_tile(xb_ref, xpad, cw_ref, cb_ref, cbuf_ref, tt)
    a, b = _lru_gates(xc, wbd_ref, bias_ref, lam_ref)
    a_scr[...] = a
    b_scr[...] = b

    def body(r, h):
        h = a_scr[pl.ds(r, 1), :] * h + b_scr[pl.ds(r, 1), :]
        b_scr[pl.ds(r, 1), :] = h
        return h

    h = lax.fori_loop(0, tt, body, h_scr[...], unroll=8)
    h_scr[...] = h
    hlast_ref[...] = h
    y_ref[...] = (_gelu_tanh(gate_ref[...]) * b_scr[...]).astype(y_ref.dtype)


def _lru_weights(wa, ba, wx, bx):
    nblk, bd, _ = wa.shape
    w = nblk * bd
    eye = jnp.eye(nblk, dtype=F32)

    def bdiag(m):
        return jnp.einsum('kij,kl->kilj', m, eye).reshape(w, w)

    wbd = jnp.concatenate([bdiag(wa), bdiag(wx)], axis=1).astype(BF16)
    bias = jnp.concatenate([ba, bx]).reshape(1, 2 * w)
    return wbd, bias


def _rglru_prompt(xb, gate, conv_w, conv_b, wbd, bias, lam, *, nb, tt):
    m, w = xb.shape
    t = m // nb
    nt = t // tt
    row = lambda b, i: (b * nt + i, 0)
    return pl.pallas_call(
        functools.partial(_rglru_prompt_kernel, tt=tt),
        grid=(nb, nt),
        in_specs=[pl.BlockSpec((tt, w), row), pl.BlockSpec((tt, w), row), _full((CONV_WIDTH, w)),
                  _full((1, w)), _full((w, 2 * w)), _full((1, 2 * w)), _full((1, w))],
        out_specs=[pl.BlockSpec((tt, w), row), pl.BlockSpec((None, 1, w), lambda b, i: (b, 0, 0)),
                   pl.BlockSpec((None, CONV_WIDTH - 1, w), lambda b, i: (b, 0, 0))],
        out_shape=[jax.ShapeDtypeStruct((m, w), BF16), jax.ShapeDtypeStruct((nb, 1, w), F32),
                   jax.ShapeDtypeStruct((nb, CONV_WIDTH - 1, w), F32)],
        scratch_shapes=[pltpu.VMEM((tt + SUBLANES, w), F32), pltpu.VMEM((tt, w), F32), pltpu.VMEM((tt, w), F32),
                        pltpu.VMEM((1, w), F32)],
        compiler_params=_cparams("parallel", "arbitrary"),
    )(xb, gate, conv_w, conv_b.reshape(1, w), wbd, bias, lam.reshape(1, w))


def _post_kernel(x_ref, oh_ref, y2_ref, wo_ref, g_ref, wu_ref, wg_ref, wd_ref, out_ref,
                 x1_scr, hb_scr, acc_scr):
    j = pl.program_id(1)
    nh, _, hd = oh_ref.shape

    @pl.when(j == 0)
    def _():
        heads = jnp.concatenate([oh_ref[h] for h in range(nh)], axis=1)
        mix = _dot(y2_ref[...], wo_ref[nh * hd:, :]) + _dot(heads, wo_ref[:nh * hd, :])
        x1 = x_ref[...] + mix
        x1_scr[...] = x1
        hb_scr[...] = _rms(x1, g_ref[...]).astype(BF16)
        acc_scr[...] = jnp.zeros_like(acc_scr)

    hb = hb_scr[...]
    u = _dot(hb, wu_ref[...])
    gg = _dot(hb, wg_ref[...])
    acc_scr[...] += _dot((_silu(u) * gg).astype(BF16), wd_ref[...])

    @pl.when(j == pl.num_programs(1) - 1)
    def _():
        out_ref[...] = x1_scr[...] + acc_scr[...]


def _post(x, oh, y2, w_out_bf, g, w_in_bf, w_dn_bf, *, tm, th):
    m, d = x.shape
    nh, _, hd = oh.shape
    hid = w_dn_bf.shape[0]
    nj = hid // th
    return pl.pallas_call(
        _post_kernel,
        grid=(m // tm, nj),
        in_specs=[pl.BlockSpec((tm, d), lambda i, j: (i, 0)),
                  pl.BlockSpec((nh, tm, hd), lambda i, j: (0, i, 0)),
                  pl.BlockSpec((tm, y2.shape[1]), lambda i, j: (i, 0)),
                  pl.BlockSpec(w_out_bf.shape, lambda i, j: (0, 0)),
                  pl.BlockSpec((1, d), lambda i, j: (0, 0)),
                  pl.BlockSpec((d, th), lambda i, j: (0, j)),
                  pl.BlockSpec((d, th), lambda i, j: (0, nj + j)),
                  pl.BlockSpec((th, d), lambda i, j: (j, 0))],
        out_specs=pl.BlockSpec((tm, d), lambda i, j: (i, 0)),
        out_shape=jax.ShapeDtypeStruct((m, d), F32),
        scratch_shapes=[pltpu.VMEM((tm, d), F32), pltpu.VMEM((tm, d), BF16), pltpu.VMEM((tm, d), F32)],
        compiler_params=_cparams("parallel", "arbitrary"),
    )(x, oh, y2, w_out_bf, g.reshape(1, d), w_in_bf, w_in_bf, w_dn_bf)


def _rope_tables(pos, nope, rope):
    half = rope // 2
    inv = ROPE_THETA ** (-jnp.arange(half, dtype=F32) / half)
    ang = pos.astype(F32)[:, None] * inv[None, :]
    n = pos.shape[0]
    lead = jnp.zeros((n, nope), F32)
    trail = jnp.zeros((n, LANES - nope - rope), F32)
    cos = jnp.concatenate([lead, jnp.cos(ang), jnp.cos(ang), trail], axis=1)
    sin = jnp.concatenate([lead, jnp.sin(ang), jnp.sin(ang), trail], axis=1)
    return cos, sin


def _rot_partner(w):
    half = w.shape[-1] // 2
    return jnp.concatenate([-w[..., half:], w[..., :half]], axis=-1)


def _cd_weights(w_in, w_uq, w_ukv, gq, gk, dt_bias, a_log, d_skip, *, kv_rank, q_rank, rope, inner, conv_dim):
    d = w_in.shape[0]
    nh, qk = w_uq.shape[1], w_uq.shape[2]
    nope = qk - rope
    vd = w_ukv.shape[2] - nope
    n_ssd = dt_bias.shape[0]
    assert qk <= LANES and kv_rank == LANES and n_ssd <= LANES
    o = 0
    cq = w_in[:, o:o + q_rank]; o += q_rank
    ckv = w_in[:, o:o + kv_rank]; o += kv_rank
    kpe = w_in[:, o:o + rope]; o += rope
    z = w_in[:, o:o + inner]; o += inner
    xbc = w_in[:, o:o + conv_dim]; o += conv_dim
    dt = w_in[:, o:o + n_ssd]

    def on_rope_lanes(w):
        return jnp.concatenate([jnp.zeros((d, nope), F32), w, jnp.zeros((d, LANES - qk), F32)], axis=1)

    w_pad = jnp.concatenate([cq, ckv, on_rope_lanes(kpe), on_rope_lanes(_rot_partner(kpe)), z, xbc,
                             dt, jnp.zeros((d, LANES - n_ssd), F32)], axis=1).astype(BF16)
    zq = jnp.zeros((q_rank, nh, LANES - qk), F32)
    wqm = jnp.concatenate([w_uq, zq], axis=2).reshape(q_rank, nh * LANES).astype(BF16)
    wqp = jnp.concatenate([jnp.zeros((q_rank, nh, nope), F32), _rot_partner(w_uq[:, :, nope:]), zq],
                          axis=2).reshape(q_rank, nh * LANES).astype(BF16)
    wk = w_ukv[:, :, :nope]
    wk128 = jnp.concatenate([wk, jnp.zeros((kv_rank, nh, LANES - nope), F32)], axis=2)
    pad1 = lambda v, fill=0.0: jnp.concatenate([v, jnp.full((LANES - v.shape[0],), fill, F32)]).reshape(1, LANES)
    return dict(
        w_pad=w_pad, wqm=wqm, wqp=wqp,
        wk128=wk128.reshape(kv_rank, nh * LANES).astype(BF16),
        wkt=wk.reshape(kv_rank, nh * nope).T.astype(BF16),
        wv=w_ukv[:, :, nope:].reshape(kv_rank, nh * vd).astype(BF16),
        gq128=pad1(gq * (math.log2(math.e) * qk ** -0.5)),
        gk128=pad1(gk), dtb128=pad1(dt_bias), alog128=pad1(a_log),
        dsk=jnp.repeat(d_skip, inner // n_ssd).reshape(1, inner),
        nh=nh, qk=qk, nope=nope, rope=rope, vd=vd, q_rank=q_rank, kv_rank=kv_rank, inner=inner,
        conv_dim=conv_dim, n_ssd=n_ssd)


def _cd_inproj_kernel(x_ref, g_ref, w_ref, gql_ref, gkl_ref, wqm_ref, wqp_ref, wk_ref, wv_ref, gq_ref, gk_ref,
                      cos_ref, sin_ref, qh_ref, kh_ref, vh_ref, lat_ref, z_ref, xbc_ref, dt_ref,
                      *, nh, qk, nope, rope, vd, q_rank, kv_rank, inner, conv_dim):
    h = _rms(x_ref[...], g_ref[...]).astype(BF16)
    p = _dot(h, w_ref[...])
    o = 0
    cq = p[:, o:o + q_rank]; o += q_rank
    ckv = p[:, o:o + kv_rank]; o += kv_rank
    kpe = p[:, o:o + LANES]; o += LANES
    kpe_rot = p[:, o:o + LANES]; o += LANES
    z_ref[...] = p[:, o:o + inner]; o += inner
    xbc_ref[...] = p[:, o:o + conv_dim]; o += conv_dim
    dt_ref[...] = p[:, o:o + LANES]

    cos_t = cos_ref[...]
    sin_t = sin_ref[...]
    lane = lax.broadcasted_iota(jnp.int32, (1, LANES), 1)
    cos_q = cos_t + (lane < nope).astype(F32)
    cqn = _rms(cq, gql_ref[...]).astype(BF16)
    qm = _dot(cqn, wqm_ref[...])
    qp = _dot(cqn, wqp_ref[...])
    for i in range(nh):
        qi = qm[:, i * LANES:(i + 1) * LANES] * cos_q + qp[:, i * LANES:(i + 1) * LANES] * sin_t
        ms = jnp.sum(qi * qi, axis=-1, keepdims=True) * (1.0 / qk)
        qh_ref[i] = (qi * lax.rsqrt(ms + NORM_EPS) * gq_ref[...]).astype(BF16)

    ckvn = _rms(ckv, gkl_ref[...])
    kper = kpe * cos_t + kpe_rot * sin_t
    lat_ref[:, :kv_rank] = ckvn
    lat_ref[:, kv_rank:] = kper[:, nope:nope + rope]
    cb = ckvn.astype(BF16)
    kn = _dot(cb, wk_ref[...])
    vv = _dot(cb, wv_ref[...])
    for i in range(nh):
        ki = kn[:, i * LANES:(i + 1) * LANES] + kper
        ms = jnp.sum(ki * ki, axis=-1, keepdims=True) * (1.0 / qk)
        kh_ref[i] = (ki * lax.rsqrt(ms + NORM_EPS) * gk_ref[...]).astype(BF16)
        vh_ref[i] = vv[:, i * vd:(i + 1) * vd].astype(BF16)


def _cd_inproj(x, g, cw, gql, gkl, cos, sin, *, tm):
    m, d = x.shape
    nh, vd, kv_rank, rope = cw["nh"], cw["vd"], cw["kv_rank"], cw["rope"]
    inner, conv_dim, q_rank = cw["inner"], cw["conv_dim"], cw["q_rank"]
    n_pos = cos.shape[0] // tm
    row = lambda i: (i, 0)
    head = lambda i: (0, i, 0)
    dims = {k: cw[k] for k in ("nh", "qk", "nope", "rope", "vd", "q_rank", "kv_rank", "inner", "conv_dim")}
    return pl.pallas_call(
        functools.partial(_cd_inproj_kernel, **dims),
        grid=(m // tm,),
        in_specs=[pl.BlockSpec((tm, d), row), _full((1, d)), _full(cw["w_pad"].shape), _full((1, q_rank)),
                  _full((1, kv_rank)), _full(cw["wqm"].shape), _full(cw["wqp"].shape), _full(cw["wk128"].shape),
                  _full(cw["wv"].shape), _full((1, LANES)), _full((1, LANES)),
                  pl.BlockSpec((tm, LANES), lambda i: (i % n_pos, 0)),
                  pl.BlockSpec((tm, LANES), lambda i: (i % n_pos, 0))],
        out_specs=[pl.BlockSpec((nh, tm, LANES), head), pl.BlockSpec((nh, tm, LANES), head),
                   pl.BlockSpec((nh, tm, vd), head), pl.BlockSpec((tm, kv_rank + rope), row),
                   pl.BlockSpec((tm, inner), row), pl.BlockSpec((tm, conv_dim), row),
                   pl.BlockSpec((tm, LANES), row)],
        out_shape=[jax.ShapeDtypeStruct((nh, m, LANES), BF16), jax.ShapeDtypeStruct((nh, m, LANES), BF16),
                   jax.ShapeDtypeStruct((nh, m, vd), BF16), jax.ShapeDtypeStruct((m, kv_rank + rope), F32),
                   jax.ShapeDtypeStruct((m, inner), F32), jax.ShapeDtypeStruct((m, conv_dim), F32),
                   jax.ShapeDtypeStruct((m, LANES), F32)],
        compiler_params=_cparams("parallel"),
    )(x, g.reshape(1, d), cw["w_pad"], gql.reshape(1, q_rank), gkl.reshape(1, kv_rank), cw["wqm"], cw["wqp"],
      cw["wk128"], cw["wv"], cw["gq128"], cw["gk128"], cos, sin)


_MLA_DEPTH = 3
_MLA_GROUP = 4


def _mla_prompt_kernel(qi_ref, kb_ref, kind_ref, q_ref, k_ref, v_ref, bias_ref, o_ref,
                       s_ring, p_ring, pv_ring, a_ring, linv_ring, m_scr, l_scr, acc_scr, *, tq, n_steps):
    halves = [slice(c * (tq // 2), (c + 1) * (tq // 2)) for c in range(2)]
    vd = v_ref.shape[-1]

    def blk(idx):
        return pl.ds(pl.multiple_of(idx * tq, tq), tq)

    for ring in (s_ring, p_ring, pv_ring, a_ring, linv_ring, l_scr, acc_scr):
        ring[...] = jnp.zeros_like(ring)
    m_scr[...] = jnp.full(m_scr.shape, -jnp.inf, F32)

    def step(t, ph):
        acc = a_ring[(ph + 1) % 4, :, :vd] * acc_scr[...] + pv_ring[(ph + 1) % 2]
        acc_scr[...] = acc
        o_ref[blk(qi_ref[t]), :] = (acc * linv_ring[(ph + 1) % 4, :, :vd]).astype(o_ref.dtype)
        v = v_ref[blk(kb_ref[t + 1]), :]
        outs = [_dot(p_ring[ph % 2, h, :], v) for h in halves]
        for h, o in zip(halves, outs):
            pv_ring[ph % 2, h, :] = o
        kind = kind_ref[t + 2]
        tiles = [slice(c * LANES, (c + 1) * LANES) for c in range(tq // LANES)]
        s = [s_ring[(ph + 1) % 2, :, c] + bias_ref[kind, :, c] for c in tiles]
        m_prev = jnp.where(kind == _SB_DIAG, -jnp.inf, m_scr[...])
        m_blk = jnp.max(functools.reduce(jnp.maximum, s), axis=-1, keepdims=True)
        m_new = jnp.maximum(m_prev, jnp.broadcast_to(m_blk, m_prev.shape))
        alpha = jnp.exp2(m_prev - m_new)
        p = [jnp.exp2(sc - m_new) for sc in s]
        l_blk = jnp.sum(functools.reduce(jnp.add, p), axis=-1, keepdims=True)
        l = alpha * l_scr[...] + jnp.broadcast_to(l_blk, m_prev.shape)
        m_scr[...] = m_new
        l_scr[...] = l
        a_ring[(ph + 3) % 4] = alpha
        linv_ring[(ph + 3) % 4] = 1.0 / l
        for c, pc in zip(tiles, p):
            p_ring[(ph + 1) % 2, :, c] = pc.astype(BF16)
        k = k_ref[blk(kb_ref[t + 3]), :]
        qrows = blk(qi_ref[t + 3])
        ss = [lax.dot_general(q_ref[qrows, :][h], k, _NT, preferred_element_type=F32) for h in halves]
        for h, sc in zip(halves, ss):
            s_ring[ph % 2, h, :] = sc

    def step_group(j, carry):
        for ph in range(_MLA_GROUP):
            step(_MLA_GROUP * j + ph, ph % 4)
        return carry

    lax.fori_loop(0, n_steps // _MLA_GROUP, step_group, 0)


def _mla_prompt(qh, kh, vh, *, nb, tq):
    nh, m, dk = qh.shape
    vd = vh.shape[-1]
    t = m // nb
    n_blk = t // tq
    items = [(i, i if s == 0 else s - 1, _SB_DIAG if s == 0 else _SB_FULL) for i in range(n_blk) for s in range(i + 1)]
    n_steps = -(-(len(items) + _MLA_DEPTH) // _MLA_GROUP) * _MLA_GROUP
    items = [(0, 0, _SB_IDLE)] * _MLA_DEPTH + items
    items = items + [(n_blk - 1, 0, _SB_IDLE)] * (n_steps + _MLA_DEPTH - len(items))
    qi_tab, kb_tab, kind_tab = (jnp.asarray([it[c] for it in items], jnp.int32) for c in range(3))
    causal = lax.broadcasted_iota(jnp.int32, (tq, tq), 1) <= lax.broadcasted_iota(jnp.int32, (tq, tq), 0)
    bias = jnp.stack([jnp.where(causal, 0.0, _MASKED), jnp.zeros((tq, tq), F32), jnp.full((tq, tq), _MASKED, F32)])
    per_seq = lambda b, h, *_: (h, b, 0)
    grid_spec = pltpu.PrefetchScalarGridSpec(
        num_scalar_prefetch=3, grid=(nb, nh),
        in_specs=[pl.BlockSpec((None, t, dk), per_seq), pl.BlockSpec((None, t, dk), per_seq),
                  pl.BlockSpec((None, t, vd), per_seq), pl.BlockSpec((3, tq, tq), lambda b, h, *_: (0, 0, 0))],
        out_specs=pl.BlockSpec((None, t, vd), per_seq),
        scratch_shapes=[pltpu.VMEM((2, tq, tq), F32), pltpu.VMEM((2, tq, tq), BF16), pltpu.VMEM((2, tq, vd), F32),
                        pltpu.VMEM((4, tq, LANES), F32), pltpu.VMEM((4, tq, LANES), F32), pltpu.VMEM((tq, LANES), F32),
                        pltpu.VMEM((tq, LANES), F32), pltpu.VMEM((tq, vd), F32)])
    return pl.pallas_call(
        functools.partial(_mla_prompt_kernel, tq=tq, n_steps=n_steps),
        grid_spec=grid_spec,
        out_shape=jax.ShapeDtypeStruct((nh, m, vd), BF16),
        compiler_params=_cparams("parallel", "parallel"),
        name="mla_prompt",
    )(qi_tab, kb_tab, kind_tab, qh, kh, vh, bias)


def _causal_conv_tile(x_ref, xpad, cw_ref, cb_ref, cbuf_ref, rows):
    pad = SUBLANES
    xpad[pad:pad + rows, :] = x_ref[...]
    xc = cb_ref[...]
    for i in range(CONV_WIDTH):
        o = pad - (CONV_WIDTH - 1) + i
        xc = xc + xpad[o:o + rows, :] * cw_ref[i:i + 1, :]
    cbuf_ref[...] = xpad[pad + rows - (CONV_WIDTH - 1):pad + rows, :]
    xpad[0:pad, :] = xpad[rows:rows + pad, :]
    return xc


def _gated_group_norm(y, z, ng_ref, y_ref, n_groups):
    y = y * _silu(z)
    gs = y.shape[-1] // n_groups
    for g in range(n_groups):
        y_ref[:, g * gs:(g + 1) * gs] = _rms(y[:, g * gs:(g + 1) * gs], ng_ref[:, g * gs:(g + 1) * gs]).astype(y_ref.dtype)


def _ssd_prompt_kernel(xbc_ref, z_ref, dt_ref, cw_ref, cb_ref, dtb_ref, alog_ref, dsk_ref, ng_ref, ltri_ref,
                       y_ref, hlast_ref, cbuf_ref, xpad, state, yscr, *, q, nh, hd, ns, n_groups):
    c = pl.program_id(1)

    @pl.when(c == 0)
    def _():
        state[...] = jnp.zeros_like(state)
        xpad[0:SUBLANES, :] = jnp.zeros((SUBLANES, xpad.shape[1]), F32)

    xc = _silu(_causal_conv_tile(xbc_ref, xpad, cw_ref, cb_ref, cbuf_ref, q))
    inner = nh * hd
    hpg = nh // n_groups
    xs = xc[:, :inner]
    bm = xc[:, inner:inner + n_groups * ns]
    cm = xc[:, inner + n_groups * ns:]
    lane = lax.broadcasted_iota(jnp.int32, (1, LANES), 1)
    dtv = _softplus(dt_ref[...] + dtb_ref[...])
    a = jnp.where(lane < nh, -jnp.exp(alog_ref[...]), 0.0)
    ltri = ltri_ref[...]
    cum = sum(_dot(ltri, part) for part in _split3(dtv * a))
    cum_t = cum.T
    dt_t = dtv.T
    cum_last = cum[q - 1:q, :]
    to_end = jnp.exp(cum_last - cum) * dtv
    ecum = jnp.exp(cum)
    elast = jnp.exp(cum_last)
    causal = lax.broadcasted_iota(jnp.int32, (q, q), 0) >= lax.broadcasted_iota(jnp.int32, (q, q), 1)
    for g in range(n_groups):
        cmg = cm[:, g * ns:(g + 1) * ns].astype(BF16)
        bmg = bm[:, g * ns:(g + 1) * ns].astype(BF16)
        cb = lax.dot_general(cmg, bmg, _NT, preferred_element_type=F32)
        for r in range(g * hpg, (g + 1) * hpg):
            seg = cum[:, r:r + 1] - cum_t[r:r + 1, :]
            decay = jnp.exp(jnp.where(causal, seg, -jnp.inf))
            mix = cb * decay * dt_t[r:r + 1, :]
            xh = xs[:, r * hd:(r + 1) * hd]
            hprev = state[r]
            y = _dot(mix.astype(BF16), xh.astype(BF16))
            y = y + lax.dot_general(cmg, hprev.astype(BF16), _NT, preferred_element_type=F32) * ecum[:, r:r + 1]
            yscr[:, r * hd:(r + 1) * hd] = y + dsk_ref[:, r * hd:(r + 1) * hd] * xh
            xw = (xh * to_end[:, r:r + 1]).astype(BF16)
            state[r] = elast[:, r:r + 1] * hprev + lax.dot_general(xw, bmg, _TN, preferred_element_type=F32)
    hlast_ref[...] = state[...]
    _gated_group_norm(yscr[...], z_ref[...], ng_ref, y_ref, n_groups)


def _lower_tri_ones(n):
    r = lax.broadcasted_iota(jnp.int32, (n, n), 0)
    c = lax.broadcasted_iota(jnp.int32, (n, n), 1)
    return (r >= c).astype(BF16)


def _ssd_prompt(xbc, z, dt, conv_w, conv_b, cw, norm_g, *, nb, q, state_dim):
    m, conv_dim = xbc.shape
    inner, nh = cw["inner"], cw["n_ssd"]
    hd = inner // nh
    n_groups = (conv_dim - inner) // (2 * state_dim)
    t = m // nb
    nc = t // q
    row = lambda b, c: (b * nc + c, 0)
    return pl.pallas_call(
        functools.partial(_ssd_prompt_kernel, q=q, nh=nh, hd=hd, ns=state_dim, n_groups=n_groups),
        grid=(nb, nc),
        in_specs=[pl.BlockSpec((q, conv_dim), row), pl.BlockSpec((q, inner), row), pl.BlockSpec((q, LANES), row),
                  _full((CONV_WIDTH, conv_dim)), _full((1, conv_dim)), _full((1, LANES)), _full((1, LANES)),
                  _full((1, inner)), _full((1, inner)), _full((q, q))],
        out_specs=[pl.BlockSpec((q, inner), row),
                   pl.BlockSpec((None, nh, hd, state_dim), lambda b, c: (b, 0, 0, 0)),
                   pl.BlockSpec((None, CONV_WIDTH - 1, conv_dim), lambda b, c: (b, 0, 0))],
        out_shape=[jax.ShapeDtypeStruct((m, inner), BF16), jax.ShapeDtypeStruct((nb, nh, hd, state_dim), F32),
                   jax.ShapeDtypeStruct((nb, CONV_WIDTH - 1, conv_dim), F32)],
        scratch_shapes=[pltpu.VMEM((q + SUBLANES, conv_dim), F32), pltpu.VMEM((nh, hd, state_dim), F32),
                        pltpu.VMEM((q, inner), F32)],
        compiler_params=_cparams("parallel", "arbitrary"),
        name="ssd_prompt",
    )(xbc, z, dt, conv_w, conv_b.reshape(1, conv_dim), cw["dtb128"], cw["alog128"], cw["dsk"],
      norm_g.reshape(1, inner), _lower_tri_ones(q))


def _sb_sample_kernel(pt_ref, qbd_ref, uu_ref, *rest, npg, dk, grp):
    pages = rest[:npg]
    o_ref, z_scr, r_scr, acc_scr, wb_scr = rest[npg:]
    c = pl.program_id(1)
    nrow = qbd_ref.shape[0]
    n_kv = nrow // grp
    hd = dk // n_kv
    keys = z_scr.shape[1]

    @pl.when(c == 0)
    def _():
        r_scr[...] = jnp.zeros_like(r_scr)
        acc_scr[...] = jnp.zeros_like(acc_scr)

    qbd = qbd_ref[...]
    for j in range(npg):
        z_scr[j * nrow:(j + 1) * nrow, :] = _dot(qbd, pages[j][:dk, :].astype(BF16))
    nz = z_scr[...]
    hi, lo = _split2(_sb_log2_keep(nz))
    tl = _dot(jnp.concatenate([hi, lo], axis=1), uu_ref[...])
    tot = jnp.broadcast_to(tl[:, 0:1], tl.shape)
    r = r_scr[...]
    carries = [None] * npg
    for j in reversed(range(npg)):
        carries[j] = r
        r = r + tot[j * nrow:(j + 1) * nrow, :]
    r_scr[...] = r
    z_scr[...] = jnp.exp2(tl + jnp.concatenate(carries, axis=0) - nz)
    sub = wb_scr.shape[1]
    for r in range(npg * nrow):
        wb_scr[r] = jnp.broadcast_to(z_scr[r:r + 1, :], (sub, keys))
    for rg in range(dk // sub):
        h = rg * sub // hd
        rows = slice(rg * sub, (rg + 1) * sub)
        acc = [acc_scr[g, rows, :] for g in range(grp)]
        for j in range(npg):
            vt = pages[j][dk + rg * sub:dk + (rg + 1) * sub, :]
            for g in range(grp):
                acc[g] = acc[g] + vt * wb_scr[j * nrow + g * n_kv + h]
        for g in range(grp):
            acc_scr[g, rows, :] = acc[g]

    @pl.when(c == pl.num_programs(1) - 1)
    def _():
        for g in range(grp):
            o_ref[g:g + 1, :] = jnp.sum(acc_scr[g].T, axis=0, keepdims=True)


def _sb_sample(qh, cache_t, layer, page_table, *, n_kv, npg=32):
    n_q, s, hd = qh.shape
    grp = n_q // n_kv
    dk = n_kv * hd
    page = cache_t.shape[3]
    n_pages = page_table.shape[1]
    npg = min(npg, n_pages)
    nch = n_pages // npg
    eye = jnp.eye(n_kv, dtype=qh.dtype)
    qbd = jnp.einsum('hgsd,hk->sghkd', qh.reshape(n_kv, grp, s, hd), eye).reshape(s, n_q, dk)

    def page_spec(j):
        return pl.BlockSpec((None, None, 2 * dk, page),
                            lambda i, c, pt: (layer, pt[i, (nch - 1 - c) * npg + j], 0, 0))

    grid_spec = pltpu.PrefetchScalarGridSpec(
        num_scalar_prefetch=1, grid=(s, nch),
        in_specs=[pl.BlockSpec((None, n_q, dk), lambda i, c, pt: (i, 0, 0)),
                  pl.BlockSpec((2 * page, page), lambda i, c, pt: (0, 0))] + [page_spec(j) for j in range(npg)],
        out_specs=pl.BlockSpec((None, grp, dk), lambda i, c, pt: (i, 0, 0)),
        scratch_shapes=[pltpu.VMEM((npg * n_q, page), F32), pltpu.VMEM((n_q, page), F32),
                        pltpu.VMEM((grp, dk, page), F32), pltpu.VMEM((npg * n_q, SUBLANES, page), F32)])
    og = pl.pallas_call(
        functools.partial(_sb_sample_kernel, npg=npg, dk=dk, grp=grp),
        grid_spec=grid_spec,
        out_shape=jax.ShapeDtypeStruct((s, grp, dk), F32),
        compiler_params=_cparams("parallel", "arbitrary"),
        name="sb_sample",
    )(page_table, qbd, _rev_cumsum_ones(page), *([cache_t] * npg))
    o = og.reshape(s, grp, n_kv, hd)
    return jnp.transpose(o, (2, 1, 0, 3)).reshape(n_q, s, hd).astype(BF16)


def _conv_step(x, buf_ref, nbuf_ref, cw_ref, cb_ref):
    xc = cb_ref[...]
    for i in range(CONV_WIDTH - 1):
        xc = xc + buf_ref[i] * cw_ref[i:i + 1, :]
        if i > 0:
            nbuf_ref[i - 1] = buf_ref[i]
    nbuf_ref[CONV_WIDTH - 2] = x
    return xc + x * cw_ref[CONV_WIDTH - 1:CONV_WIDTH, :]


def _rglru_step_kernel(xb_ref, gate_ref, buf_ref, h0_ref, cw_ref, cb_ref, wbd_ref, bias_ref, lam_ref,
                       y_ref, h_ref, nbuf_ref):
    xc = _conv_step(xb_ref[...], buf_ref, nbuf_ref, cw_ref, cb_ref)
    a, b = _lru_gates(xc, wbd_ref, bias_ref, lam_ref)
    h = a * h0_ref[...] + b
    h_ref[...] = h
    y_ref[...] = (_gelu_tanh(gate_ref[...]) * h).astype(y_ref.dtype)


def _rglru_step(xb, gate, buf, h0, conv_w, conv_b, wbd, bias, lam):
    s, w = xb.shape
    return pl.pallas_call(
        _rglru_step_kernel,
        out_shape=[jax.ShapeDtypeStruct((s, w), BF16), jax.ShapeDtypeStruct((s, w), F32),
                   jax.ShapeDtypeStruct((CONV_WIDTH - 1, s, w), F32)],
        compiler_params=pltpu.CompilerParams(vmem_limit_bytes=VMEM_LIMIT_BYTES),
    )(xb, gate, buf, h0, conv_w, conv_b.reshape(1, w), wbd, bias, lam.reshape(1, w))


def _mla_sample_kernel(pt_ref, qn_ref, qr_ref, new_ref, wkt_ref, wv_ref, *rest,
                       n_pages, page, kv_rank, rope, qk, nh, ppi):
    pages = rest[:n_pages]
    o_ref, ckv_scr, kpe_scr, s_scr = rest[n_pages:]
    n_iter = n_pages // ppi
    width = ppi * page
    nope = wkt_ref.shape[0] // nh
    for j in range(n_pages):
        lanes = slice((j % ppi) * page, (j % ppi + 1) * page)
        ckv_scr[j // ppi, :, lanes] = pages[j][:kv_rank, :].astype(BF16)
        kpe_scr[j // ppi, :, lanes] = pages[j][kv_rank:, :]
    qr = qr_ref[...]
    wkt = wkt_ref[...]
    q_lat = _dot(qn_ref[...], wkt).astype(BF16)

    def scores(kn, ct, kp):
        kn = kn.reshape(nh, nope, kn.shape[-1])
        ssq = jnp.sum(kn * kn, axis=1) + jnp.sum(kp * kp, axis=0, keepdims=True)
        s = _dot(q_lat, ct) + _dot(qr, kp.astype(BF16))
        return s * lax.rsqrt(ssq * (1.0 / qk) + NORM_EPS)

    new_t = jnp.broadcast_to(jnp.concatenate([new_ref[...], jnp.zeros((1, 2 * LANES - kv_rank - rope), F32)], axis=1),
                             (LANES, 2 * LANES)).T
    ct_new = new_t[:kv_rank, :].astype(BF16)
    first = lax.broadcasted_iota(jnp.int32, (nh, LANES), 1) == 0
    s_new = jnp.where(first, scores(_dot(wkt, ct_new), ct_new, new_t[kv_rank:kv_rank + rope, :]), -jnp.inf)

    def score_pass(i, m):
        ct = ckv_scr[i]
        s = scores(_dot(wkt, ct), ct, kpe_scr[i])
        s_scr[i] = s
        return jnp.maximum(m, jnp.max(s, axis=1, keepdims=True))

    unroll = max(c for c in (1, 2, 4, 8) if n_iter % c == 0)
    m = lax.fori_loop(0, n_iter, score_pass, jnp.max(s_new, axis=1, keepdims=True), unroll=unroll)
    p_new = jnp.exp2(s_new - m)

    def value_pass(i, c):
        l, acc_t = c
        p = jnp.exp2(s_scr[i] - m)
        acc_t = acc_t + lax.dot_general(ckv_scr[i], p.astype(BF16), _NT, preferred_element_type=F32)
        return l + jnp.sum(p, axis=1, keepdims=True), acc_t

    l0 = jnp.sum(p_new, axis=1, keepdims=True)
    acc0 = lax.dot_general(ct_new, p_new.astype(BF16), _NT, preferred_element_type=F32)
    l, acc_t = lax.fori_loop(0, n_iter, value_pass, (l0, acc0), unroll=unroll)
    acc = jnp.concatenate([acc_t, jnp.zeros((kv_rank, LANES - nh), F32)], axis=1).T[:nh, :]
    hi, lo = _split2(acc / l)
    o_ref[...] = _dot(hi, wv_ref[...]) + _dot(lo, wv_ref[...])


def _mla_sample(qh, lat_new, cache, layer, page_table, cw):
    nh, s, _ = qh.shape
    nope, rope, vd, kv_rank, qk = cw["nope"], cw["rope"], cw["vd"], cw["kv_rank"], cw["qk"]
    page = cache.shape[3]
    n_pages = page_table.shape[1]
    ppi = max(c for c in (1, 2, 4) if n_pages % c == 0)
    qg = jnp.swapaxes(qh.astype(F32) * cw["gk128"].reshape(1, 1, LANES), 0, 1)
    qn = jnp.einsum('shd,hk->shkd', qg[:, :, :nope], jnp.eye(nh, dtype=F32)).reshape(s, nh, nh * nope).astype(BF16)
    qr = qg[:, :, nope:qk].astype(BF16)
    new = lat_new.reshape(s, 1, kv_rank + rope)

    def page_spec(j):
        return pl.BlockSpec((None, None, kv_rank + rope, page), lambda i, pt: (layer, pt[i, j], 0, 0))

    grid_spec = pltpu.PrefetchScalarGridSpec(
        num_scalar_prefetch=1, grid=(s,),
        in_specs=[pl.BlockSpec((None, nh, nh * nope), lambda i, pt: (i, 0, 0)),
                  pl.BlockSpec((None, nh, rope), lambda i, pt: (i, 0, 0)),
                  pl.BlockSpec((None, 1, kv_rank + rope), lambda i, pt: (i, 0, 0)),
                  pl.BlockSpec(cw["wkt"].shape, lambda i, pt: (0, 0)),
                  pl.BlockSpec(cw["wv"].shape, lambda i, pt: (0, 0))] + [page_spec(j) for j in range(n_pages)],
        out_specs=pl.BlockSpec((None, nh, nh * vd), lambda i, pt: (i, 0, 0)),
        scratch_shapes=[pltpu.VMEM((n_pages // ppi, kv_rank, ppi * page), BF16),
                        pltpu.VMEM((n_pages // ppi, rope, ppi * page), F32),
                        pltpu.VMEM((n_pages // ppi, nh, ppi * page), F32)])
    om = pl.pallas_call(
        functools.partial(_mla_sample_kernel, n_pages=n_pages, page=page, kv_rank=kv_rank, rope=rope, qk=qk, nh=nh,
                          ppi=ppi),
        grid_spec=grid_spec,
        out_shape=jax.ShapeDtypeStruct((s, nh, nh * vd), F32),
        compiler_params=_cparams("parallel"),
        name="mla_sample",
    )(page_table, qn, qr, new, cw["wkt"], cw["wv"], *([cache] * n_pages))
    idx = jnp.arange(nh)
    o = om.reshape(s, nh, nh, vd)[:, idx, idx, :]
    return jnp.swapaxes(o, 0, 1).astype(BF16)


def _ssd_step_kernel(xbc_ref, z_ref, dt_ref, buf_ref, h0_ref, cw_ref, cb_ref, dtb_ref, alog_ref, dsk_ref, ng_ref,
                     y_ref, hnew_ref, nbuf_ref, xc_scr, xct_scr, dtt_scr, dat_scr, yoff_scr,
                     *, nh, hd, ns, n_groups):
    r = pl.program_id(0)
    inner = nh * hd
    hpg = nh // n_groups
    lane = lax.broadcasted_iota(jnp.int32, (1, LANES), 1)

    @pl.when(r == 0)
    def _():
        xc = _silu(_conv_step(xbc_ref[...], buf_ref, nbuf_ref, cw_ref, cb_ref))
        xc_scr[...] = xc
        xct_scr[...] = xc.T
        dtv = _softplus(dt_ref[...] + dtb_ref[...])
        a = jnp.where(lane < nh, -jnp.exp(alog_ref[...]), 0.0)
        dtt_scr[...] = dtv.T
        dat_scr[...] = jnp.exp(dtv * a).T

    g = r // hpg
    xt = xct_scr[pl.ds(pl.multiple_of(r * hd, hd), hd), :]
    bt = xct_scr[pl.ds(pl.multiple_of(inner + g * ns, ns), ns), :]
    ct = xct_scr[pl.ds(pl.multiple_of(inner + (n_groups + g) * ns, ns), ns), :]
    dar = dat_scr[pl.ds(r, 1), :]
    coef = xt * dtt_scr[pl.ds(r, 1), :]
    for p in range(hd):
        h0p = h0_ref[p * ns:(p + 1) * ns, :]
        hnew_ref[p * ns:(p + 1) * ns, :] = dar * h0p + coef[p:p + 1, :] * bt
        yoff_scr[pl.ds(r * hd + p, 1), :] = jnp.sum(ct * h0p, axis=0, keepdims=True)

    @pl.when(r == nh - 1)
    def _():
        xc = xc_scr[...]
        xs = xc[:, :inner]
        dtv = dtt_scr[...].T
        da = dat_scr[...].T
        per_head = lambda v: jnp.concatenate(
            [jnp.broadcast_to(v[:, i:i + 1], (v.shape[0], hd)) for i in range(nh)], axis=1)
        cb = []
        for gi in range(n_groups):
            bmg = xc[:, inner + gi * ns:inner + (gi + 1) * ns]
            cmg = xc[:, inner + (n_groups + gi) * ns:inner + (n_groups + gi + 1) * ns]
            cb.append(jnp.broadcast_to(jnp.sum(cmg * bmg, axis=-1, keepdims=True), (xc.shape[0], hpg * hd)))
        y = jnp.concatenate(cb, axis=1) * per_head(dtv) * xs + yoff_scr[...].T * per_head(da) + dsk_ref[...] * xs
        _gated_group_norm(y, z_ref[...], ng_ref, y_ref, n_groups)


def _ssd_step(xbc, z, dt, buf, h0, conv_w, conv_b, cw, norm_g):
    s, conv_dim = xbc.shape
    _, nh, hd, ns = h0.shape
    inner = cw["inner"]
    n_groups = (conv_dim - inner) // (2 * ns)
    sz = hd * ns
    h0t = jnp.transpose(h0, (1, 2, 3, 0)).reshape(nh * sz, s)
    y, hnew, nbuf = pl.pallas_call(
        functools.partial(_ssd_step_kernel, nh=nh, hd=hd, ns=ns, n_groups=n_groups),
        grid=(nh,),
        in_specs=[_full((s, conv_dim)), _full((s, inner)), _full((s, LANES)), _full((CONV_WIDTH - 1, s, conv_dim)),
                  pl.BlockSpec((sz, s), lambda r: (r, 0)), _full((CONV_WIDTH, conv_dim)), _full((1, conv_dim)),
                  _full((1, LANES)), _full((1, LANES)), _full((1, inner)), _full((1, inner))],
        out_specs=[_full((s, inner)), pl.BlockSpec((sz, s), lambda r: (r, 0)),
                   _full((CONV_WIDTH - 1, s, conv_dim))],
        out_shape=[jax.ShapeDtypeStruct((s, inner), BF16), jax.ShapeDtypeStruct((nh * sz, s), F32),
                   jax.ShapeDtypeStruct((CONV_WIDTH - 1, s, conv_dim), F32)],
        scratch_shapes=[pltpu.VMEM((s, conv_dim), F32), pltpu.VMEM((conv_dim, s), F32), pltpu.VMEM((LANES, s), F32),
                        pltpu.VMEM((LANES, s), F32), pltpu.VMEM((inner, s), F32)],
        compiler_params=_cparams("arbitrary"),
        name="ssd_step",
    )(xbc, z, dt, buf, h0t, conv_w, conv_b.reshape(1, conv_dim), cw["dtb128"], cw["alog128"],
      cw["dsk"], norm_g.reshape(1, inner))
    return y, jnp.transpose(hnew.reshape(nh, hd, ns, s), (3, 0, 1, 2)), nbuf


def _prompt_tiles(t, hidden):
    pick = lambda want: max(c for c in (8, 16, 32, 64, 128, 256, 512, 1024) if c <= want and t % c == 0)
    th = hidden // 2 if hidden % (2 * LANES) == 0 else hidden
    return dict(tm_proj=pick(512), tm_post=pick(512), tq_sb=pick(256), tq_mla=pick(512), tt_lru=pick(512),
                ssd_chunk=pick(128), th=th)


def kernel(x_prompt, x_sample, cache_sb_kv, cache_mla_kv, page_table, state_lru_h, state_lru_conv, state_ssm_h, state_ssm_conv, norm_mix, norm_ffn, ab_w_in, ab_w_out, sb_q_gain, sb_k_gain, lru_conv_w, lru_conv_b, lru_wa, lru_ba, lru_wx, lru_bx, lru_lambda, cd_w_in, cd_w_out, mla_q_lat_gain, mla_w_uq, mla_kv_lat_gain, mla_w_ukv, mla_q_gain, mla_k_gain, ssd_conv_w, ssd_conv_b, ssd_dt_bias, ssd_a_log, ssd_d, ssd_norm_gain, ffn_w_in, ffn_w_out):
    nb, t, d = x_prompt.shape
    ns = x_sample.shape[0]
    depth = norm_mix.shape[0]
    page = cache_sb_kv.shape[2]
    past = page_table.shape[1] * page
    n_kv, hd = cache_sb_kv.shape[4], cache_sb_kv.shape[5]
    n_q = (ab_w_in.shape[2] - 2 * n_kv * hd - 2 * lru_lambda.shape[1]) // hd
    w_lru = lru_lambda.shape[1]
    sizes = _prompt_tiles(t, ffn_w_out.shape[1])

    xp = x_prompt.reshape(nb * t, d)
    xs = x_sample.reshape(ns, d)
    cache_sb = jnp.transpose(cache_sb_kv, (0, 1, 3, 4, 5, 2)).reshape(cache_sb_kv.shape[:2] + (2 * n_kv * hd, page))
    cache_mla = jnp.transpose(cache_mla_kv, (0, 1, 3, 2))
    cos_p, sin_p = _rope_tables(jnp.arange(t), mla_q_gain.shape[1] - (cache_mla_kv.shape[3] - mla_kv_lat_gain.shape[1]),
                                cache_mla_kv.shape[3] - mla_kv_lat_gain.shape[1])
    cos_s, sin_s = _rope_tables(jnp.full((ns,), past), mla_q_gain.shape[1] - (cache_mla_kv.shape[3] - mla_kv_lat_gain.shape[1]),
                                cache_mla_kv.shape[3] - mla_kv_lat_gain.shape[1])
    outs = {k: [] for k in ("sb_p", "sb_s", "lh_p", "lh_s", "lc_p", "lc_s", "ml_p", "ml_s", "sh_p", "sh_s", "sc_p", "sc_s")}
    for li in range(depth):
        w_ffn_in = ffn_w_in[li].astype(BF16)
        w_ffn_out = ffn_w_out[li].astype(BF16)
        if li % 2 == 0:
            e = li // 2
            w_in = ab_w_in[e].astype(BF16)
            w_out = ab_w_out[e].astype(BF16)
            wbd, bias = _lru_weights(lru_wa[e], lru_ba[e], lru_wx[e], lru_bx[e])
            qh, kh, vh, kv, xb, gate = _ab_inproj(xp, norm_mix[li], w_in, sb_q_gain[e], sb_k_gain[e],
                                                  nq=n_q, nk=n_kv, hd=hd, w_lru=w_lru, tm=sizes["tm_proj"])
            oh = _sb_prompt(qh, kh, vh, nb=nb, tq=sizes["tq_sb"])
            y2, hl, cbuf = _rglru_prompt(xb, gate, lru_conv_w[e], lru_conv_b[e], wbd, bias, lru_lambda[e],
                                         nb=nb, tt=sizes["tt_lru"])
            xp = _post(xp, oh, y2, w_out, norm_ffn[li], w_ffn_in, w_ffn_out, tm=sizes["tm_post"], th=sizes["th"])
            outs["sb_p"].append(kv.reshape(nb, t, 2, n_kv, hd))
            outs["lh_p"].append(hl.reshape(nb, w_lru))
            outs["lc_p"].append(cbuf)
            qh, kh, vh, kv, xb, gate = _ab_inproj(xs, norm_mix[li], w_in, sb_q_gain[e], sb_k_gain[e],
                                                  nq=n_q, nk=n_kv, hd=hd, w_lru=w_lru, tm=ns)
            oh = _sb_sample(qh, cache_sb, e, page_table, n_kv=n_kv)
            y2, hl, cbuf = _rglru_step(xb, gate, jnp.swapaxes(state_lru_conv[e], 0, 1), state_lru_h[e],
                                       lru_conv_w[e], lru_conv_b[e], wbd, bias, lru_lambda[e])
            xs = _post(xs, oh, y2, w_out, norm_ffn[li], w_ffn_in, w_ffn_out, tm=ns, th=sizes["th"])
            outs["sb_s"].append(kv.reshape(ns, 1, 2, n_kv, hd))
            outs["lh_s"].append(hl)
            outs["lc_s"].append(jnp.swapaxes(cbuf, 0, 1))
        else:
            o = li // 2
            cw = _cd_weights(cd_w_in[o], mla_w_uq[o], mla_w_ukv[o], mla_q_gain[o], mla_k_gain[o], ssd_dt_bias[o],
                             ssd_a_log[o], ssd_d[o], kv_rank=mla_kv_lat_gain.shape[1], q_rank=mla_q_lat_gain.shape[1],
                             rope=cache_mla_kv.shape[3] - mla_kv_lat_gain.shape[1], inner=ssd_norm_gain.shape[1],
                             conv_dim=ssd_conv_w.shape[2])
            w_out = cd_w_out[o].astype(BF16)
            qh, kh, vh, lat, z, xbc, dt = _cd_inproj(xp, norm_mix[li], cw, mla_q_lat_gain[o], mla_kv_lat_gain[o],
                                                     cos_p, sin_p, tm=sizes["tm_proj"])
            oh = _mla_prompt(qh, kh, vh, nb=nb, tq=sizes["tq_mla"])
            y2, hl, cbuf = _ssd_prompt(xbc, z, dt, ssd_conv_w[o], ssd_conv_b[o], cw, ssd_norm_gain[o],
                                       nb=nb, q=sizes["ssd_chunk"], state_dim=state_ssm_h.shape[4])
            xp = _post(xp, oh, y2, w_out, norm_ffn[li], w_ffn_in, w_ffn_out, tm=sizes["tm_post"], th=sizes["th"])
            outs["ml_p"].append(lat.reshape(nb, t, lat.shape[1]))
            outs["sh_p"].append(hl)
            outs["sc_p"].append(cbuf)
            qh, kh, vh, lat, z, xbc, dt = _cd_inproj(xs, norm_mix[li], cw, mla_q_lat_gain[o], mla_kv_lat_gain[o],
                                                     cos_s, sin_s, tm=ns)
            oh = _mla_sample(qh, lat, cache_mla, o, page_table, cw)
            y2, hl, cbuf = _ssd_step(xbc, z, dt, jnp.swapaxes(state_ssm_conv[o], 0, 1), state_ssm_h[o],
                                     ssd_conv_w[o], ssd_conv_b[o], cw, ssd_norm_gain[o])
            xs = _post(xs, oh, y2, w_out, norm_ffn[li], w_ffn_in, w_ffn_out, tm=ns, th=sizes["th"])
            outs["ml_s"].append(lat.reshape(ns, 1, lat.shape[1]))
            outs["sh_s"].append(hl)
            outs["sc_s"].append(jnp.swapaxes(cbuf, 0, 1))
    st = {k: jnp.stack(v) for k, v in outs.items()}
    return (xp.reshape(nb, t, d), xs.reshape(ns, 1, d), st["sb_p"], st["sb_s"], st["lh_p"], st["lh_s"], st["lc_p"],
            st["lc_s"], st["ml_p"], st["ml_s"], st["sh_p"], st["sh_s"], st["sc_p"], st["sc_s"])
```

```python
import functools
import math

import jax
import jax.numpy as jnp
from jax import lax
from jax.experimental import pallas as pl
from jax.experimental.pallas import tpu as pltpu

F32 = jnp.float32
BF16 = jnp.bfloat16
NORM_EPS = 1e-6
LRU_C = 8.0
ROPE_THETA = 10000.0
CONV_WIDTH = 4
LANES = 128
SUBLANES = 8
VMEM_LIMIT_BYTES = 56 * 1024 * 1024
_NT = (((1,), (1,)), ((), ()))
_TN = (((0,), (0,)), ((), ()))


def _cparams(*sem):
    return pltpu.CompilerParams(dimension_semantics=sem, vmem_limit_bytes=VMEM_LIMIT_BYTES)


def _dot(a, b):
    return jnp.dot(a, b, preferred_element_type=F32)


def _split2(x):
    hi = lax.bitcast_convert_type(lax.bitcast_convert_type(x, jnp.uint32) & jnp.uint32(0xFFFF0000), F32)
    return hi.astype(BF16), (x - hi).astype(BF16)


def _split3(x):
    hi = x.astype(BF16)
    r = x - hi.astype(F32)
    mid = r.astype(BF16)
    lo = (r - mid.astype(F32)).astype(BF16)
    return hi, mid, lo


def _rms(x, g):
    return x * lax.rsqrt(jnp.mean(x * x, axis=-1, keepdims=True) + NORM_EPS) * g


def _sigmoid(x):
    return 1.0 / (1.0 + jnp.exp(-x))


def _silu(x):
    return x * _sigmoid(x)


def _softplus(x):
    return jnp.maximum(x, 0.0) + jnp.log(1.0 + jnp.exp(-jnp.abs(x)))


def _gelu_tanh(x):
    c = math.sqrt(2.0 / math.pi)
    return x * (0.5 * (1.0 + jnp.tanh(c * (x + 0.044715 * (x * x * x)))))


def _block_ones(n, blk, dtype=BF16):
    r = lax.broadcasted_iota(jnp.int32, (n, n), 0) // blk
    c = lax.broadcasted_iota(jnp.int32, (n, n), 1) // blk
    return (r == c).astype(dtype)


def _full(shape):
    nd = len(shape)
    return pl.BlockSpec(shape, lambda *_: (0,) * nd)


def _ab_inproj_kernel(x_ref, g_ref, w_ref, gq_ref, gk_ref, pq_ref, pk_ref,
                      qh_ref, kh_ref, vh_ref, kv_ref, xb_ref, gate_ref, *, nq, nk, hd):
    h = _rms(x_ref[...], g_ref[...]).astype(BF16)
    p = _dot(h, w_ref[...])
    dq, dk = nq * hd, nk * hd
    q = p[:, :dq]
    k = p[:, dq:dq + dk]
    v = p[:, dq + dk:dq + 2 * dk]

    def head_norm(t, ones_ref, gain):
        hi, lo = _split2(t * t)
        ms = (_dot(hi, ones_ref[...]) + _dot(lo, ones_ref[...])) * (1.0 / hd)
        return t * lax.rsqrt(ms + NORM_EPS) * gain

    qn = head_norm(q, pq_ref, gq_ref[...])
    kn = head_norm(k, pk_ref, gk_ref[...])
    for i in range(nq):
        qh_ref[i] = qn[:, i * hd:(i + 1) * hd].astype(BF16)
    for i in range(nk):
        kh_ref[i] = kn[:, i * hd:(i + 1) * hd].astype(BF16)
        vh_ref[i] = v[:, i * hd:(i + 1) * hd].astype(BF16)
    kv_ref[:, :dk] = kn
    kv_ref[:, dk:] = v
    w_lru = xb_ref.shape[-1]
    xb_ref[...] = p[:, dq + 2 * dk:dq + 2 * dk + w_lru]
    gate_ref[...] = p[:, dq + 2 * dk + w_lru:]


def _ab_inproj(x, g, w_bf, gq, gk, *, nq, nk, hd, w_lru, tm):
    m, d = x.shape
    n = w_bf.shape[1]
    dq, dk = nq * hd, nk * hd
    scale = -math.log2(math.e) * hd ** -0.5
    gq_t = (jnp.tile(gq, nq) * scale).reshape(1, dq)
    gk_t = jnp.tile(gk, nk).reshape(1, dk)
    row = lambda i: (i, 0)
    head = lambda i: (0, i, 0)
    return pl.pallas_call(
        functools.partial(_ab_inproj_kernel, nq=nq, nk=nk, hd=hd),
        grid=(m // tm,),
        in_specs=[pl.BlockSpec((tm, d), row), _full((1, d)), _full((d, n)), _full((1, dq)), _full((1, dk)),
                  _full((dq, dq)), _full((dk, dk))],
        out_specs=[pl.BlockSpec((nq, tm, hd), head), pl.BlockSpec((nk, tm, hd), head),
                   pl.BlockSpec((nk, tm, hd), head), pl.BlockSpec((tm, 2 * dk), row),
                   pl.BlockSpec((tm, w_lru), row), pl.BlockSpec((tm, w_lru), row)],
        out_shape=[jax.ShapeDtypeStruct((nq, m, hd), BF16), jax.ShapeDtypeStruct((nk, m, hd), BF16),
                   jax.ShapeDtypeStruct((nk, m, hd), BF16), jax.ShapeDtypeStruct((m, 2 * dk), F32),
                   jax.ShapeDtypeStruct((m, w_lru), F32), jax.ShapeDtypeStruct((m, w_lru), F32)],
        compiler_params=_cparams("parallel"),
    )(x, g.reshape(1, d), w_bf, gq_t, gk_t, _block_ones(dq, hd), _block_ones(dk, hd))


def _neg_abs(x):
    bits = lax.bitcast_convert_type(x, jnp.uint32) | jnp.uint32(0x80000000)
    return lax.bitcast_convert_type(bits, F32)


def _sb_log2_keep(nz):
    return jnp.minimum(nz, 0.0) - jnp.log2(1.0 + jnp.exp2(_neg_abs(nz)))


_SB_DIAG, _SB_FULL, _SB_IDLE = 0, 1, 2
_SB_DEPTH = 4
_STEP_GROUP = 8
_MASKED = -1e30


def _sb_prompt_kernel(qi_ref, kb_ref, kind_ref, q_ref, k_ref, v_ref, uu_ref, bias_ref, o_ref,
                      nz_ring, x_ring, tail_ring, w_ring, r_scr, acc_scr, *, tq, grp, hd, n_steps):
    rows = grp * tq

    def blk(idx):
        return pl.ds(pl.multiple_of(idx * tq, tq), tq)

    for ring in (x_ring, tail_ring, w_ring, r_scr, acc_scr):
        ring[...] = jnp.zeros_like(ring)
    nz_ring[...] = jnp.full(nz_ring.shape, -_MASKED, F32)

    def step(t, ph):
        kind, qi = kind_ref[t], qi_ref[t]
        v = v_ref[blk(kb_ref[t]), :]
        carry_on = jnp.where(kind == _SB_DIAG, 0.0, 1.0)
        outs = [_dot(w_ring[ph % 2, g * tq:(g + 1) * tq, :], v) for g in range(grp)]
        for g in range(grp):
            acc = acc_scr[g * tq:(g + 1) * tq, :] * carry_on + outs[g]
            acc_scr[g * tq:(g + 1) * tq, :] = acc
            o_ref[g, blk(qi), :] = acc.astype(o_ref.dtype)
        kind = kind_ref[t + 1]
        tail = tail_ring[(ph + 1) % 2] + r_scr[...] * jnp.where(kind == _SB_DIAG, 0.0, 1.0)
        w = jnp.exp2(tail - nz_ring[(ph + 1) % 4])
        w_ring[(ph + 1) % 2] = w.astype(BF16)
        r_scr[...] = tail[:, 0:1]
        uu = uu_ref[...]
        tails = [_dot(x_ring[ph % 2, g * tq:(g + 1) * tq, :], uu) for g in range(grp)]
        for g in range(grp):
            tail_ring[ph % 2, g * tq:(g + 1) * tq, :] = tails[g]
        hi, lo = _split2(_sb_log2_keep(nz_ring[(ph + 3) % 4]))
        x_ring[(ph + 1) % 2, :, :tq] = hi
        x_ring[(ph + 1) % 2, :, tq:] = lo
        k = k_ref[blk(kb_ref[t + 4]), :]
        qrows = blk(qi_ref[t + 4])
        nzs = [lax.dot_general(q_ref[g, qrows, :], k, _NT, preferred_element_type=F32) for g in range(grp)]
        for g in range(grp):
            nz_ring[ph % 4, g * tq:(g + 1) * tq, :] = nzs[g] + bias_ref[kind_ref[t + 4], g * tq:(g + 1) * tq, :]

    def step_group(j, carry):
        for ph in range(_STEP_GROUP):
            step(_STEP_GROUP * j + ph, ph % 4)
        return carry

    lax.fori_loop(0, n_steps // _STEP_GROUP, step_group, 0)


def _rev_cumsum_ones(n):
    s = lax.broadcasted_iota(jnp.int32, (n, n), 0)
    j = lax.broadcasted_iota(jnp.int32, (n, n), 1)
    u = (s >= j).astype(BF16)
    return jnp.concatenate([u, u], axis=0)


def _sb_prompt(qh, kh, vh, *, nb, tq):
    nq, m, hd = qh.shape
    nk = kh.shape[0]
    grp = nq // nk
    t = m // nb
    n_blk = t // tq
    rows = grp * tq
    items = [(i, i - s, _SB_DIAG if s == 0 else _SB_FULL) for i in range(n_blk) for s in range(i + 1)]
    n_steps = -(-(len(items) + _SB_DEPTH) // _STEP_GROUP) * _STEP_GROUP
    items = [(0, 0, _SB_IDLE)] * _SB_DEPTH + items
    items = items + [(n_blk - 1, 0, _SB_IDLE)] * (n_steps + _SB_DEPTH - len(items))
    qi_tab, kb_tab, kind_tab = (jnp.asarray([it[c] for it in items], jnp.int32) for c in range(3))
    t_idx = lax.broadcasted_iota(jnp.int32, (grp, tq, tq), 1).reshape(rows, tq)
    earlier = lax.broadcasted_iota(jnp.int32, (rows, tq), 1) < t_idx
    bias = jnp.stack([jnp.where(earlier, 0.0, -_MASKED), jnp.zeros((rows, tq), F32),
                      jnp.full((rows, tq), -_MASKED, F32)])
    per_seq = lambda b, h, *_: (h, b, 0)
    const = lambda nd: (lambda b, h, *_: (0,) * nd)
    grid_spec = pltpu.PrefetchScalarGridSpec(
        num_scalar_prefetch=3, grid=(nb, nk),
        in_specs=[pl.BlockSpec((grp, t, hd), per_seq), pl.BlockSpec((None, t, hd), per_seq),
                  pl.BlockSpec((None, t, hd), per_seq), pl.BlockSpec((2 * tq, tq), const(2)),
                  pl.BlockSpec((3, rows, tq), const(3))],
        out_specs=pl.BlockSpec((grp, t, hd), per_seq),
        scratch_shapes=[pltpu.VMEM((4, rows, tq), F32), pltpu.VMEM((2, rows, 2 * tq), BF16),
                        pltpu.VMEM((2, rows, tq), F32), pltpu.VMEM((2, rows, tq), BF16),
                        pltpu.VMEM((rows, 1), F32), pltpu.VMEM((rows, hd), F32)])
    return pl.pallas_call(
        functools.partial(_sb_prompt_kernel, tq=tq, grp=grp, hd=hd, n_steps=n_steps),
        grid_spec=grid_spec,
        out_shape=jax.ShapeDtypeStruct((nq, m, hd), BF16),
        compiler_params=_cparams("parallel", "parallel"),
        name="sb_prompt",
    )(qi_tab, kb_tab, kind_tab, qh, kh, vh, _rev_cumsum_ones(tq), bias)


def _lru_gates(xc, wbd_ref, bias_ref, lam_ref):
    w = xc.shape[-1]
    ra = _dot(xc.astype(BF16), wbd_ref[...]) + bias_ref[...]
    r = _sigmoid(ra[:, :w])
    ig = _sigmoid(ra[:, w:])
    log_a = (-LRU_C) * r * _softplus(-lam_ref[...])
    a = jnp.exp(log_a)
    b = jnp.sqrt(-jnp.tanh(log_a) * (a * a + 1.0)) * (ig * xc)
    return a, b


def _rglru_prompt_kernel(xb_ref, gate_ref, cw_ref, cb_ref, wbd_ref, bias_ref, lam_ref,
                         y_ref, hlast_ref, cbuf_ref, xpad, a_scr, b_scr, h_scr, *, tt):
    t = pl.program_id(0)
    nb = xb_ref.shape[0]

    @pl.when(t == 0)
    def _():
        h_scr[...] = jnp.zeros_like(h_scr)
        xpad[:, 0:SUBLANES, :] = jnp.zeros((nb, SUBLANES, xpad.shape[2]), F32)

    for s in range(nb):
        xc = _causal_conv_tile(xb_ref.at[s], xpad.at[s], cw_ref, cb_ref, cbuf_ref.at[s], tt)
        a, b = _lru_gates(xc, wbd_ref, bias_ref, lam_ref)
        a_scr[s] = a
        b_scr[s] = b

    def body(r, hs):
        out = []
        for s in range(nb):
            h = a_scr[s, pl.ds(r, 1), :] * hs[s] + b_scr[s, pl.ds(r, 1), :]
            b_scr[s, pl.ds(r, 1), :] = h
            out.append(h)
        return tuple(out)

    hs = lax.fori_loop(0, tt, body, tuple(h_scr[s] for s in range(nb)), unroll=8)
    for s in range(nb):
        h_scr[s] = hs[s]
        hlast_ref[s] = hs[s]
        y_ref[s] = (_gelu_tanh(gate_ref[s]) * b_scr[s]).astype(y_ref.dtype)


def _lru_weights(wa, ba, wx, bx):
    nblk, bd, _ = wa.shape
    w = nblk * bd
    eye = jnp.eye(nblk, dtype=F32)

    def bdiag(m):
        return jnp.einsum('kij,kl->kilj', m, eye).reshape(w, w)

    wbd = jnp.concatenate([bdiag(wa), bdiag(wx)], axis=1).astype(BF16)
    bias = jnp.concatenate([ba, bx]).reshape(1, 2 * w)
    return wbd, bias


def _rglru_prompt(xb, gate, conv_w, conv_b, wbd, bias, lam, *, nb, tt):
    m, w = xb.shape
    t = m // nb
    nt = t // tt
    tile = lambda i: (0, i, 0)
    y, hlast, cbuf = pl.pallas_call(
        functools.partial(_rglru_prompt_kernel, tt=tt),
        grid=(nt,),
        in_specs=[pl.BlockSpec((nb, tt, w), tile), pl.BlockSpec((nb, tt, w), tile), _full((CONV_WIDTH, w)),
                  _full((1, w)), _full((w, 2 * w)), _full((1, 2 * w)), _full((1, w))],
        out_specs=[pl.BlockSpec((nb, tt, w), tile), _full((nb, 1, w)), _full((nb, CONV_WIDTH - 1, w))],
        out_shape=[jax.ShapeDtypeStruct((nb, t, w), BF16), jax.ShapeDtypeStruct((nb, 1, w), F32),
                   jax.ShapeDtypeStruct((nb, CONV_WIDTH - 1, w), F32)],
        scratch_shapes=[pltpu.VMEM((nb, tt + SUBLANES, w), F32), pltpu.VMEM((nb, tt, w), F32),
                        pltpu.VMEM((nb, tt, w), F32), pltpu.VMEM((nb, 1, w), F32)],
        compiler_params=_cparams("arbitrary"),
        name="rglru_prompt",
    )(xb.reshape(nb, t, w), gate.reshape(nb, t, w), conv_w, conv_b.reshape(1, w), wbd, bias, lam.reshape(1, w))
    return y.reshape(m, w), hlast, cbuf


def _post_kernel(x_ref, oh_ref, y2_ref, wo_ref, g_ref, wu_ref, wg_ref, wd_ref, out_ref,
                 x1_scr, hb_scr, acc_scr):
    j = pl.program_id(1)
    nh, _, hd = oh_ref.shape

    @pl.when(j == 0)
    def _():
        heads = jnp.concatenate([oh_ref[h] for h in range(nh)], axis=1)
        mix = _dot(y2_ref[...], wo_ref[nh * hd:, :]) + _dot(heads, wo_ref[:nh * hd, :])
        x1 = x_ref[...] + mix
        x1_scr[...] = x1
        hb_scr[...] = _rms(x1, g_ref[...]).astype(BF16)
        acc_scr[...] = jnp.zeros_like(acc_scr)

    hb = hb_scr[...]
    u = _dot(hb, wu_ref[...])
    gg = _dot(hb, wg_ref[...])
    acc_scr[...] += _dot((_silu(u) * gg).astype(BF16), wd_ref[...])

    @pl.when(j == pl.num_programs(1) - 1)
    def _():
        out_ref[...] = x1_scr[...] + acc_scr[...]


def _post(x, oh, y2, w_out_bf, g, w_in_bf, w_dn_bf, *, tm, th):
    m, d = x.shape
    nh, _, hd = oh.shape
    hid = w_dn_bf.shape[0]
    nj = hid // th
    return pl.pallas_call(
        _post_kernel,
        grid=(m // tm, nj),
        in_specs=[pl.BlockSpec((tm, d), lambda i, j: (i, 0)),
                  pl.BlockSpec((nh, tm, hd), lambda i, j: (0, i, 0)),
                  pl.BlockSpec((tm, y2.shape[1]), lambda i, j: (i, 0)),
                  pl.BlockSpec(w_out_bf.shape, lambda i, j: (0, 0)),
                  pl.BlockSpec((1, d), lambda i, j: (0, 0)),
                  pl.BlockSpec((d, th), lambda i, j: (0, j)),
                  pl.BlockSpec((d, th), lambda i, j: (0, nj + j)),
                  pl.BlockSpec((th, d), lambda i, j: (j, 0))],
        out_specs=pl.BlockSpec((tm, d), lambda i, j: (i, 0)),
        out_shape=jax.ShapeDtypeStruct((m, d), F32),
        scratch_shapes=[pltpu.VMEM((tm, d), F32), pltpu.VMEM((tm, d), BF16), pltpu.VMEM((tm, d), F32)],
        compiler_params=_cparams("parallel", "arbitrary"),
    )(x, oh, y2, w_out_bf, g.reshape(1, d), w_in_bf, w_in_bf, w_dn_bf)


def _rope_tables(pos, nope, rope):
    half = rope // 2
    inv = ROPE_THETA ** (-jnp.arange(half, dtype=F32) / half)
    ang = pos.astype(F32)[:, None] * inv[None, :]
    n = pos.shape[0]
    lead = jnp.zeros((n, nope), F32)
    trail = jnp.zeros((n, LANES - nope - rope), F32)
    cos = jnp.concatenate([lead, jnp.cos(ang), jnp.cos(ang), trail], axis=1)
    sin = jnp.concatenate([lead, jnp.sin(ang), jnp.sin(ang), trail], axis=1)
    return cos, sin


def _rot_partner(w):
    half = w.shape[-1] // 2
    return jnp.concatenate([-w[..., half:], w[..., :half]], axis=-1)


def _cd_weights(w_in, w_uq, w_ukv, gq, gk, dt_bias, a_log, d_skip, *, kv_rank, q_rank, rope, inner, conv_dim):
    d = w_in.shape[0]
    nh, qk = w_uq.shape[1], w_uq.shape[2]
    nope = qk - rope
    vd = w_ukv.shape[2] - nope
    n_ssd = dt_bias.shape[0]
    assert qk <= LANES and kv_rank == LANES and n_ssd <= LANES
    o = 0
    cq = w_in[:, o:o + q_rank]; o += q_rank
    ckv = w_in[:, o:o + kv_rank]; o += kv_rank
    kpe = w_in[:, o:o + rope]; o += rope
    z = w_in[:, o:o + inner]; o += inner
    xbc = w_in[:, o:o + conv_dim]; o += conv_dim
    dt = w_in[:, o:o + n_ssd]

    def on_rope_lanes(w):
        return jnp.concatenate([jnp.zeros((d, nope), F32), w, jnp.zeros((d, LANES - qk), F32)], axis=1)

    w_pad = jnp.concatenate([cq, ckv, on_rope_lanes(kpe), on_rope_lanes(_rot_partner(kpe)), z, xbc,
                             dt, jnp.zeros((d, LANES - n_ssd), F32)], axis=1).astype(BF16)
    zq = jnp.zeros((q_rank, nh, LANES - qk), F32)
    wqm = jnp.concatenate([w_uq, zq], axis=2).reshape(q_rank, nh * LANES).astype(BF16)
    wqp = jnp.concatenate([jnp.zeros((q_rank, nh, nope), F32), _rot_partner(w_uq[:, :, nope:]), zq],
                          axis=2).reshape(q_rank, nh * LANES).astype(BF16)
    wk = w_ukv[:, :, :nope]
    wk128 = jnp.concatenate([wk, jnp.zeros((kv_rank, nh, LANES - nope), F32)], axis=2)
    pad1 = lambda v, fill=0.0: jnp.concatenate([v, jnp.full((LANES - v.shape[0],), fill, F32)]).reshape(1, LANES)
    return dict(
        w_pad=w_pad, wqm=wqm, wqp=wqp,
        wk128=wk128.reshape(kv_rank, nh * LANES).astype(BF16),
        wkt=wk.reshape(kv_rank, nh * nope).T.astype(BF16),
        wv=w_ukv[:, :, nope:].reshape(kv_rank, nh * vd).astype(BF16),
        gq128=pad1(gq * (math.log2(math.e) * qk ** -0.5)),
        gk128=pad1(gk), dtb128=pad1(dt_bias), alog128=pad1(a_log),
        dsk=jnp.repeat(d_skip, inner // n_ssd).reshape(1, inner),
        nh=nh, qk=qk, nope=nope, rope=rope, vd=vd, q_rank=q_rank, kv_rank=kv_rank, inner=inner,
        conv_dim=conv_dim, n_ssd=n_ssd)


def _cd_inproj_kernel(x_ref, g_ref, w_ref, gql_ref, gkl_ref, wqm_ref, wqp_ref, wk_ref, wv_ref, gq_ref, gk_ref,
                      cos_ref, sin_ref, qh_ref, kh_ref, vh_ref, lat_ref, z_ref, xbc_ref, dt_ref,
                      *, nh, qk, nope, rope, vd, q_rank, kv_rank, inner, conv_dim):
    h = _rms(x_ref[...], g_ref[...]).astype(BF16)
    p = _dot(h, w_ref[...])
    o = 0
    cq = p[:, o:o + q_rank]; o += q_rank
    ckv = p[:, o:o + kv_rank]; o += kv_rank
    kpe = p[:, o:o + LANES]; o += LANES
    kpe_rot = p[:, o:o + LANES]; o += LANES
    z_ref[...] = p[:, o:o + inner]; o += inner
    xbc_ref[...] = p[:, o:o + conv_dim]; o += conv_dim
    dt_ref[...] = p[:, o:o + LANES]

    cos_t = cos_ref[...]
    sin_t = sin_ref[...]
    lane = lax.broadcasted_iota(jnp.int32, (1, LANES), 1)
    cos_q = cos_t + (lane < nope).astype(F32)
    cqn = _rms(cq, gql_ref[...]).astype(BF16)
    qm = _dot(cqn, wqm_ref[...])
    qp = _dot(cqn, wqp_ref[...])
    for i in range(nh):
        qi = qm[:, i * LANES:(i + 1) * LANES] * cos_q + qp[:, i * LANES:(i + 1) * LANES] * sin_t
        ms = jnp.sum(qi * qi, axis=-1, keepdims=True) * (1.0 / qk)
        qh_ref[i] = (qi * lax.rsqrt(ms + NORM_EPS) * gq_ref[...]).astype(BF16)

    ckvn = _rms(ckv, gkl_ref[...])
    kper = kpe * cos_t + kpe_rot * sin_t
    lat_ref[:, :kv_rank] = ckvn
    lat_ref[:, kv_rank:] = kper[:, nope:nope + rope]
    cb = ckvn.astype(BF16)
    kn = _dot(cb, wk_ref[...])
    vv = _dot(cb, wv_ref[...])
    for i in range(nh):
        ki = kn[:, i * LANES:(i + 1) * LANES] + kper
        ms = jnp.sum(ki * ki, axis=-1, keepdims=True) * (1.0 / qk)
        kh_ref[i] = (ki * lax.rsqrt(ms + NORM_EPS) * gk_ref[...]).astype(BF16)
        vh_ref[i] = vv[:, i * vd:(i + 1) * vd].astype(BF16)


def _cd_inproj(x, g, cw, gql, gkl, cos, sin, *, tm):
    m, d = x.shape
    nh, vd, kv_rank, rope = cw["nh"], cw["vd"], cw["kv_rank"], cw["rope"]
    inner, conv_dim, q_rank = cw["inner"], cw["conv_dim"], cw["q_rank"]
    n_pos = cos.shape[0] // tm
    row = lambda i: (i, 0)
    head = lambda i: (0, i, 0)
    dims = {k: cw[k] for k in ("nh", "qk", "nope", "rope", "vd", "q_rank", "kv_rank", "inner", "conv_dim")}
    return pl.pallas_call(
        functools.partial(_cd_inproj_kernel, **dims),
        grid=(m // tm,),
        in_specs=[pl.BlockSpec((tm, d), row), _full((1, d)), _full(cw["w_pad"].shape), _full((1, q_rank)),
                  _full((1, kv_rank)), _full(cw["wqm"].shape), _full(cw["wqp"].shape), _full(cw["wk128"].shape),
                  _full(cw["wv"].shape), _full((1, LANES)), _full((1, LANES)),
                  pl.BlockSpec((tm, LANES), lambda i: (i % n_pos, 0)),
                  pl.BlockSpec((tm, LANES), lambda i: (i % n_pos, 0))],
        out_specs=[pl.BlockSpec((nh, tm, LANES), head), pl.BlockSpec((nh, tm, LANES), head),
                   pl.BlockSpec((nh, tm, vd), head), pl.BlockSpec((tm, kv_rank + rope), row),
                   pl.BlockSpec((tm, inner), row), pl.BlockSpec((tm, conv_dim), row),
                   pl.BlockSpec((tm, LANES), row)],
        out_shape=[jax.ShapeDtypeStruct((nh, m, LANES), BF16), jax.ShapeDtypeStruct((nh, m, LANES), BF16),
                   jax.ShapeDtypeStruct((nh, m, vd), BF16), jax.ShapeDtypeStruct((m, kv_rank + rope), F32),
                   jax.ShapeDtypeStruct((m, inner), F32), jax.ShapeDtypeStruct((m, conv_dim), F32),
                   jax.ShapeDtypeStruct((m, LANES), F32)],
        compiler_params=_cparams("parallel"),
    )(x, g.reshape(1, d), cw["w_pad"], gql.reshape(1, q_rank), gkl.reshape(1, kv_rank), cw["wqm"], cw["wqp"],
      cw["wk128"], cw["wv"], cw["gq128"], cw["gk128"], cos, sin)


_MLA_DEPTH = 3
_MLA_GROUP = 4


def _mla_prompt_kernel(qi_ref, kb_ref, kind_ref, q_ref, k_ref, v_ref, bias_ref, o_ref,
                       s_ring, p_ring, pv_ring, a_ring, linv_ring, m_scr, l_scr, acc_scr, *, tq, n_steps):
    halves = [slice(c * (tq // 2), (c + 1) * (tq // 2)) for c in range(2)]
    vd = v_ref.shape[-1]

    def blk(idx):
        return pl.ds(pl.multiple_of(idx * tq, tq), tq)

    for ring in (s_ring, p_ring, pv_ring, a_ring, linv_ring, l_scr, acc_scr):
        ring[...] = jnp.zeros_like(ring)
    m_scr[...] = jnp.full(m_scr.shape, -jnp.inf, F32)

    def step(t, ph):
        acc = a_ring[(ph + 1) % 4, :, :vd] * acc_scr[...] + pv_ring[(ph + 1) % 2]
        acc_scr[...] = acc
        o_ref[blk(qi_ref[t]), :] = (acc * linv_ring[(ph + 1) % 4, :, :vd]).astype(o_ref.dtype)
        v = v_ref[blk(kb_ref[t + 1]), :]
        outs = [_dot(p_ring[ph % 2, h, :], v) for h in halves]
        for h, o in zip(halves, outs):
            pv_ring[ph % 2, h, :] = o
        kind = kind_ref[t + 2]
        tiles = [slice(c * LANES, (c + 1) * LANES) for c in range(tq // LANES)]
        s = [s_ring[(ph + 1) % 2, :, c] + bias_ref[kind, :, c] for c in tiles]
        m_prev = jnp.where(kind == _SB_DIAG, -jnp.inf, m_scr[...])
        m_blk = jnp.max(functools.reduce(jnp.maximum, s), axis=-1, keepdims=True)
        m_new = jnp.maximum(m_prev, jnp.broadcast_to(m_blk, m_prev.shape))
        alpha = jnp.exp2(m_prev - m_new)
        p = [jnp.exp2(sc - m_new) for sc in s]
        l_blk = jnp.sum(functools.reduce(jnp.add, p), axis=-1, keepdims=True)
        l = alpha * l_scr[...] + jnp.broadcast_to(l_blk, m_prev.shape)
        m_scr[...] = m_new
        l_scr[...] = l
        a_ring[(ph + 3) % 4] = alpha
        linv_ring[(ph + 3) % 4] = 1.0 / l
        for c, pc in zip(tiles, p):
            p_ring[(ph + 1) % 2, :, c] = pc.astype(BF16)
        k = k_ref[blk(kb_ref[t + 3]), :]
        qrows = blk(qi_ref[t + 3])
        ss = [lax.dot_general(q_ref[qrows, :][h], k, _NT, preferred_element_type=F32) for h in halves]
        for h, sc in zip(halves, ss):
            s_ring[ph % 2, h, :] = sc

    def step_group(j, carry):
        for ph in range(_MLA_GROUP):
            step(_MLA_GROUP * j + ph, ph % 4)
        return carry

    lax.fori_loop(0, n_steps // _MLA_GROUP, step_group, 0)


def _mla_prompt(qh, kh, vh, *, nb, tq):
    nh, m, dk = qh.shape
    vd = vh.shape[-1]
    t = m // nb
    n_blk = t // tq
    items = [(i, i if s == 0 else s - 1, _SB_DIAG if s == 0 else _SB_FULL) for i in range(n_blk) for s in range(i + 1)]
    n_steps = -(-(len(items) + _MLA_DEPTH) // _MLA_GROUP) * _MLA_GROUP
    items = [(0, 0, _SB_IDLE)] * _MLA_DEPTH + items
    items = items + [(n_blk - 1, 0, _SB_IDLE)] * (n_steps + _MLA_DEPTH - len(items))
    qi_tab, kb_tab, kind_tab = (jnp.asarray([it[c] for it in items], jnp.int32) for c in range(3))
    causal = lax.broadcasted_iota(jnp.int32, (tq, tq), 1) <= lax.broadcasted_iota(jnp.int32, (tq, tq), 0)
    bias = jnp.stack([jnp.where(causal, 0.0, _MASKED), jnp.zeros((tq, tq), F32), jnp.full((tq, tq), _MASKED, F32)])
    per_seq = lambda b, h, *_: (h, b, 0)
    grid_spec = pltpu.PrefetchScalarGridSpec(
        num_scalar_prefetch=3, grid=(nb, nh),
        in_specs=[pl.BlockSpec((None, t, dk), per_seq), pl.BlockSpec((None, t, dk), per_seq),
                  pl.BlockSpec((None, t, vd), per_seq), pl.BlockSpec((3, tq, tq), lambda b, h, *_: (0, 0, 0))],
        out_specs=pl.BlockSpec((None, t, vd), per_seq),
        scratch_shapes=[pltpu.VMEM((2, tq, tq), F32), pltpu.VMEM((2, tq, tq), BF16), pltpu.VMEM((2, tq, vd), F32),
                        pltpu.VMEM((4, tq, LANES), F32), pltpu.VMEM((4, tq, LANES), F32), pltpu.VMEM((tq, LANES), F32),
                        pltpu.VMEM((tq, LANES), F32), pltpu.VMEM((tq, vd), F32)])
    return pl.pallas_call(
        functools.partial(_mla_prompt_kernel, tq=tq, n_steps=n_steps),
        grid_spec=grid_spec,
        out_shape=jax.ShapeDtypeStruct((nh, m, vd), BF16),
        compiler_params=_cparams("parallel", "parallel"),
        name="mla_prompt",
    )(qi_tab, kb_tab, kind_tab, qh, kh, vh, bias)


def _causal_conv_tile(x_ref, xpad, cw_ref, cb_ref, cbuf_ref, rows):
    pad = SUBLANES
    xpad[pad:pad + rows, :] = x_ref[...]
    xc = cb_ref[...]
    for i in range(CONV_WIDTH):
        o = pad - (CONV_WIDTH - 1) + i
        xc = xc + xpad[o:o + rows, :] * cw_ref[i:i + 1, :]
    cbuf_ref[...] = xpad[pad + rows - (CONV_WIDTH - 1):pad + rows, :]
    xpad[0:pad, :] = xpad[rows:rows + pad, :]
    return xc


def _gated_group_norm(y, z, ng_ref, y_ref, n_groups):
    y = y * _silu(z)
    gs = y.shape[-1] // n_groups
    for g in range(n_groups):
        y_ref[:, g * gs:(g + 1) * gs] = _rms(y[:, g * gs:(g + 1) * gs], ng_ref[:, g * gs:(g + 1) * gs]).astype(y_ref.dtype)


def _ssd_prompt_kernel(xbc_ref, z_ref, dt_ref, cw_ref, cb_ref, dtb_ref, alog_ref, dsk_ref, ng_ref, ltri_ref,
                       y_ref, hlast_ref, cbuf_ref, xpad, state, yscr, *, q, nh, hd, ns, n_groups):
    c = pl.program_id(1)

    @pl.when(c == 0)
    def _():
        state[...] = jnp.zeros_like(state)
        xpad[0:SUBLANES, :] = jnp.zeros((SUBLANES, xpad.shape[1]), F32)

    xc = _silu(_causal_conv_tile(xbc_ref, xpad, cw_ref, cb_ref, cbuf_ref, q))
    inner = nh * hd
    hpg = nh // n_groups
    xs = xc[:, :inner]
    bm = xc[:, inner:inner + n_groups * ns]
    cm = xc[:, inner + n_groups * ns:]
    lane = lax.broadcasted_iota(jnp.int32, (1, LANES), 1)
    dtv = _softplus(dt_ref[...] + dtb_ref[...])
    a = jnp.where(lane < nh, -jnp.exp(alog_ref[...]), 0.0)
    ltri = ltri_ref[...]
    cum = sum(_dot(ltri, part) for part in _split3(dtv * a))
    cum_t = cum.T
    dt_t = dtv.T
    cum_last = cum[q - 1:q, :]
    to_end = jnp.exp(cum_last - cum) * dtv
    ecum = jnp.exp(cum)
    elast = jnp.exp(cum_last)
    causal = lax.broadcasted_iota(jnp.int32, (q, q), 0) >= lax.broadcasted_iota(jnp.int32, (q, q), 1)
    for g in range(n_groups):
        cmg = cm[:, g * ns:(g + 1) * ns].astype(BF16)
        bmg = bm[:, g * ns:(g + 1) * ns].astype(BF16)
        cb = lax.dot_general(cmg, bmg, _NT, preferred_element_type=F32)
        for r in range(g * hpg, (g + 1) * hpg):
            seg = cum[:, r:r + 1] - cum_t[r:r + 1, :]
            decay = jnp.exp(jnp.where(causal, seg, -jnp.inf))
            mix = cb * decay * dt_t[r:r + 1, :]
            xh = xs[:, r * hd:(r + 1) * hd]
            hprev = state[r]
            y = _dot(mix.astype(BF16), xh.astype(BF16))
            y = y + lax.dot_general(cmg, hprev.astype(BF16), _NT, preferred_element_type=F32) * ecum[:, r:r + 1]
            yscr[:, r * hd:(r + 1) * hd] = y + dsk_ref[:, r * hd:(r + 1) * hd] * xh
            xw = (xh * to_end[:, r:r + 1]).astype(BF16)
            state[r] = elast[:, r:r + 1] * hprev + lax.dot_general(xw, bmg, _TN, preferred_element_type=F32)
    hlast_ref[...] = state[...]
    _gated_group_norm(yscr[...], z_ref[...], ng_ref, y_ref, n_groups)


def _lower_tri_ones(n):
    r = lax.broadcasted_iota(jnp.int32, (n, n), 0)
    c = lax.broadcasted_iota(jnp.int32, (n, n), 1)
    return (r >= c).astype(BF16)


def _ssd_prompt(xbc, z, dt, conv_w, conv_b, cw, norm_g, *, nb, q, state_dim):
    m, conv_dim = xbc.shape
    inner, nh = cw["inner"], cw["n_ssd"]
    hd = inner // nh
    n_groups = (conv_dim - inner) // (2 * state_dim)
    t = m // nb
    nc = t // q
    row = lambda b, c: (b * nc + c, 0)
    return pl.pallas_call(
        functools.partial(_ssd_prompt_kernel, q=q, nh=nh, hd=hd, ns=state_dim, n_groups=n_groups),
        grid=(nb, nc),
        in_specs=[pl.BlockSpec((q, conv_dim), row), pl.BlockSpec((q, inner), row), pl.BlockSpec((q, LANES), row),
                  _full((CONV_WIDTH, conv_dim)), _full((1, conv_dim)), _full((1, LANES)), _full((1, LANES)),
                  _full((1, inner)), _full((1, inner)), _full((q, q))],
        out_specs=[pl.BlockSpec((q, inner), row),
                   pl.BlockSpec((None, nh, hd, state_dim), lambda b, c: (b, 0, 0, 0)),
                   pl.BlockSpec((None, CONV_WIDTH - 1, conv_dim), lambda b, c: (b, 0, 0))],
        out_shape=[jax.ShapeDtypeStruct((m, inner), BF16), jax.ShapeDtypeStruct((nb, nh, hd, state_dim), F32),
                   jax.ShapeDtypeStruct((nb, CONV_WIDTH - 1, conv_dim), F32)],
        scratch_shapes=[pltpu.VMEM((q + SUBLANES, conv_dim), F32), pltpu.VMEM((nh, hd, state_dim), F32),
                        pltpu.VMEM((q, inner), F32)],
        compiler_params=_cparams("parallel", "arbitrary"),
        name="ssd_prompt",
    )(xbc, z, dt, conv_w, conv_b.reshape(1, conv_dim), cw["dtb128"], cw["alog128"], cw["dsk"],
      norm_g.reshape(1, inner), _lower_tri_ones(q))


def _sb_sample_kernel(pt_ref, qbd_ref, uu_ref, *rest, npg, dk, grp):
    pages = rest[:npg]
    o_ref, z_scr, r_scr, acc_scr, wb_scr = rest[npg:]
    c = pl.program_id(1)
    nrow = qbd_ref.shape[0]
    n_kv = nrow // grp
    hd = dk // n_kv
    keys = z_scr.shape[1]

    @pl.when(c == 0)
    def _():
        r_scr[...] = jnp.zeros_like(r_scr)
        acc_scr[...] = jnp.zeros_like(acc_scr)

    qbd = qbd_ref[...]
    for j in range(npg):
        z_scr[j * nrow:(j + 1) * nrow, :] = _dot(qbd, pages[j][:dk, :].astype(BF16))
    nz = z_scr[...]
    hi, lo = _split2(_sb_log2_keep(nz))
    tl = _dot(jnp.concatenate([hi, lo], axis=1), uu_ref[...])
    tot = jnp.broadcast_to(tl[:, 0:1], tl.shape)
    r = r_scr[...]
    carries = [None] * npg
    for j in reversed(range(npg)):
        carries[j] = r
        r = r + tot[j * nrow:(j + 1) * nrow, :]
    r_scr[...] = r
    z_scr[...] = jnp.exp2(tl + jnp.concatenate(carries, axis=0) - nz)
    sub = wb_scr.shape[1]
    for r in range(npg * nrow):
        wb_scr[r] = jnp.broadcast_to(z_scr[r:r + 1, :], (sub, keys))
    for rg in range(dk // sub):
        h = rg * sub // hd
        rows = slice(rg * sub, (rg + 1) * sub)
        acc = [acc_scr[g, rows, :] for g in range(grp)]
        for j in range(npg):
            vt = pages[j][dk + rg * sub:dk + (rg + 1) * sub, :]
            for g in range(grp):
                acc[g] = acc[g] + vt * wb_scr[j * nrow + g * n_kv + h]
        for g in range(grp):
            acc_scr[g, rows, :] = acc[g]

    @pl.when(c == pl.num_programs(1) - 1)
    def _():
        for g in range(grp):
            o_ref[g:g + 1, :] = jnp.sum(acc_scr[g].T, axis=0, keepdims=True)


def _sb_sample(qh, cache_t, layer, page_table, *, n_kv, npg=32):
    n_q, s, hd = qh.shape
    grp = n_q // n_kv
    dk = n_kv * hd
    page = cache_t.shape[3]
    n_pages = page_table.shape[1]
    npg = min(npg, n_pages)
    nch = n_pages // npg
    eye = jnp.eye(n_kv, dtype=qh.dtype)
    qbd = jnp.einsum('hgsd,hk->sghkd', qh.reshape(n_kv, grp, s, hd), eye).reshape(s, n_q, dk)

    def page_spec(j):
        return pl.BlockSpec((None, None, 2 * dk, page),
                            lambda i, c, pt: (layer, pt[i, (nch - 1 - c) * npg + j], 0, 0))

    grid_spec = pltpu.PrefetchScalarGridSpec(
        num_scalar_prefetch=1, grid=(s, nch),
        in_specs=[pl.BlockSpec((None, n_q, dk), lambda i, c, pt: (i, 0, 0)),
                  pl.BlockSpec((2 * page, page), lambda i, c, pt: (0, 0))] + [page_spec(j) for j in range(npg)],
        out_specs=pl.BlockSpec((None, grp, dk), lambda i, c, pt: (i, 0, 0)),
        scratch_shapes=[pltpu.VMEM((npg * n_q, page), F32), pltpu.VMEM((n_q, page), F32),
                        pltpu.VMEM((grp, dk, page), F32), pltpu.VMEM((npg * n_q, SUBLANES, page), F32)])
    og = pl.pallas_call(
        functools.partial(_sb_sample_kernel, npg=npg, dk=dk, grp=grp),
        grid_spec=grid_spec,
        out_shape=jax.ShapeDtypeStruct((s, grp, dk), F32),
        compiler_params=_cparams("parallel", "arbitrary"),
        name="sb_sample",
    )(page_table, qbd, _rev_cumsum_ones(page), *([cache_t] * npg))
    o = og.reshape(s, grp, n_kv, hd)
    return jnp.transpose(o, (2, 1, 0, 3)).reshape(n_q, s, hd).astype(BF16)


def _conv_step(x, buf_ref, nbuf_ref, cw_ref, cb_ref):
    xc = cb_ref[...]
    for i in range(CONV_WIDTH - 1):
        xc = xc + buf_ref[i] * cw_ref[i:i + 1, :]
        if i > 0:
            nbuf_ref[i - 1] = buf_ref[i]
    nbuf_ref[CONV_WIDTH - 2] = x
    return xc + x * cw_ref[CONV_WIDTH - 1:CONV_WIDTH, :]


def _rglru_step_kernel(xb_ref, gate_ref, buf_ref, h0_ref, cw_ref, cb_ref, wbd_ref, bias_ref, lam_ref,
                       y_ref, h_ref, nbuf_ref):
    xc = _conv_step(xb_ref[...], buf_ref, nbuf_ref, cw_ref, cb_ref)
    a, b = _lru_gates(xc, wbd_ref, bias_ref, lam_ref)
    h = a * h0_ref[...] + b
    h_ref[...] = h
    y_ref[...] = (_gelu_tanh(gate_ref[...]) * h).astype(y_ref.dtype)


def _rglru_step(xb, gate, buf, h0, conv_w, conv_b, wbd, bias, lam):
    s, w = xb.shape
    return pl.pallas_call(
        _rglru_step_kernel,
        out_shape=[jax.ShapeDtypeStruct((s, w), BF16), jax.ShapeDtypeStruct((s, w), F32),
                   jax.ShapeDtypeStruct((CONV_WIDTH - 1, s, w), F32)],
        compiler_params=pltpu.CompilerParams(vmem_limit_bytes=VMEM_LIMIT_BYTES),
    )(xb, gate, buf, h0, conv_w, conv_b.reshape(1, w), wbd, bias, lam.reshape(1, w))


def _mla_sample_kernel(pt_ref, qn_ref, qr_ref, new_ref, wkt_ref, wv_ref, *rest,
                       n_pages, page, kv_rank, rope, qk, nh, ppi):
    pages = rest[:n_pages]
    o_ref, ckv_scr, kpe_scr, s_scr = rest[n_pages:]
    n_iter = n_pages // ppi
    width = ppi * page
    nope = wkt_ref.shape[0] // nh
    for j in range(n_pages):
        lanes = slice((j % ppi) * page, (j % ppi + 1) * page)
        ckv_scr[j // ppi, :, lanes] = pages[j][:kv_rank, :].astype(BF16)
        kpe_scr[j // ppi, :, lanes] = pages[j][kv_rank:, :]
    qr = qr_ref[...]
    wkt = wkt_ref[...]
    q_lat = _dot(qn_ref[...], wkt).astype(BF16)

    def scores(kn, ct, kp):
        kn = kn.reshape(nh, nope, kn.shape[-1])
        ssq = jnp.sum(kn * kn, axis=1) + jnp.sum(kp * kp, axis=0, keepdims=True)
        s = _dot(q_lat, ct) + _dot(qr, kp.astype(BF16))
        return s * lax.rsqrt(ssq * (1.0 / qk) + NORM_EPS)

    new_t = jnp.broadcast_to(jnp.concatenate([new_ref[...], jnp.zeros((1, 2 * LANES - kv_rank - rope), F32)], axis=1),
                             (LANES, 2 * LANES)).T
    ct_new = new_t[:kv_rank, :].astype(BF16)
    first = lax.broadcasted_iota(jnp.int32, (nh, LANES), 1) == 0
    s_new = jnp.where(first, scores(_dot(wkt, ct_new), ct_new, new_t[kv_rank:kv_rank + rope, :]), -jnp.inf)

    def score_pass(i, m):
        ct = ckv_scr[i]
        s = scores(_dot(wkt, ct), ct, kpe_scr[i])
        s_scr[i] = s
        return jnp.maximum(m, jnp.max(s, axis=1, keepdims=True))

    unroll = max(c for c in (1, 2, 4, 8) if n_iter % c == 0)
    m = lax.fori_loop(0, n_iter, score_pass, jnp.max(s_new, axis=1, keepdims=True), unroll=unroll)
    p_new = jnp.exp2(s_new - m)

    def value_pass(i, c):
        l, acc_t = c
        p = jnp.exp2(s_scr[i] - m)
        acc_t = acc_t + lax.dot_general(ckv_scr[i], p.astype(BF16), _NT, preferred_element_type=F32)
        return l + jnp.sum(p, axis=1, keepdims=True), acc_t

    l0 = jnp.sum(p_new, axis=1, keepdims=True)
    acc0 = lax.dot_general(ct_new, p_new.astype(BF16), _NT, preferred_element_type=F32)
    l, acc_t = lax.fori_loop(0, n_iter, value_pass, (l0, acc0), unroll=unroll)
    acc = jnp.concatenate([acc_t, jnp.zeros((kv_rank, LANES - nh), F32)], axis=1).T[:nh, :]
    hi, lo = _split2(acc / l)
    o_ref[...] = _dot(hi, wv_ref[...]) + _dot(lo, wv_ref[...])


def _mla_sample(qh, lat_new, cache, layer, page_table, cw):
    nh, s, _ = qh.shape
    nope, rope, vd, kv_rank, qk = cw["nope"], cw["rope"], cw["vd"], cw["kv_rank"], cw["qk"]
    page = cache.shape[3]
    n_pages = page_table.shape[1]
    ppi = max(c for c in (1, 2, 4) if n_pages % c == 0)
    qg = jnp.swapaxes(qh.astype(F32) * cw["gk128"].reshape(1, 1, LANES), 0, 1)
    qn = jnp.einsum('shd,hk->shkd', qg[:, :, :nope], jnp.eye(nh, dtype=F32)).reshape(s, nh, nh * nope).astype(BF16)
    qr = qg[:, :, nope:qk].astype(BF16)
    new = lat_new.reshape(s, 1, kv_rank + rope)

    def page_spec(j):
        return pl.BlockSpec((None, None, kv_rank + rope, page), lambda i, pt: (layer, pt[i, j], 0, 0))

    grid_spec = pltpu.PrefetchScalarGridSpec(
        num_scalar_prefetch=1, grid=(s,),
        in_specs=[pl.BlockSpec((None, nh, nh * nope), lambda i, pt: (i, 0, 0)),
                  pl.BlockSpec((None, nh, rope), lambda i, pt: (i, 0, 0)),
                  pl.BlockSpec((None, 1, kv_rank + rope), lambda i, pt: (i, 0, 0)),
                  pl.BlockSpec(cw["wkt"].shape, lambda i, pt: (0, 0)),
                  pl.BlockSpec(cw["wv"].shape, lambda i, pt: (0, 0))] + [page_spec(j) for j in range(n_pages)],
        out_specs=pl.BlockSpec((None, nh, nh * vd), lambda i, pt: (i, 0, 0)),
        scratch_shapes=[pltpu.VMEM((n_pages // ppi, kv_rank, ppi * page), BF16),
                        pltpu.VMEM((n_pages // ppi, rope, ppi * page), F32),
                        pltpu.VMEM((n_pages // ppi, nh, ppi * page), F32)])
    om = pl.pallas_call(
        functools.partial(_mla_sample_kernel, n_pages=n_pages, page=page, kv_rank=kv_rank, rope=rope, qk=qk, nh=nh,
                          ppi=ppi),
        grid_spec=grid_spec,
        out_shape=jax.ShapeDtypeStruct((s, nh, nh * vd), F32),
        compiler_params=_cparams("parallel"),
        name="mla_sample",
    )(page_table, qn, qr, new, cw["wkt"], cw["wv"], *([cache] * n_pages))
    idx = jnp.arange(nh)
    o = om.reshape(s, nh, nh, vd)[:, idx, idx, :]
    return jnp.swapaxes(o, 0, 1).astype(BF16)


def _ssd_step_kernel(xbc_ref, z_ref, dt_ref, buf_ref, h0_ref, cw_ref, cb_ref, dtb_ref, alog_ref, dsk_ref, ng_ref,
                     y_ref, hnew_ref, nbuf_ref, xc_scr, xct_scr, dtt_scr, dat_scr, yoff_scr,
                     *, nh, hd, ns, n_groups):
    r = pl.program_id(0)
    inner = nh * hd
    hpg = nh // n_groups
    lane = lax.broadcasted_iota(jnp.int32, (1, LANES), 1)

    @pl.when(r == 0)
    def _():
        xc = _silu(_conv_step(xbc_ref[...], buf_ref, nbuf_ref, cw_ref, cb_ref))
        xc_scr[...] = xc
        xct_scr[...] = xc.T
        dtv = _softplus(dt_ref[...] + dtb_ref[...])
        a = jnp.where(lane < nh, -jnp.exp(alog_ref[...]), 0.0)
        dtt_scr[...] = dtv.T
        dat_scr[...] = jnp.exp(dtv * a).T

    g = r // hpg
    xt = xct_scr[pl.ds(pl.multiple_of(r * hd, hd), hd), :]
    bt = xct_scr[pl.ds(pl.multiple_of(inner + g * ns, ns), ns), :]
    ct = xct_scr[pl.ds(pl.multiple_of(inner + (n_groups + g) * ns, ns), ns), :]
    dar = dat_scr[pl.ds(r, 1), :]
    coef = xt * dtt_scr[pl.ds(r, 1), :]
    for p in range(hd):
        h0p = h0_ref[p * ns:(p + 1) * ns, :]
        hnew_ref[p * ns:(p + 1) * ns, :] = dar * h0p + coef[p:p + 1, :] * bt
        yoff_scr[pl.ds(r * hd + p, 1), :] = jnp.sum(ct * h0p, axis=0, keepdims=True)

    @pl.when(r == nh - 1)
    def _():
        xc = xc_scr[...]
        xs = xc[:, :inner]
        dtv = dtt_scr[...].T
        da = dat_scr[...].T
        per_head = lambda v: jnp.concatenate(
            [jnp.broadcast_to(v[:, i:i + 1], (v.shape[0], hd)) for i in range(nh)], axis=1)
        cb = []
        for gi in range(n_groups):
            bmg = xc[:, inner + gi * ns:inner + (gi + 1) * ns]
            cmg = xc[:, inner + (n_groups + gi) * ns:inner + (n_groups + gi + 1) * ns]
            cb.append(jnp.broadcast_to(jnp.sum(cmg * bmg, axis=-1, keepdims=True), (xc.shape[0], hpg * hd)))
        y = jnp.concatenate(cb, axis=1) * per_head(dtv) * xs + yoff_scr[...].T * per_head(da) + dsk_ref[...] * xs
        _gated_group_norm(y, z_ref[...], ng_ref, y_ref, n_groups)


def _ssd_step(xbc, z, dt, buf, h0, conv_w, conv_b, cw, norm_g):
    s, conv_dim = xbc.shape
    _, nh, hd, ns = h0.shape
    inner = cw["inner"]
    n_groups = (conv_dim - inner) // (2 * ns)
    sz = hd * ns
    h0t = jnp.transpose(h0, (1, 2, 3, 0)).reshape(nh * sz, s)
    y, hnew, nbuf = pl.pallas_call(
        functools.partial(_ssd_step_kernel, nh=nh, hd=hd, ns=ns, n_groups=n_groups),
        grid=(nh,),
        in_specs=[_full((s, conv_dim)), _full((s, inner)), _full((s, LANES)), _full((CONV_WIDTH - 1, s, conv_dim)),
                  pl.BlockSpec((sz, s), lambda r: (r, 0)), _full((CONV_WIDTH, conv_dim)), _full((1, conv_dim)),
                  _full((1, LANES)), _full((1, LANES)), _full((1, inner)), _full((1, inner))],
        out_specs=[_full((s, inner)), pl.BlockSpec((sz, s), lambda r: (r, 0)),
                   _full((CONV_WIDTH - 1, s, conv_dim))],
        out_shape=[jax.ShapeDtypeStruct((s, inner), BF16), jax.ShapeDtypeStruct((nh * sz, s), F32),
                   jax.ShapeDtypeStruct((CONV_WIDTH - 1, s, conv_dim), F32)],
        scratch_shapes=[pltpu.VMEM((s, conv_dim), F32), pltpu.VMEM((conv_dim, s), F32), pltpu.VMEM((LANES, s), F32),
                        pltpu.VMEM((LANES, s), F32), pltpu.VMEM((inner, s), F32)],
        compiler_params=_cparams("arbitrary"),
        name="ssd_step",
    )(xbc, z, dt, buf, h0t, conv_w, conv_b.reshape(1, conv_dim), cw["dtb128"], cw["alog128"],
      cw["dsk"], norm_g.reshape(1, inner))
    return y, jnp.transpose(hnew.reshape(nh, hd, ns, s), (3, 0, 1, 2)), nbuf


def _prompt_tiles(t, hidden):
    pick = lambda want: max(c for c in (8, 16, 32, 64, 128, 256, 512, 1024) if c <= want and t % c == 0)
    th = hidden // 2 if hidden % (2 * LANES) == 0 else hidden
    return dict(tm_proj=pick(512), tm_post=pick(512), tq_sb=pick(256), tq_mla=pick(512), tt_lru=pick(512),
                ssd_chunk=pick(128), th=th)


def kernel(x_prompt, x_sample, cache_sb_kv, cache_mla_kv, page_table, state_lru_h, state_lru_conv, state_ssm_h, state_ssm_conv, norm_mix, norm_ffn, ab_w_in, ab_w_out, sb_q_gain, sb_k_gain, lru_conv_w, lru_conv_b, lru_wa, lru_ba, lru_wx, lru_bx, lru_lambda, cd_w_in, cd_w_out, mla_q_lat_gain, mla_w_uq, mla_kv_lat_gain, mla_w_ukv, mla_q_gain, mla_k_gain, ssd_conv_w, ssd_conv_b, ssd_dt_bias, ssd_a_log, ssd_d, ssd_norm_gain, ffn_w_in, ffn_w_out):
    nb, t, d = x_prompt.shape
    ns = x_sample.shape[0]
    depth = norm_mix.shape[0]
    page = cache_sb_kv.shape[2]
    past = page_table.shape[1] * page
    n_kv, hd = cache_sb_kv.shape[4], cache_sb_kv.shape[5]
    n_q = (ab_w_in.shape[2] - 2 * n_kv * hd - 2 * lru_lambda.shape[1]) // hd
    w_lru = lru_lambda.shape[1]
    sizes = _prompt_tiles(t, ffn_w_out.shape[1])

    xp = x_prompt.reshape(nb * t, d)
    xs = x_sample.reshape(ns, d)
    cache_sb = jnp.transpose(cache_sb_kv, (0, 1, 3, 4, 5, 2)).reshape(cache_sb_kv.shape[:2] + (2 * n_kv * hd, page))
    cache_mla = jnp.transpose(cache_mla_kv, (0, 1, 3, 2))
    cos_p, sin_p = _rope_tables(jnp.arange(t), mla_q_gain.shape[1] - (cache_mla_kv.shape[3] - mla_kv_lat_gain.shape[1]),
                                cache_mla_kv.shape[3] - mla_kv_lat_gain.shape[1])
    cos_s, sin_s = _rope_tables(jnp.full((ns,), past), mla_q_gain.shape[1] - (cache_mla_kv.shape[3] - mla_kv_lat_gain.shape[1]),
                                cache_mla_kv.shape[3] - mla_kv_lat_gain.shape[1])
    outs = {k: [] for k in ("sb_p", "sb_s", "lh_p", "lh_s", "lc_p", "lc_s", "ml_p", "ml_s", "sh_p", "sh_s", "sc_p", "sc_s")}
    for li in range(depth):
        w_ffn_in = ffn_w_in[li].astype(BF16)
        w_ffn_out = ffn_w_out[li].astype(BF16)
        if li % 2 == 0:
            e = li // 2
            w_in = ab_w_in[e].astype(BF16)
            w_out = ab_w_out[e].astype(BF16)
            wbd, bias = _lru_weights(lru_wa[e], lru_ba[e], lru_wx[e], lru_bx[e])
            qh, kh, vh, kv, xb, gate = _ab_inproj(xp, norm_mix[li], w_in, sb_q_gain[e], sb_k_gain[e],
                                                  nq=n_q, nk=n_kv, hd=hd, w_lru=w_lru, tm=sizes["tm_proj"])
            oh = _sb_prompt(qh, kh, vh, nb=nb, tq=sizes["tq_sb"])
            y2, hl, cbuf = _rglru_prompt(xb, gate, lru_conv_w[e], lru_conv_b[e], wbd, bias, lru_lambda[e],
                                         nb=nb, tt=sizes["tt_lru"])
            xp = _post(xp, oh, y2, w_out, norm_ffn[li], w_ffn_in, w_ffn_out, tm=sizes["tm_post"], th=sizes["th"])
            outs["sb_p"].append(kv.reshape(nb, t, 2, n_kv, hd))
            outs["lh_p"].append(hl.reshape(nb, w_lru))
            outs["lc_p"].append(cbuf)
            qh, kh, vh, kv, xb, gate = _ab_inproj(xs, norm_mix[li], w_in, sb_q_gain[e], sb_k_gain[e],
                                                  nq=n_q, nk=n_kv, hd=hd, w_lru=w_lru, tm=ns)
            oh = _sb_sample(qh, cache_sb, e, page_table, n_kv=n_kv)
            y2, hl, cbuf = _rglru_step(xb, gate, jnp.swapaxes(state_lru_conv[e], 0, 1), state_lru_h[e],
                                       lru_conv_w[e], lru_conv_b[e], wbd, bias, lru_lambda[e])
            xs = _post(xs, oh, y2, w_out, norm_ffn[li], w_ffn_in, w_ffn_out, tm=ns, th=sizes["th"])
            outs["sb_s"].append(kv.reshape(ns, 1, 2, n_kv, hd))
            outs["lh_s"].append(hl)
            outs["lc_s"].append(jnp.swapaxes(cbuf, 0, 1))
        else:
            o = li // 2
            cw = _cd_weights(cd_w_in[o], mla_w_uq[o], mla_w_ukv[o], mla_q_gain[o], mla_k_gain[o], ssd_dt_bias[o],
                             ssd_a_log[o], ssd_d[o], kv_rank=mla_kv_lat_gain.shape[1], q_rank=mla_q_lat_gain.shape[1],
                             rope=cache_mla_kv.shape[3] - mla_kv_lat_gain.shape[1], inner=ssd_norm_gain.shape[1],
                             conv_dim=ssd_conv_w.shape[2])
            w_out = cd_w_out[o].astype(BF16)
            qh, kh, vh, lat, z, xbc, dt = _cd_inproj(xp, norm_mix[li], cw, mla_q_lat_gain[o], mla_kv_lat_gain[o],
                                                     cos_p, sin_p, tm=sizes["tm_proj"])
            oh = _mla_prompt(qh, kh, vh, nb=nb, tq=sizes["tq_mla"])
            y2, hl, cbuf = _ssd_prompt(xbc, z, dt, ssd_conv_w[o], ssd_conv_b[o], cw, ssd_norm_gain[o],
                                       nb=nb, q=sizes["ssd_chunk"], state_dim=state_ssm_h.shape[4])
            xp = _post(xp, oh, y2, w_out, norm_ffn[li], w_ffn_in, w_ffn_out, tm=sizes["tm_post"], th=sizes["th"])
            outs["ml_p"].append(lat.reshape(nb, t, lat.shape[1]))
            outs["sh_p"].append(hl)
            outs["sc_p"].append(cbuf)
            qh, kh, vh, lat, z, xbc, dt = _cd_inproj(xs, norm_mix[li], cw, mla_q_lat_gain[o], mla_kv_lat_gain[o],
                                                     cos_s, sin_s, tm=ns)
            oh = _mla_sample(qh, lat, cache_mla, o, page_table, cw)
            y2, hl, cbuf = _ssd_step(xbc, z, dt, jnp.swapaxes(state_ssm_conv[o], 0, 1), state_ssm_h[o],
                                     ssd_conv_w[o], ssd_conv_b[o], cw, ssd_norm_gain[o])
            xs = _post(xs, oh, y2, w_out, norm_ffn[li], w_ffn_in, w_ffn_out, tm=ns, th=sizes["th"])
            outs["ml_s"].append(lat.reshape(ns, 1, lat.shape[1]))
            outs["sh_s"].append(hl)
            outs["sc_s"].append(jnp.swapaxes(cbuf, 0, 1))
    st = {k: jnp.stack(v) for k, v in outs.items()}
    return (xp.reshape(nb, t, d), xs.reshape(ns, 1, d), st["sb_p"], st["sb_s"], st["lh_p"], st["lh_s"], st["lc_p"],
            st["lc_s"], st["ml_p"], st["ml_s"], st["sh_p"], st["sh_s"], st["sc_p"], st["sc_s"])
```
